```python
import math
import jax, jax.numpy as jnp
from jax import lax
import numpy as np

D_MODEL = 4096
BATCH = 8
SEQ = 2048
DEPTH = 1
DEC_BATCH = 16
DEC_SEQ = 32
PAST_LEN = 4096

CHUNK = 64
Q_BLOCK = 128
HD_FOX = 128
H_FOX = (D_MODEL // 2) // HD_FOX
FOX_W = H_FOX * HD_FOX
DK_RET = 128
DV_RET = 256
H_RET = (D_MODEL // 2) // DV_RET
RET_QK_W = H_RET * DK_RET
RET_V_W = H_RET * DV_RET
MIX_W = FOX_W + RET_V_W
N_IN = 3 * FOX_W + H_FOX + 2 * RET_QK_W + 2 * RET_V_W
D_FF = 4 * D_MODEL
ROPE_BASE = 10000.0
EPS = 1e-6

kernel_name = "hymba_fox_retnet_streaming_step"


def rmsnorm(x, g):
    xf = x.astype(jnp.float32)
    xf = xf * lax.rsqrt(jnp.mean(xf * xf, axis=-1, keepdims=True) + EPS)
    return (xf * g.astype(jnp.float32)).astype(x.dtype)


def rotary(x, pos):
    half = x.shape[-1] // 2
    inv = ROPE_BASE ** (-jnp.arange(half, dtype=jnp.float32) / half)
    ang = pos.astype(jnp.float32)[:, None] * inv[None, :]
    cos = jnp.cos(ang)[None, :, None, :]
    sin = jnp.sin(ang)[None, :, None, :]
    xf = x.astype(jnp.float32)
    x1, x2 = xf[..., :half], xf[..., half:]
    return jnp.concatenate([x1 * cos - x2 * sin, x1 * sin + x2 * cos], axis=-1)


def retention_log_gamma():
    return jnp.log1p(-jnp.exp2(-5.0 - jnp.arange(H_RET, dtype=jnp.float32)))


def split_projection(h, w_in, b_forget):
    B, T, _ = h.shape
    cols = jnp.einsum('btd,dn->btn', h, w_in)
    o1 = FOX_W; o2 = 2 * FOX_W; o3 = 3 * FOX_W; o4 = o3 + H_FOX
    o5 = o4 + RET_QK_W; o6 = o5 + RET_QK_W; o7 = o6 + RET_V_W
    q_f = cols[..., :o1].reshape(B, T, H_FOX, HD_FOX)
    k_f = cols[..., o1:o2].reshape(B, T, H_FOX, HD_FOX)
    v_f = cols[..., o2:o3].reshape(B, T, H_FOX, HD_FOX)
    logf = jax.nn.log_sigmoid((cols[..., o3:o4] + b_forget).astype(jnp.float32))
    q_r = cols[..., o4:o5].reshape(B, T, H_RET, DK_RET)
    k_r = cols[..., o5:o6].reshape(B, T, H_RET, DK_RET)
    v_r = cols[..., o6:o7].reshape(B, T, H_RET, DV_RET)
    g_r = cols[..., o7:]
    return q_f, k_f, v_f, logf, q_r, k_r, v_r, g_r


def fox_block(q, c_q, pos_q, k, v, c_k, pos_k):
    s = jnp.einsum('bqhd,bkhd->bhqk', q, k).astype(jnp.float32) * (HD_FOX ** -0.5)
    s = s + jnp.transpose(c_q, (0, 2, 1))[:, :, :, None] - jnp.transpose(c_k, (0, 2, 1))[:, :, None, :]
    mask = pos_k[None, :] <= pos_q[:, None]
    s = jnp.where(mask[None, None], s, -jnp.inf)
    p = jax.nn.softmax(s, axis=-1).astype(v.dtype)
    return jnp.einsum('bhqk,bkhd->bqhd', p, v)


def retention_chunk(q, k, v, S, log_gamma):
    C = q.shape[1]
    idx = jnp.arange(C, dtype=jnp.float32)
    D = jnp.exp(log_gamma[:, None, None] * jnp.abs(idx[:, None] - idx[None, :]))
    A = jnp.einsum('bihd,bjhd->bhij', q, k) * D[None]
    o = jnp.einsum('bhij,bjhe->bihe', A, v)
    cross = jnp.exp(log_gamma[None, :] * (idx[:, None] + 1.0))
    o = o + jnp.einsum('bihd,bhde->bihe', q, S) * cross[None, :, :, None]
    k_dec = jnp.exp(log_gamma[None, :] * (C - 1.0 - idx)[:, None])
    S_new = jnp.exp(log_gamma * C)[None, :, None, None] * S + jnp.einsum(
        'bjhd,bjhe->bhde', k * k_dec[None, :, :, None], v)
    return o, S_new


def retention_inputs(q_r, k_r, v_r, pos):
    q = rotary(q_r, pos)
    k = rotary(k_r, pos) * (DK_RET ** -0.5)
    return q, k, v_r.astype(jnp.float32)


def merge_groups(o_fox, o_ret, g_r, g_fox_out, g_ret_out, w_out, dtype):
    B, T = o_fox.shape[:2]
    fox = rmsnorm(o_fox.reshape(B, T, FOX_W), g_fox_out).astype(dtype)
    r = o_ret.astype(jnp.float32)
    r = r * lax.rsqrt(jnp.mean(r * r, axis=-1, keepdims=True) + EPS)
    r = r.reshape(B, T, RET_V_W) * g_ret_out.astype(jnp.float32)
    r = (r * jax.nn.silu(g_r.astype(jnp.float32))).astype(dtype)
    return jnp.einsum('btm,md->btd', jnp.concatenate([fox, r], axis=-1), w_out)


def squared_relu_mlp(x, g_mlp, w_up, w_down):
    h = rmsnorm(x, g_mlp)
    u = jnp.maximum(jnp.einsum('btd,df->btf', h, w_up), 0)
    return x + jnp.einsum('btf,fd->btd', u * u, w_down)


def layer_prompt(x, g_attn, w_in, b_forget, g_fox_out, g_ret_out, w_out, g_mlp, w_up, w_down):
    B, T, _ = x.shape
    h = rmsnorm(x, g_attn)
    q_f, k_f, v_f, logf, q_r, k_r, v_r, g_r = split_projection(h, w_in, b_forget)
    pos = jnp.arange(T)
    c = jnp.cumsum(logf, axis=1)
    nb = T // Q_BLOCK
    qb = jnp.swapaxes(q_f.reshape(B, nb, Q_BLOCK, H_FOX, HD_FOX), 0, 1)
    cb = jnp.swapaxes(c.reshape(B, nb, Q_BLOCK, H_FOX), 0, 1)
    pb = pos.reshape(nb, Q_BLOCK)
    o_fox = lax.map(lambda a: fox_block(a[0], a[1], a[2], k_f, v_f, c, pos), (qb, cb, pb))
    o_fox = jnp.swapaxes(o_fox, 0, 1).reshape(B, T, H_FOX, HD_FOX)
    q, k, v = retention_inputs(q_r, k_r, v_r, pos)
    lg = retention_log_gamma()
    nc = T // CHUNK
    xs = (jnp.swapaxes(q.reshape(B, nc, CHUNK, H_RET, DK_RET), 0, 1),
          jnp.swapaxes(k.reshape(B, nc, CHUNK, H_RET, DK_RET), 0, 1),
          jnp.swapaxes(v.reshape(B, nc, CHUNK, H_RET, DV_RET), 0, 1))

    def body(S, inp):
        o, S_new = retention_chunk(inp[0], inp[1], inp[2], S, lg)
        return S_new, o

    S0 = jnp.zeros((B, H_RET, DK_RET, DV_RET), jnp.float32)
    S_fin, o_ret = lax.scan(body, S0, xs)
    o_ret = jnp.swapaxes(o_ret, 0, 1).reshape(B, T, H_RET, DV_RET)
    x = x + merge_groups(o_fox, o_ret, g_r, g_fox_out, g_ret_out, w_out, x.dtype)
    x = squared_relu_mlp(x, g_mlp, w_up, w_down)
    return x, k_f, v_f, logf.astype(x.dtype), S_fin.astype(x.dtype)


def layer_sample(x, ck, cv, clogf, S, g_attn, w_in, b_forget, g_fox_out, g_ret_out, w_out, g_mlp, w_up, w_down):
    B, T, _ = x.shape
    P = ck.shape[1]
    h = rmsnorm(x, g_attn)
    q_f, k_f, v_f, logf, q_r, k_r, v_r, g_r = split_projection(h, w_in, b_forget)
    pos_q = P + jnp.arange(T)
    k_all = jnp.concatenate([ck.astype(k_f.dtype), k_f], axis=1)
    v_all = jnp.concatenate([cv.astype(v_f.dtype), v_f], axis=1)
    c_all = jnp.cumsum(jnp.concatenate([clogf.astype(jnp.float32), logf], axis=1), axis=1)
    o_fox = fox_block(q_f, c_all[:, P:], pos_q, k_all, v_all, c_all, jnp.arange(P + T))
    q, k, v = retention_inputs(q_r, k_r, v_r, pos_q)
    o_ret, S_new = retention_chunk(q, k, v, S.astype(jnp.float32), retention_log_gamma())
    x = x + merge_groups(o_fox, o_ret, g_r, g_fox_out, g_ret_out, w_out, x.dtype)
    x = squared_relu_mlp(x, g_mlp, w_up, w_down)
    return x, k_f, v_f, logf.astype(x.dtype), S_new.astype(x.dtype)


def setup_inputs(seed: int = 0) -> dict:
    key = jax.random.key(seed)
    ks = jax.random.split(key, 18)
    f32 = jnp.float32
    nrm = lambda k, s: jax.random.normal(k, s, f32)
    return {
        "x_prompt": nrm(ks[0], (BATCH, SEQ, D_MODEL)),
        "x_sample": nrm(ks[1], (DEC_BATCH, DEC_SEQ, D_MODEL)),
        "cache_fox_k": nrm(ks[2], (DEPTH, DEC_BATCH, PAST_LEN, H_FOX, HD_FOX)),
        "cache_fox_v": nrm(ks[3], (DEPTH, DEC_BATCH, PAST_LEN, H_FOX, HD_FOX)),
        "cache_fox_logf": jax.nn.log_sigmoid(2.0 + nrm(ks[4], (DEPTH, DEC_BATCH, PAST_LEN, H_FOX))),
        "state_ret": nrm(ks[5], (DEPTH, DEC_BATCH, H_RET, DK_RET, DV_RET)),
        "g_attn": 1.0 + 0.02 * nrm(ks[6], (DEPTH, D_MODEL)),
        "w_in": nrm(ks[7], (DEPTH, D_MODEL, N_IN)) * D_MODEL ** -0.5,
        "b_forget": 2.0 + 0.2 * nrm(ks[8], (DEPTH, H_FOX)),
        "g_fox_out": 1.0 + 0.02 * nrm(ks[9], (DEPTH, FOX_W)),
        "g_ret_out": 1.0 + 0.02 * nrm(ks[10], (DEPTH, RET_V_W)),
        "w_out": nrm(ks[11], (DEPTH, MIX_W, D_MODEL)) * MIX_W ** -0.5,
        "g_mlp": 1.0 + 0.02 * nrm(ks[12], (DEPTH, D_MODEL)),
        "w_up": nrm(ks[13], (DEPTH, D_MODEL, D_FF)) * D_MODEL ** -0.5,
        "w_down": nrm(ks[14], (DEPTH, D_FF, D_MODEL)) * D_FF ** -0.5,
        "g_final": 1.0 + 0.02 * nrm(ks[15], (D_MODEL,)),
    }


def reference(x_prompt, x_sample, cache_fox_k, cache_fox_v, cache_fox_logf, state_ret,
              g_attn, w_in, b_forget, g_fox_out, g_ret_out, w_out, g_mlp, w_up, w_down, g_final):
    yp, ys = x_prompt, x_sample
    kp, vp, lp, sp, kss, vss, lss, sss = [], [], [], [], [], [], [], []
    for l in range(DEPTH):
        yp, k1, v1, lf1, s1 = layer_prompt(yp, g_attn[l], w_in[l], b_forget[l], g_fox_out[l], g_ret_out[l],
                                           w_out[l], g_mlp[l], w_up[l], w_down[l])
        ys, k2, v2, lf2, s2 = layer_sample(ys, cache_fox_k[l], cache_fox_v[l], cache_fox_logf[l], state_ret[l],
                                           g_attn[l], w_in[l], b_forget[l], g_fox_out[l], g_ret_out[l],
                                           w_out[l], g_mlp[l], w_up[l], w_down[l])
        kp.append(k1); vp.append(v1); lp.append(lf1); sp.append(s1)
        kss.append(k2); vss.append(v2); lss.append(lf2); sss.append(s2)
    y_prompt = rmsnorm(yp, g_final)
    y_sample = rmsnorm(ys, g_final)
    new_fox_k_p = jnp.stack(kp); new_fox_v_p = jnp.stack(vp)
    new_fox_logf_p = jnp.stack(lp); new_ret_state_p = jnp.stack(sp)
    new_fox_k_s = jnp.stack(kss); new_fox_v_s = jnp.stack(vss)
    new_fox_logf_s = jnp.stack(lss); new_ret_state_s = jnp.stack(sss)
    return (y_prompt, y_sample, new_fox_k_p, new_fox_v_p, new_fox_logf_p, new_ret_state_p,
            new_fox_k_s, new_fox_v_s, new_fox_logf_s, new_ret_state_s)
```

```python
import functools

import jax
import jax.numpy as jnp
from jax import lax
from jax.experimental import pallas as pl
from jax.experimental.pallas import tpu as pltpu

CHUNK = 64
HD_FOX = 128
DK_RET = 128
DV_RET = 256
ROPE_BASE = 10000.0
EPS = 1e-6

LANES = 128
VMEM_LIMIT_BYTES = 56 * 1024 * 1024

F32 = jnp.float32
BF16 = jnp.bfloat16


def _params(*sem):
    return pltpu.CompilerParams(dimension_semantics=sem, vmem_limit_bytes=VMEM_LIMIT_BYTES)


def _tile(n, pref, mult=LANES):
    if n <= pref:
        return n
    t = (pref // mult) * mult
    while t > mult and n % t:
        t -= mult
    assert n % t == 0, (n, pref)
    return t


def _rms_kernel(x_ref, g_ref, o_ref):
    x = x_ref[...]
    ms = jnp.mean(x * x, axis=-1, keepdims=True)
    o_ref[...] = (x * lax.rsqrt(ms + EPS) * g_ref[...]).astype(o_ref.dtype)


def _rmsnorm(x, g, out_dtype):
    m, d = x.shape
    tr = _tile(m, 256, 8)
    return pl.pallas_call(
        _rms_kernel,
        grid=(m // tr,),
        in_specs=[pl.BlockSpec((tr, d), lambda i: (i, 0)),
                  pl.BlockSpec((1, d), lambda i: (0, 0))],
        out_specs=pl.BlockSpec((tr, d), lambda i: (i, 0)),
        out_shape=jax.ShapeDtypeStruct((m, d), out_dtype),
        compiler_params=_params("arbitrary"),
        name="rmsnorm",
    )(x, g.reshape(1, d))


def _log_sigmoid(z):
    return jnp.minimum(z, 0.0) - jnp.log(1.0 + jnp.exp(-jnp.abs(z)))


def _mm_kernel(a_ref, w_ref, *rest, epilogue):
    acc = jnp.dot(a_ref[...], w_ref[...], preferred_element_type=F32)
    if epilogue == "logsig":
        b_ref, o_ref = rest
        o_ref[...] = _log_sigmoid(acc + b_ref[...])
    elif epilogue == "relu2":
        (o_ref,) = rest
        u = jnp.maximum(acc, 0.0)
        o_ref[...] = (u * u).astype(o_ref.dtype)
    else:
        (o_ref,) = rest
        o_ref[...] = acc.astype(o_ref.dtype)


def _matmul(a, w, out_dtype, epilogue="cast", bias=None, tm_pref=1024, tn_pref=1024):
    m, k = a.shape
    n = w.shape[1]
    tm = _tile(m, tm_pref, 16)
    tn = _tile(n, tn_pref)
    in_specs = [pl.BlockSpec((tm, k), lambda i, j: (i, 0)),
                pl.BlockSpec((k, tn), lambda i, j: (0, j))]
    args = [a, w]
    if bias is not None:
        in_specs.append(pl.BlockSpec((1, tn), lambda i, j: (0, j)))
        args.append(bias)
    return pl.pallas_call(
        functools.partial(_mm_kernel, epilogue=epilogue),
        grid=(m // tm, n // tn),
        in_specs=in_specs,
        out_specs=pl.BlockSpec((tm, tn), lambda i, j: (i, j)),
        out_shape=jax.ShapeDtypeStruct((m, n), out_dtype),
        compiler_params=_params("arbitrary", "arbitrary"),
        name="matmul_" + epilogue,
    )(*args)


def _mix_mm_kernel(o_ref, gf_ref, r_ref, w_ref, res_ref, out_ref, a_scr, *, fw):
    @pl.when(pl.program_id(1) == 0)
    def _():
        o = o_ref[...]
        ms = jnp.mean(o * o, axis=-1, keepdims=True)
        a_scr[:, :fw] = (o * lax.rsqrt(ms + EPS) * gf_ref[...]).astype(BF16)
        a_scr[:, fw:] = r_ref[...]

    out_ref[...] = res_ref[...] + jnp.dot(a_scr[...], w_ref[...], preferred_element_type=F32)


def _out_proj(o_fox, g_fox, r, w, res):
    m, fw = o_fox.shape
    rw = r.shape[1]
    n = w.shape[1]
    tm = _tile(m, 512, 16)
    tn = _tile(n, 1024)
    return pl.pallas_call(
        functools.partial(_mix_mm_kernel, fw=fw),
        grid=(m // tm, n // tn),
        in_specs=[pl.BlockSpec((tm, fw), lambda i, j: (i, 0)),
                  pl.BlockSpec((1, fw), lambda i, j: (0, 0)),
                  pl.BlockSpec((tm, rw), lambda i, j: (i, 0)),
                  pl.BlockSpec((fw + rw, tn), lambda i, j: (0, j)),
                  pl.BlockSpec((tm, tn), lambda i, j: (i, j))],
        out_specs=pl.BlockSpec((tm, tn), lambda i, j: (i, j)),
        out_shape=jax.ShapeDtypeStruct((m, n), F32),
        scratch_shapes=[pltpu.VMEM((tm, fw + rw), BF16)],
        compiler_params=_params("arbitrary", "arbitrary"),
        name="out_proj",
    )(o_fox, g_fox.reshape(1, fw), r, w, res)


def _mm_acc_kernel(a_ref, w_ref, res_ref, out_ref):
    part = jnp.dot(a_ref[...], w_ref[...], preferred_element_type=F32)

    @pl.when(pl.program_id(2) == 0)
    def _():
        out_ref[...] = res_ref[...] + part

    @pl.when(pl.program_id(2) != 0)
    def _():
        out_ref[...] += part


def _matmul_residual_ksplit(a, w, res):
    m, k = a.shape
    n = w.shape[1]
    tm = _tile(m, 1024, 16)
    tn = _tile(n, 1024)
    tk = _tile(k, 2048)
    return pl.pallas_call(
        _mm_acc_kernel,
        grid=(m // tm, n // tn, k // tk),
        in_specs=[pl.BlockSpec((tm, tk), lambda i, j, kk: (i, kk)),
                  pl.BlockSpec((tk, tn), lambda i, j, kk: (kk, j)),
                  pl.BlockSpec((tm, tn), lambda i, j, kk: (i, j))],
        out_specs=pl.BlockSpec((tm, tn), lambda i, j, kk: (i, j)),
        out_shape=jax.ShapeDtypeStruct((m, n), F32),
        compiler_params=_params("arbitrary", "arbitrary", "arbitrary"),
        name="down_proj",
    )(a, w, res)


def _cumsum_kernel(x_ref, o_ref):
    x = x_ref[...]
    t = x.shape[1]
    lane = lax.broadcasted_iota(jnp.int32, x.shape, 1)
    d = 1
    while d < t:
        x = x + jnp.where(lane >= d, pltpu.roll(x, d, axis=1), 0.0)
        d *= 2
    o_ref[...] = x


def _cumsum_lanes(x):
    b, h, t = x.shape
    return pl.pallas_call(
        _cumsum_kernel,
        grid=(b,),
        in_specs=[pl.BlockSpec((None, h, t), lambda i: (i, 0, 0))],
        out_specs=pl.BlockSpec((None, h, t), lambda i: (i, 0, 0)),
        out_shape=jax.ShapeDtypeStruct((b, h, t), F32),
        compiler_params=_params("arbitrary"),
        name="cumsum_logf",
    )(x)


def _softmax_step(s, v_bf16, m_prev, l_prev, acc_prev):
    m_new = jnp.maximum(m_prev, jnp.max(s, axis=1, keepdims=True))
    alpha = jnp.exp(m_prev - m_new)
    p = jnp.exp(s - m_new)
    l_new = alpha * l_prev + jnp.sum(p, axis=1, keepdims=True)
    acc_new = alpha * acc_prev + jnp.dot(p.astype(BF16), v_bf16, preferred_element_type=F32)
    return m_new, l_new, acc_new


def _qk(q_bf16, k_bf16):
    return lax.dot_general(q_bf16, k_bf16, (((1,), (1,)), ((), ())), preferred_element_type=F32)


def _fox_prompt_kernel(q_ref, k_ref, v_ref, cq_ref, ck_ref, o_ref, m_scr, l_scr, acc_scr, *, tq, tk):
    h = pl.program_id(1)
    qi = pl.program_id(2)
    ki = pl.program_id(3)

    @pl.when(ki == 0)
    def _():
        m_scr[...] = jnp.full(m_scr.shape, -jnp.inf, F32)
        l_scr[...] = jnp.zeros(l_scr.shape, F32)
        acc_scr[...] = jnp.zeros(acc_scr.shape, F32)

    @pl.when(ki * tk <= qi * tq + (tq - 1))
    def _():
        s = _qk(q_ref[...], k_ref[...].astype(BF16)) * (HD_FOX ** -0.5)
        cq_all = cq_ref[...]
        head = lax.broadcasted_iota(jnp.int32, cq_all.shape, 1)
        cq = jnp.sum(jnp.where(head == h, cq_all, 0.0), axis=1, keepdims=True)
        s = s + cq - ck_ref[...]
        row = lax.broadcasted_iota(jnp.int32, s.shape, 0) + qi * tq
        col = lax.broadcasted_iota(jnp.int32, s.shape, 1) + ki * tk
        s = jnp.where(col <= row, s, -jnp.inf)
        m_new, l_new, acc_new = _softmax_step(s, v_ref[...].astype(BF16), m_scr[...], l_scr[...], acc_scr[...])
        m_scr[...] = m_new
        l_scr[...] = l_new
        acc_scr[...] = acc_new

    @pl.when(ki == pl.num_programs(3) - 1)
    def _():
        o_ref[...] = acc_scr[...] / l_scr[...]


def _fox_prompt(q, k, v, c_col, c_row, batch, seq, heads):
    m, fw = k.shape
    tq = _tile(seq, 512)
    tk = tq
    nq = seq // tq
    nk = seq // tk

    def kv_idx(b, h, qi, ki):
        return (b * nk + jnp.minimum(ki, qi), h)

    return pl.pallas_call(
        functools.partial(_fox_prompt_kernel, tq=tq, tk=tk),
        grid=(batch, heads, nq, nk),
        in_specs=[pl.BlockSpec((tq, HD_FOX), lambda b, h, qi, ki: (b * nq + qi, h)),
                  pl.BlockSpec((tk, HD_FOX), kv_idx),
                  pl.BlockSpec((tk, HD_FOX), kv_idx),
                  pl.BlockSpec((tq, heads), lambda b, h, qi, ki: (b * nq + qi, 0)),
                  pl.BlockSpec((None, 1, tk), lambda b, h, qi, ki: (b * heads + h, 0, jnp.minimum(ki, qi)))],
        out_specs=pl.BlockSpec((tq, HD_FOX), lambda b, h, qi, ki: (b * nq + qi, h)),
        out_shape=jax.ShapeDtypeStruct((m, fw), F32),
        scratch_shapes=[pltpu.VMEM((tq, 1), F32), pltpu.VMEM((tq, 1), F32), pltpu.VMEM((tq, HD_FOX), F32)],
        compiler_params=_params("arbitrary", "arbitrary", "arbitrary", "arbitrary"),
        name="fox_prompt",
    )(q, k, v, c_col, c_row)


def _fox_sample_kernel(q_ref, kc_ref, vc_ref, kn_ref, vn_ref, cq_ref, ckc_ref, ckn_ref, o_ref,
                       m_scr, l_scr, acc_scr, *, heads):
    ki = pl.program_id(1)
    last = pl.num_programs(1) - 1
    scale = HD_FOX ** -0.5

    @pl.when(ki == 0)
    def _():
        m_scr[...] = jnp.full(m_scr.shape, -jnp.inf, F32)
        l_scr[...] = jnp.zeros(l_scr.shape, F32)
        acc_scr[...] = jnp.zeros(acc_scr.shape, F32)

    def update(k_ref, v_ref, ck_ref, causal):
        for h in range(heads):
            cols = slice(h * HD_FOX, (h + 1) * HD_FOX)
            s = _qk(q_ref[:, cols], k_ref[:, cols].astype(BF16)) * scale
            s = s + cq_ref[:, h:h + 1] - ck_ref[h:h + 1, :]
            if causal:
                row = lax.broadcasted_iota(jnp.int32, s.shape, 0)
                col = lax.broadcasted_iota(jnp.int32, s.shape, 1)
                s = jnp.where(col <= row, s, -jnp.inf)
            m_new, l_new, acc_new = _softmax_step(
                s, v_ref[:, cols].astype(BF16), m_scr[h, :, :1], l_scr[h, :, :1], acc_scr[:, cols])
            m_scr[h] = jnp.broadcast_to(m_new, m_scr.shape[1:])
            l_scr[h] = jnp.broadcast_to(l_new, l_scr.shape[1:])
            acc_scr[:, cols] = acc_new

    @pl.when(ki < last)
    def _():
        update(kc_ref, vc_ref, ckc_ref, causal=False)

    @pl.when(ki == last)
    def _():
        update(kn_ref, vn_ref, ckn_ref, causal=True)
        for h in range(heads):
            cols = slice(h * HD_FOX, (h + 1) * HD_FOX)
            o_ref[:, cols] = acc_scr[:, cols] / l_scr[h, :, :1]


def _fox_sample(q, kc, vc, kn, vn, cq, ckc, ckn, batch, t_new, heads):
    past = kc.shape[1]
    fw = kc.shape[2]
    tk = _tile(past, 512)
    nkc = past // tk
    return pl.pallas_call(
        functools.partial(_fox_sample_kernel, heads=heads),
        grid=(batch, nkc + 1),
        in_specs=[pl.BlockSpec((t_new, fw), lambda b, ki: (b, 0)),
                  pl.BlockSpec((None, tk, fw), lambda b, ki: (b, jnp.minimum(ki, nkc - 1), 0)),
                  pl.BlockSpec((None, tk, fw), lambda b, ki: (b, jnp.minimum(ki, nkc - 1), 0)),
                  pl.BlockSpec((t_new, fw), lambda b, ki: (b, 0)),
                  pl.BlockSpec((t_new, fw), lambda b, ki: (b, 0)),
                  pl.BlockSpec((t_new, heads), lambda b, ki: (b, 0)),
                  pl.BlockSpec((None, heads, tk), lambda b, ki: (b, 0, jnp.minimum(ki, nkc - 1))),
                  pl.BlockSpec((None, heads, t_new), lambda b, ki: (b, 0, 0))],
        out_specs=pl.BlockSpec((t_new, fw), lambda b, ki: (b, 0)),
        out_shape=jax.ShapeDtypeStruct((batch * t_new, fw), F32),
        scratch_shapes=[pltpu.VMEM((heads, t_new, LANES), F32), pltpu.VMEM((heads, t_new, LANES), F32),
                        pltpu.VMEM((t_new, fw), F32)],
        compiler_params=_params("arbitrary", "arbitrary"),
        name="fox_sample",
    )(q, kc, vc, kn, vn, cq, ckc, ckn)


def _retention_kernel(q_ref, k_ref, v_ref, g_ref, cos_ref, sin_ref, gro_ref, lg_ref, s0_ref,
                      r_ref, sout_ref, s_scr, *, blk):
    t = pl.program_id(2)

    @pl.when(t == 0)
    def _():
        s_scr[...] = s0_ref[...]

    lg = lg_ref[:, :1]
    cos = cos_ref[...]
    sin = sin_ref[...]
    half = DK_RET // 2
    q = q_ref[...]
    k = k_ref[...]
    q = q * cos + pltpu.roll(q, half, axis=1) * sin
    k = (k * cos + pltpu.roll(k, half, axis=1) * sin) * (DK_RET ** -0.5)
    v = v_ref[...]
    qb = q.astype(BF16)

    ri = lax.broadcasted_iota(jnp.int32, (blk, blk), 0)
    ci = lax.broadcasted_iota(jnp.int32, (blk, blk), 1)
    dist = jnp.abs(ri - ci).astype(F32)
    decay = jnp.where(ci // CHUNK <= ri // CHUNK, jnp.exp(lg * dist), 0.0)
    a = _qk(qb, k.astype(BF16)) * decay
    idx = lax.broadcasted_iota(jnp.int32, (blk, 1), 0).astype(F32)
    cross = jnp.exp(lg * (idx + 1.0))
    s_prev = s_scr[...]
    o = jnp.dot(a.astype(BF16), v, preferred_element_type=F32)
    o = o + jnp.dot(qb, s_prev.astype(BF16), preferred_element_type=F32) * cross

    k_dec = (k * jnp.exp(lg * (blk - 1.0 - idx))).astype(BF16)
    s_new = jnp.exp(lg * blk) * s_prev + lax.dot_general(
        k_dec, v, (((0,), (0,)), ((), ())), preferred_element_type=F32)
    s_scr[...] = s_new

    ms = jnp.mean(o * o, axis=-1, keepdims=True)
    g = g_ref[...]
    silu = g / (1.0 + jnp.exp(-g))
    r_ref[...] = (o * lax.rsqrt(ms + EPS) * gro_ref[...] * silu).astype(r_ref.dtype)

    @pl.when(t == pl.num_programs(2) - 1)
    def _():
        sout_ref[...] = s_new


def _retention(q_r, k_r, v_r, g_r, cos, sin, g_ret_out, lg, s0, batch, seq, heads, blk):
    m = q_r.shape[0]
    nb = seq // blk
    row = lambda b, h, t: (b * nb + t, h)
    return pl.pallas_call(
        functools.partial(_retention_kernel, blk=blk),
        grid=(batch, heads, nb),
        in_specs=[pl.BlockSpec((blk, DK_RET), row),
                  pl.BlockSpec((blk, DK_RET), row),
                  pl.BlockSpec((blk, DV_RET), row),
                  pl.BlockSpec((blk, DV_RET), row),
                  pl.BlockSpec((blk, DK_RET), lambda b, h, t: (t, 0)),
                  pl.BlockSpec((blk, DK_RET), lambda b, h, t: (t, 0)),
                  pl.BlockSpec((1, DV_RET), lambda b, h, t: (0, h)),
                  pl.BlockSpec((None, 1, LANES), lambda b, h, t: (h, 0, 0)),
                  pl.BlockSpec((None, None, DK_RET, DV_RET), lambda b, h, t: (b, h, 0, 0))],
        out_specs=[pl.BlockSpec((blk, DV_RET), row),
                   pl.BlockSpec((None, None, DK_RET, DV_RET), lambda b, h, t: (b, h, 0, 0))],
        out_shape=[jax.ShapeDtypeStruct((m, heads * DV_RET), BF16),
                   jax.ShapeDtypeStruct((batch, heads, DK_RET, DV_RET), F32)],
        scratch_shapes=[pltpu.VMEM((DK_RET, DV_RET), F32)],
        compiler_params=_params("arbitrary", "arbitrary", "arbitrary"),
        name="retention",
    )(q_r, k_r, v_r, g_r, cos, sin, g_ret_out.reshape(1, -1), lg, s0)


def _rotary_tables(pos):
    half = DK_RET // 2
    inv = ROPE_BASE ** (-jnp.arange(half, dtype=F32) / half)
    ang = pos.astype(F32)[:, None] * inv[None, :]
    cos = jnp.cos(ang)
    sin = jnp.sin(ang)
    return jnp.concatenate([cos, cos], axis=-1), jnp.concatenate([-sin, sin], axis=-1)


def _layer_weights(w_in, b_forget, w_out, w_up, w_down, h_fox, h_ret):
    fw = h_fox * HD_FOX
    qkw = h_ret * DK_RET
    vw = h_ret * DV_RET
    o1, o2, o3 = fw, 2 * fw, 3 * fw
    o4 = o3 + h_fox
    o5 = o4 + qkw
    o6 = o5 + qkw
    o7 = o6 + vw
    seg = lambda a, b: w_in[:, a:b].astype(BF16)
    pad = LANES - h_fox
    return dict(
        q_f=seg(0, o1), k_f=seg(o1, o2), v_f=seg(o2, o3),
        logf=jnp.pad(w_in[:, o3:o4], ((0, 0), (0, pad))).astype(BF16),
        b_logf=jnp.pad(b_forget, (0, pad)).reshape(1, LANES).astype(F32),
        q_r=seg(o4, o5), k_r=seg(o5, o6), v_r=seg(o6, o7), g_r=seg(o7, o7 + vw),
        w_out=w_out.astype(BF16), w_up=w_up.astype(BF16), w_down=w_down.astype(BF16))


def _project(x2d, g_attn, w):
    h = _rmsnorm(x2d, g_attn, BF16)
    tm = 1024
    return dict(
        q_f=_matmul(h, w["q_f"], BF16, tm_pref=tm),
        k_f=_matmul(h, w["k_f"], F32, tm_pref=tm),
        v_f=_matmul(h, w["v_f"], F32, tm_pref=tm),
        logf=_matmul(h, w["logf"], F32, epilogue="logsig", bias=w["b_logf"], tm_pref=tm),
        q_r=_matmul(h, w["q_r"], F32, tm_pref=tm),
        k_r=_matmul(h, w["k_r"], F32, tm_pref=tm),
        v_r=_matmul(h, w["v_r"], BF16, tm_pref=tm),
        g_r=_matmul(h, w["g_r"], F32, tm_pref=tm))


def _finish_layer(x2d, o_fox, r, w, g_fox_out, g_mlp):
    x1 = _out_proj(o_fox, g_fox_out, r, w["w_out"], x2d)
    h2 = _rmsnorm(x1, g_mlp, BF16)
    u = _matmul(h2, w["w_up"], BF16, epilogue="relu2")
    return _matmul_residual_ksplit(u, w["w_down"], x1)


def _retention_log_gamma(h_ret):
    lg = jnp.log1p(-jnp.exp2(-5.0 - jnp.arange(h_ret, dtype=F32)))
    return jnp.broadcast_to(lg[:, None, None], (h_ret, 1, LANES))


def kernel(x_prompt, x_sample, cache_fox_k, cache_fox_v, cache_fox_logf, state_ret,
           g_attn, w_in, b_forget, g_fox_out, g_ret_out, w_out, g_mlp, w_up, w_down, g_final):
    bp, seq, d = x_prompt.shape
    bs, t_new, _ = x_sample.shape
    depth, _, past, h_fox, _ = cache_fox_k.shape
    h_ret = state_ret.shape[2]
    fw = h_fox * HD_FOX
    dtype = x_prompt.dtype

    lg = _retention_log_gamma(h_ret)
    cos_p, sin_p = _rotary_tables(jnp.arange(seq))
    cos_s, sin_s = _rotary_tables(past + jnp.arange(t_new))
    blk_p = _tile(seq, 256, CHUNK)

    yp = x_prompt.reshape(bp * seq, d)
    ys = x_sample.reshape(bs * t_new, d)
    outs = [[] for _ in range(8)]
    for l in range(depth):
        w = _layer_weights(w_in[l], b_forget[l], w_out[l], w_up[l], w_down[l], h_fox, h_ret)

        p = _project(yp, g_attn[l], w)
        logf_p = p["logf"][:, :h_fox]
        c_row = _cumsum_lanes(jnp.swapaxes(logf_p.reshape(bp, seq, h_fox), 1, 2))
        c_col = jnp.swapaxes(c_row, 1, 2).reshape(bp * seq, h_fox)
        o_fox = _fox_prompt(p["q_f"], p["k_f"], p["v_f"], c_col, c_row.reshape(bp * h_fox, 1, seq),
                            bp, seq, h_fox)
        s0 = jnp.zeros((bp, h_ret, DK_RET, DV_RET), F32)
        r, s_fin = _retention(p["q_r"], p["k_r"], p["v_r"], p["g_r"], cos_p, sin_p, g_ret_out[l], lg, s0,
                              bp, seq, h_ret, blk_p)
        yp = _finish_layer(yp, o_fox, r, w, g_fox_out[l], g_mlp[l])
        outs[0].append(p["k_f"].reshape(bp, seq, h_fox, HD_FOX))
        outs[1].append(p["v_f"].reshape(bp, seq, h_fox, HD_FOX))
        outs[2].append(logf_p.reshape(bp, seq, h_fox).astype(dtype))
        outs[3].append(s_fin.astype(dtype))

        p = _project(ys, g_attn[l], w)
        logf_s = p["logf"][:, :h_fox]
        total = past + t_new
        padded = -(-total // LANES) * LANES
        logf_all = jnp.concatenate(
            [jnp.swapaxes(cache_fox_logf[l].astype(F32), 1, 2),
             jnp.swapaxes(logf_s.reshape(bs, t_new, h_fox), 1, 2),
             jnp.zeros((bs, h_fox, padded - total), F32)], axis=2)
        c_all = _cumsum_lanes(logf_all)
        ckc = c_all[:, :, :past]
        ckn = c_all[:, :, past:total]
        cq = jnp.swapaxes(ckn, 1, 2).reshape(bs * t_new, h_fox)
        o_fox = _fox_sample(p["q_f"], cache_fox_k[l].reshape(bs, past, fw), cache_fox_v[l].reshape(bs, past, fw),
                            p["k_f"], p["v_f"], cq, ckc, ckn, bs, t_new, h_fox)
        r, s_new = _retention(p["q_r"], p["k_r"], p["v_r"], p["g_r"], cos_s, sin_s, g_ret_out[l], lg,
                              state_ret[l].astype(F32), bs, t_new, h_ret, t_new)
        ys = _finish_layer(ys, o_fox, r, w, g_fox_out[l], g_mlp[l])
        outs[4].append(p["k_f"].reshape(bs, t_new, h_fox, HD_FOX))
        outs[5].append(p["v_f"].reshape(bs, t_new, h_fox, HD_FOX))
        outs[6].append(logf_s.reshape(bs, t_new, h_fox).astype(dtype))
        outs[7].append(s_new.astype(dtype))

    y_prompt = _rmsnorm(yp, g_final, dtype).reshape(bp, seq, d)
    y_sample = _rmsnorm(ys, g_final, dtype).reshape(bs, t_new, d)
    return (y_prompt, y_sample) + tuple(jnp.stack(o) for o in outs)
```

```python
import functools

import jax
import jax.numpy as jnp
from jax import lax
from jax.experimental import pallas as pl
from jax.experimental.pallas import tpu as pltpu

CHUNK = 64
HD_FOX = 128
DK_RET = 128
DV_RET = 256
ROPE_BASE = 10000.0
EPS = 1e-6

LANES = 128
SUBLANES = 8
VMEM_LIMIT_BYTES = 60 * 1024 * 1024

F32 = jnp.float32
BF16 = jnp.bfloat16


def _params(*sem):
    return pltpu.CompilerParams(dimension_semantics=sem, vmem_limit_bytes=VMEM_LIMIT_BYTES)


def _tile(n, pref, mult=LANES):
    if n <= pref:
        return n
    t = (pref // mult) * mult
    while t > mult and n % t:
        t -= mult
    assert n % t == 0, (n, pref)
    return t


def _rms_kernel(x_ref, g_ref, o_ref):
    x = x_ref[...]
    ms = jnp.mean(x * x, axis=-1, keepdims=True)
    o_ref[...] = (x * lax.rsqrt(ms + EPS) * g_ref[...]).astype(o_ref.dtype)


def _rmsnorm(x, g, out_dtype):
    m, d = x.shape
    tr = _tile(m, 256, 8)
    return pl.pallas_call(
        _rms_kernel,
        grid=(m // tr,),
        in_specs=[pl.BlockSpec((tr, d), lambda i: (i, 0)),
                  pl.BlockSpec((1, d), lambda i: (0, 0))],
        out_specs=pl.BlockSpec((tr, d), lambda i: (i, 0)),
        out_shape=jax.ShapeDtypeStruct((m, d), out_dtype),
        compiler_params=_params("arbitrary"),
        name="rmsnorm",
    )(x, g.reshape(1, d))


def _log_sigmoid(z):
    return jnp.minimum(z, 0.0) - jnp.log(1.0 + jnp.exp(-jnp.abs(z)))


def _mm_kernel(a_ref, w_ref, *rest, epilogue):
    acc = jnp.dot(a_ref[...], w_ref[...], preferred_element_type=F32)
    if epilogue == "logsig":
        b_ref, o_ref = rest
        o_ref[...] = _log_sigmoid(acc + b_ref[...])
    elif epilogue == "relu2":
        (o_ref,) = rest
        u = jnp.maximum(acc, 0.0)
        o_ref[...] = (u * u).astype(o_ref.dtype)
    else:
        (o_ref,) = rest
        o_ref[...] = acc.astype(o_ref.dtype)


def _matmul(a, w, out_dtype, epilogue="cast", bias=None, tm_pref=1024, tn_pref=1024):
    m, k = a.shape
    n = w.shape[1]
    tm = _tile(m, tm_pref, 16)
    tn = _tile(n, tn_pref)
    in_specs = [pl.BlockSpec((tm, k), lambda i, j: (i, 0)),
                pl.BlockSpec((k, tn), lambda i, j: (0, j))]
    args = [a, w]
    if bias is not None:
        in_specs.append(pl.BlockSpec((1, tn), lambda i, j: (0, j)))
        args.append(bias)
    return pl.pallas_call(
        functools.partial(_mm_kernel, epilogue=epilogue),
        grid=(m // tm, n // tn),
        in_specs=in_specs,
        out_specs=pl.BlockSpec((tm, tn), lambda i, j: (i, j)),
        out_shape=jax.ShapeDtypeStruct((m, n), out_dtype),
        compiler_params=_params("arbitrary", "arbitrary"),
        name="matmul_" + epilogue,
    )(*args)


def _mix_mm_kernel(o_ref, gf_ref, r_ref, w_ref, res_ref, out_ref, a_scr, *, fw):
    @pl.when(pl.program_id(1) == 0)
    def _():
        o = o_ref[...]
        ms = jnp.mean(o * o, axis=-1, keepdims=True)
        a_scr[:, :fw] = (o * lax.rsqrt(ms + EPS) * gf_ref[...]).astype(BF16)
        a_scr[:, fw:] = r_ref[...]

    out_ref[...] = res_ref[...] + jnp.dot(a_scr[...], w_ref[...], preferred_element_type=F32)


def _out_proj(o_fox, g_fox, r, w, res):
    m, fw = o_fox.shape
    rw = r.shape[1]
    n = w.shape[1]
    tm = _tile(m, 512, 16)
    tn = _tile(n, 1024)
    return pl.pallas_call(
        functools.partial(_mix_mm_kernel, fw=fw),
        grid=(m // tm, n // tn),
        in_specs=[pl.BlockSpec((tm, fw), lambda i, j: (i, 0)),
                  pl.BlockSpec((1, fw), lambda i, j: (0, 0)),
                  pl.BlockSpec((tm, rw), lambda i, j: (i, 0)),
                  pl.BlockSpec((fw + rw, tn), lambda i, j: (0, j)),
                  pl.BlockSpec((tm, tn), lambda i, j: (i, j))],
        out_specs=pl.BlockSpec((tm, tn), lambda i, j: (i, j)),
        out_shape=jax.ShapeDtypeStruct((m, n), F32),
        scratch_shapes=[pltpu.VMEM((tm, fw + rw), BF16)],
        compiler_params=_params("arbitrary", "arbitrary"),
        name="out_proj",
    )(o_fox, g_fox.reshape(1, fw), r, w, res)


def _mm_acc_kernel(a_ref, w_ref, res_ref, out_ref):
    part = jnp.dot(a_ref[...], w_ref[...], preferred_element_type=F32)

    @pl.when(pl.program_id(2) == 0)
    def _():
        out_ref[...] = res_ref[...] + part

    @pl.when(pl.program_id(2) != 0)
    def _():
        out_ref[...] += part


def _matmul_residual_ksplit(a, w, res):
    m, k = a.shape
    n = w.shape[1]
    tm = _tile(m, 1024, 16)
    tn = _tile(n, 1024)
    tk = _tile(k, 4096)
    return pl.pallas_call(
        _mm_acc_kernel,
        grid=(m // tm, n // tn, k // tk),
        in_specs=[pl.BlockSpec((tm, tk), lambda i, j, kk: (i, kk)),
                  pl.BlockSpec((tk, tn), lambda i, j, kk: (kk, j)),
                  pl.BlockSpec((tm, tn), lambda i, j, kk: (i, j))],
        out_specs=pl.BlockSpec((tm, tn), lambda i, j, kk: (i, j)),
        out_shape=jax.ShapeDtypeStruct((m, n), F32),
        compiler_params=_params("arbitrary", "arbitrary", "arbitrary"),
        name="down_proj",
    )(a, w, res)


def _cumsum_kernel(x_ref, o_ref):
    x = x_ref[...]
    t = x.shape[1]
    lane = lax.broadcasted_iota(jnp.int32, x.shape, 1)
    d = 1
    while d < t:
        x = x + jnp.where(lane >= d, pltpu.roll(x, d, axis=1), 0.0)
        d *= 2
    o_ref[...] = x


def _cumsum_lanes(x):
    b, h, t = x.shape
    return pl.pallas_call(
        _cumsum_kernel,
        grid=(b,),
        in_specs=[pl.BlockSpec((None, h, t), lambda i: (i, 0, 0))],
        out_specs=pl.BlockSpec((None, h, t), lambda i: (i, 0, 0)),
        out_shape=jax.ShapeDtypeStruct((b, h, t), F32),
        compiler_params=_params("arbitrary"),
        name="cumsum_logf",
    )(x)


def _qk(q_bf16, k_bf16):
    return lax.dot_general(q_bf16, k_bf16, (((1,), (1,)), ((), ())), preferred_element_type=F32)


def _head_column(c_ref, h):
    c_all = c_ref[...]
    head = lax.broadcasted_iota(jnp.int32, c_all.shape, 1)
    return jnp.sum(jnp.where(head == h, c_all, 0.0), axis=1, keepdims=True)


def _fox_prompt_kernel(q_ref, k_ref, v_ref, cq_ref, ck_ref, o_ref, kt_scr, vb_scr, *, tq):
    h = pl.program_id(1)
    seq = q_ref.shape[0]
    kt_scr[...] = k_ref[...].T.astype(BF16)
    vb_scr[...] = v_ref[...].astype(BF16)
    cq = _head_column(cq_ref, h)
    ck = ck_ref[...]
    scale = HD_FOX ** -0.5
    row = lax.broadcasted_iota(jnp.int32, (tq, tq), 0)
    col = lax.broadcasted_iota(jnp.int32, (tq, tq), 1)
    causal = col <= row
    for qi in range(seq // tq):
        lo, hi = qi * tq, (qi + 1) * tq
        q = q_ref[lo:hi, :]
        cq_b = cq[lo:hi]
        s_d = jnp.dot(q, kt_scr[:, lo:hi], preferred_element_type=F32) * scale + cq_b - ck[:, lo:hi]
        s_d = jnp.where(causal, s_d, -jnp.inf)
        m = jnp.max(s_d, axis=1, keepdims=True)
        if qi > 0:
            s_o = jnp.dot(q, kt_scr[:, :lo], preferred_element_type=F32) * scale + cq_b - ck[:, :lo]
            m = jnp.maximum(m, jnp.max(s_o, axis=1, keepdims=True))
            p_o = jnp.exp(s_o - m)
            l = jnp.sum(p_o, axis=1, keepdims=True)
            acc = jnp.dot(p_o.astype(BF16), vb_scr[:lo, :], preferred_element_type=F32)
        p_d = jnp.exp(s_d - m)
        l_d = jnp.sum(p_d, axis=1, keepdims=True)
        acc_d = jnp.dot(p_d.astype(BF16), vb_scr[lo:hi, :], preferred_element_type=F32)
        if qi > 0:
            l_d = l_d + l
            acc_d = acc_d + acc
        o_ref[lo:hi, :] = acc_d / l_d


def _fox_prompt(q, k, v, c_col, c_row, batch, seq, heads):
    m, fw = k.shape
    tq = _tile(seq, 256)
    head_blk = lambda b, h: (b, h)
    return pl.pallas_call(
        functools.partial(_fox_prompt_kernel, tq=tq),
        grid=(batch, heads),
        in_specs=[pl.BlockSpec((seq, HD_FOX), head_blk),
                  pl.BlockSpec((seq, HD_FOX), head_blk),
                  pl.BlockSpec((seq, HD_FOX), head_blk),
                  pl.BlockSpec((seq, heads), lambda b, h: (b, 0)),
                  pl.BlockSpec((None, 1, seq), lambda b, h: (b * heads + h, 0, 0))],
        out_specs=pl.BlockSpec((seq, HD_FOX), head_blk),
        out_shape=jax.ShapeDtypeStruct((m, fw), F32),
        scratch_shapes=[pltpu.VMEM((HD_FOX, seq), BF16), pltpu.VMEM((seq, HD_FOX), BF16)],
        compiler_params=_params("arbitrary", "arbitrary"),
        name="fox_prompt",
    )(q, k, v, c_col, c_row)


def _every_eighth_row(ref, j):
    n, s, d = ref.shape
    return ref.reshape(n * s, d)[pl.ds(j, n, stride=s), :]


def _fox_sample_kernel(*refs, heads):
    ng = heads // SUBLANES
    q_ref = refs[0]
    k_refs = refs[1:1 + ng]
    v_refs = refs[1 + ng:1 + 2 * ng]
    kn_ref, vn_ref, cq_ref, ckc_ref, ckn_ref, o_ref, m_scr, l_scr, acc_scr = refs[1 + 2 * ng:]
    ki = pl.program_id(1)
    last = pl.num_programs(1) - 1
    scale = HD_FOX ** -0.5

    @pl.when(ki == 0)
    def _():
        m_scr[...] = jnp.full(m_scr.shape, -jnp.inf, F32)
        l_scr[...] = jnp.zeros(l_scr.shape, F32)
        acc_scr[...] = jnp.zeros(acc_scr.shape, F32)

    def cached(group_refs, h):
        return _every_eighth_row(group_refs[h // SUBLANES], h % SUBLANES).astype(BF16)

    def update(load_k, load_v, ck_ref, causal):
        new_state = []
        for h in range(heads):
            cols = slice(h * HD_FOX, (h + 1) * HD_FOX)
            s = _qk(q_ref[:, cols], load_k(h)) * scale + cq_ref[:, h:h + 1] - ck_ref[h:h + 1, :]
            if causal:
                row = lax.broadcasted_iota(jnp.int32, s.shape, 0)
                col = lax.broadcasted_iota(jnp.int32, s.shape, 1)
                s = jnp.where(col <= row, s, -jnp.inf)
            m_prev = m_scr[h, :, :1]
            m_new = jnp.maximum(m_prev, jnp.max(s, axis=1, keepdims=True))
            alpha = jnp.exp(m_prev - m_new)
            p = jnp.exp(s - m_new)
            l_new = alpha * l_scr[h, :, :1] + jnp.sum(p, axis=1, keepdims=True)
            acc_new = alpha * acc_scr[:, cols] + jnp.dot(p.astype(BF16), load_v(h), preferred_element_type=F32)
            new_state.append((m_new, l_new, acc_new))
        for h, (m_new, l_new, acc_new) in enumerate(new_state):
            m_scr[h] = jnp.broadcast_to(m_new, m_scr.shape[1:])
            l_scr[h] = jnp.broadcast_to(l_new, l_scr.shape[1:])
            acc_scr[:, h * HD_FOX:(h + 1) * HD_FOX] = acc_new

    @pl.when(ki < last)
    def _():
        update(functools.partial(cached, k_refs), functools.partial(cached, v_refs), ckc_ref, causal=False)

    @pl.when(ki == last)
    def _():
        new_rows = lambda ref, h: ref[:, h * HD_FOX:(h + 1) * HD_FOX].astype(BF16)
        update(functools.partial(new_rows, kn_ref), functools.partial(new_rows, vn_ref), ckn_ref, causal=True)
        for h in range(heads):
            cols = slice(h * HD_FOX, (h + 1) * HD_FOX)
            o_ref[:, cols] = acc_scr[:, cols] / l_scr[h, :, :1]


def _fox_sample(q, kc, vc, kn, vn, cq, ckc, ckn, batch, t_new, heads):
    past = kc.shape[1]
    fw = heads * HD_FOX
    assert heads % SUBLANES == 0, heads
    ng = heads // SUBLANES
    tk = _tile(past, 1024)
    nkc = past // tk
    kc = kc.reshape(batch, past, ng, SUBLANES, HD_FOX)
    vc = vc.reshape(batch, past, ng, SUBLANES, HD_FOX)
    row_blk = pl.BlockSpec((t_new, fw), lambda b, ki: (b, 0))
    group_blk = [pl.BlockSpec((None, tk, None, SUBLANES, HD_FOX),
                              functools.partial(lambda b, ki, g: (b, jnp.minimum(ki, nkc - 1), g, 0, 0), g=g))
                 for g in range(ng)]
    return pl.pallas_call(
        functools.partial(_fox_sample_kernel, heads=heads),
        grid=(batch, nkc + 1),
        in_specs=[row_blk] + group_blk + group_blk + [
            row_blk, row_blk,
            pl.BlockSpec((t_new, heads), lambda b, ki: (b, 0)),
            pl.BlockSpec((None, heads, tk), lambda b, ki: (b, 0, jnp.minimum(ki, nkc - 1))),
            pl.BlockSpec((None, heads, t_new), lambda b, ki: (b, 0, 0))],
        out_specs=row_blk,
        out_shape=jax.ShapeDtypeStruct((batch * t_new, fw), F32),
        scratch_shapes=[pltpu.VMEM((heads, t_new, LANES), F32), pltpu.VMEM((heads, t_new, LANES), F32),
                        pltpu.VMEM((t_new, fw), F32)],
        compiler_params=_params("arbitrary", "arbitrary"),
        name="fox_sample",
    )(q, *([kc] * ng), *([vc] * ng), kn, vn, cq, ckc, ckn)


def _retention_kernel(q_ref, k_ref, v_ref, g_ref, cos_ref, sin_ref, gro_ref, lg_ref, s0_ref,
                      r_ref, sout_ref, s_scr, *, blk):
    t = pl.program_id(2)

    @pl.when(t == 0)
    def _():
        s_scr[...] = s0_ref[...]

    lg = lg_ref[:, :1]
    cos = cos_ref[...]
    sin = sin_ref[...]
    half = DK_RET // 2
    q = q_ref[...]
    k = k_ref[...]
    q = q * cos + pltpu.roll(q, half, axis=1) * sin
    k = (k * cos + pltpu.roll(k, half, axis=1) * sin) * (DK_RET ** -0.5)
    v = v_ref[...]
    qb = q.astype(BF16)

    ri = lax.broadcasted_iota(jnp.int32, (blk, blk), 0)
    ci = lax.broadcasted_iota(jnp.int32, (blk, blk), 1)
    dist = jnp.abs(ri - ci).astype(F32)
    decay = jnp.where(ci // CHUNK <= ri // CHUNK, jnp.exp(lg * dist), 0.0)
    a = _qk(qb, k.astype(BF16)) * decay
    idx = lax.broadcasted_iota(jnp.int32, (blk, 1), 0).astype(F32)
    cross = jnp.exp(lg * (idx + 1.0))
    s_prev = s_scr[...]
    o = jnp.dot(a.astype(BF16), v, preferred_element_type=F32)
    o = o + jnp.dot(qb, s_prev.astype(BF16), preferred_element_type=F32) * cross

    k_dec = (k * jnp.exp(lg * (blk - 1.0 - idx))).astype(BF16)
    s_new = jnp.exp(lg * blk) * s_prev + lax.dot_general(
        k_dec, v, (((0,), (0,)), ((), ())), preferred_element_type=F32)
    s_scr[...] = s_new

    ms = jnp.mean(o * o, axis=-1, keepdims=True)
    g = g_ref[...]
    silu = g / (1.0 + jnp.exp(-g))
    r_ref[...] = (o * lax.rsqrt(ms + EPS) * gro_ref[...] * silu).astype(r_ref.dtype)

    @pl.when(t == pl.num_programs(2) - 1)
    def _():
        sout_ref[...] = s_new


def _retention(q_r, k_r, v_r, g_r, cos, sin, g_ret_out, lg, s0, batch, seq, heads, blk):
    m = q_r.shape[0]
    nb = seq // blk
    row = lambda b, h, t: (b * nb + t, h)
    return pl.pallas_call(
        functools.partial(_retention_kernel, blk=blk),
        grid=(batch, heads, nb),
        in_specs=[pl.BlockSpec((blk, DK_RET), row),
                  pl.BlockSpec((blk, DK_RET), row),
                  pl.BlockSpec((blk, DV_RET), row),
                  pl.BlockSpec((blk, DV_RET), row),
                  pl.BlockSpec((blk, DK_RET), lambda b, h, t: (t, 0)),
                  pl.BlockSpec((blk, DK_RET), lambda b, h, t: (t, 0)),
                  pl.BlockSpec((1, DV_RET), lambda b, h, t: (0, h)),
                  pl.BlockSpec((None, 1, LANES), lambda b, h, t: (h, 0, 0)),
                  pl.BlockSpec((None, None, DK_RET, DV_RET), lambda b, h, t: (b, h, 0, 0))],
        out_specs=[pl.BlockSpec((blk, DV_RET), row),
                   pl.BlockSpec((None, None, DK_RET, DV_RET), lambda b, h, t: (b, h, 0, 0))],
        out_shape=[jax.ShapeDtypeStruct((m, heads * DV_RET), BF16),
                   jax.ShapeDtypeStruct((batch, heads, DK_RET, DV_RET), F32)],
        scratch_shapes=[pltpu.VMEM((DK_RET, DV_RET), F32)],
        compiler_params=_params("arbitrary", "arbitrary", "arbitrary"),
        name="retention",
    )(q_r, k_r, v_r, g_r, cos, sin, g_ret_out.reshape(1, -1), lg, s0)


def _rotary_tables(pos):
    half = DK_RET // 2
    inv = ROPE_BASE ** (-jnp.arange(half, dtype=F32) / half)
    ang = pos.astype(F32)[:, None] * inv[None, :]
    cos = jnp.cos(ang)
    sin = jnp.sin(ang)
    return jnp.concatenate([cos, cos], axis=-1), jnp.concatenate([-sin, sin], axis=-1)


def _layer_weights(w_in, b_forget, w_out, w_up, w_down, h_fox, h_ret):
    fw = h_fox * HD_FOX
    qkw = h_ret * DK_RET
    vw = h_ret * DV_RET
    o1, o2, o3 = fw, 2 * fw, 3 * fw
    o4 = o3 + h_fox
    o5 = o4 + qkw
    o6 = o5 + qkw
    o7 = o6 + vw
    seg = lambda a, b: w_in[:, a:b].astype(BF16)
    pad = LANES - h_fox
    return dict(
        q_f=seg(0, o1), k_f=seg(o1, o2), v_f=seg(o2, o3),
        logf=jnp.pad(w_in[:, o3:o4], ((0, 0), (0, pad))).astype(BF16),
        b_logf=jnp.pad(b_forget, (0, pad)).reshape(1, LANES).astype(F32),
        q_r=seg(o4, o5), k_r=seg(o5, o6), v_r=seg(o6, o7), g_r=seg(o7, o7 + vw),
        w_out=w_out.astype(BF16), w_up=w_up.astype(BF16), w_down=w_down.astype(BF16))


def _project(x2d, g_attn, w):
    h = _rmsnorm(x2d, g_attn, BF16)
    tm = 1024
    return dict(
        q_f=_matmul(h, w["q_f"], BF16, tm_pref=tm),
        k_f=_matmul(h, w["k_f"], F32, tm_pref=tm),
        v_f=_matmul(h, w["v_f"], F32, tm_pref=tm),
        logf=_matmul(h, w["logf"], F32, epilogue="logsig", bias=w["b_logf"], tm_pref=tm),
        q_r=_matmul(h, w["q_r"], F32, tm_pref=tm),
        k_r=_matmul(h, w["k_r"], F32, tm_pref=tm),
        v_r=_matmul(h, w["v_r"], BF16, tm_pref=tm),
        g_r=_matmul(h, w["g_r"], F32, tm_pref=tm))


def _finish_layer(x2d, o_fox, r, w, g_fox_out, g_mlp):
    x1 = _out_proj(o_fox, g_fox_out, r, w["w_out"], x2d)
    h2 = _rmsnorm(x1, g_mlp, BF16)
    u = _matmul(h2, w["w_up"], BF16, epilogue="relu2")
    return _matmul_residual_ksplit(u, w["w_down"], x1)


def _retention_log_gamma(h_ret):
    lg = jnp.log1p(-jnp.exp2(-5.0 - jnp.arange(h_ret, dtype=F32)))
    return jnp.broadcast_to(lg[:, None, None], (h_ret, 1, LANES))


def kernel(x_prompt, x_sample, cache_fox_k, cache_fox_v, cache_fox_logf, state_ret,
           g_attn, w_in, b_forget, g_fox_out, g_ret_out, w_out, g_mlp, w_up, w_down, g_final):
    bp, seq, d = x_prompt.shape
    bs, t_new, _ = x_sample.shape
    depth, _, past, h_fox, _ = cache_fox_k.shape
    h_ret = state_ret.shape[2]
    dtype = x_prompt.dtype

    lg = _retention_log_gamma(h_ret)
    cos_p, sin_p = _rotary_tables(jnp.arange(seq))
    cos_s, sin_s = _rotary_tables(past + jnp.arange(t_new))
    blk_p = _tile(seq, 256, CHUNK)

    yp = x_prompt.reshape(bp * seq, d)
    ys = x_sample.reshape(bs * t_new, d)
    outs = [[] for _ in range(8)]
    for l in range(depth):
        w = _layer_weights(w_in[l], b_forget[l], w_out[l], w_up[l], w_down[l], h_fox, h_ret)

        p = _project(yp, g_attn[l], w)
        logf_p = p["logf"][:, :h_fox]
        c_row = _cumsum_lanes(jnp.swapaxes(logf_p.reshape(bp, seq, h_fox), 1, 2))
        c_col = jnp.swapaxes(c_row, 1, 2).reshape(bp * seq, h_fox)
        o_fox = _fox_prompt(p["q_f"], p["k_f"], p["v_f"], c_col, c_row.reshape(bp * h_fox, 1, seq),
                            bp, seq, h_fox)
        s0 = jnp.zeros((bp, h_ret, DK_RET, DV_RET), F32)
        r, s_fin = _retention(p["q_r"], p["k_r"], p["v_r"], p["g_r"], cos_p, sin_p, g_ret_out[l], lg, s0,
                              bp, seq, h_ret, blk_p)
        yp = _finish_layer(yp, o_fox, r, w, g_fox_out[l], g_mlp[l])
        outs[0].append(p["k_f"].reshape(bp, seq, h_fox, HD_FOX))
        outs[1].append(p["v_f"].reshape(bp, seq, h_fox, HD_FOX))
        outs[2].append(logf_p.reshape(bp, seq, h_fox).astype(dtype))
        outs[3].append(s_fin.astype(dtype))

        p = _project(ys, g_attn[l], w)
        logf_s = p["logf"][:, :h_fox]
        total = past + t_new
        padded = -(-total // LANES) * LANES
        logf_all = jnp.concatenate(
            [jnp.swapaxes(cache_fox_logf[l].astype(F32), 1, 2),
             jnp.swapaxes(logf_s.reshape(bs, t_new, h_fox), 1, 2),
             jnp.zeros((bs, h_fox, padded - total), F32)], axis=2)
        c_all = _cumsum_lanes(logf_all)
        ckc = c_all[:, :, :past]
        ckn = c_all[:, :, past:total]
        cq = jnp.swapaxes(ckn, 1, 2).reshape(bs * t_new, h_fox)
        o_fox = _fox_sample(p["q_f"], cache_fox_k[l], cache_fox_v[l], p["k_f"], p["v_f"],
                            cq, ckc, ckn, bs, t_new, h_fox)
        r, s_new = _retention(p["q_r"], p["k_r"], p["v_r"], p["g_r"], cos_s, sin_s, g_ret_out[l], lg,
                              state_ret[l].astype(F32), bs, t_new, h_ret, t_new)
        ys = _finish_layer(ys, o_fox, r, w, g_fox_out[l], g_mlp[l])
        outs[4].append(p["k_f"].reshape(bs, t_new, h_fox, HD_FOX))
        outs[5].append(p["v_f"].reshape(bs, t_new, h_fox, HD_FOX))
        outs[6].append(logf_s.reshape(bs, t_new, h_fox).astype(dtype))
        outs[7].append(s_new.astype(dtype))

    y_prompt = _rmsnorm(yp, g_final, dtype).reshape(bp, seq, d)
    y_sample = _rmsnorm(ys, g_final, dtype).reshape(bs, t_new, d)
    return (y_prompt, y_sample) + tuple(jnp.stack(o) for o in outs)
```

```python
import functools

import jax
import jax.numpy as jnp
from jax import lax
from jax.experimental import pallas as pl
from jax.experimental.pallas import tpu as pltpu

CHUNK = 64
HD_FOX = 128
DK_RET = 128
DV_RET = 256
ROPE_BASE = 10000.0
EPS = 1e-6

LANES = 128
SUBLANES = 8
VMEM_LIMIT_BYTES = 60 * 1024 * 1024

F32 = jnp.float32
BF16 = jnp.bfloat16


def _params(*sem):
    return pltpu.CompilerParams(dimension_semantics=sem, vmem_limit_bytes=VMEM_LIMIT_BYTES)


def _tile(n, pref, mult=LANES):
    if n <= pref:
        return n
    t = (pref // mult) * mult
    while t > mult and n % t:
        t -= mult
    assert n % t == 0, (n, pref)
    return t


def _rms_kernel(x_ref, g_ref, o_ref):
    x = x_ref[...]
    ms = jnp.mean(x * x, axis=-1, keepdims=True)
    o_ref[...] = (x * lax.rsqrt(ms + EPS) * g_ref[...]).astype(o_ref.dtype)


def _rmsnorm(x, g, out_dtype):
    m, d = x.shape
    tr = _tile(m, 256, 8)
    return pl.pallas_call(
        _rms_kernel,
        grid=(m // tr,),
        in_specs=[pl.BlockSpec((tr, d), lambda i: (i, 0)),
                  pl.BlockSpec((1, d), lambda i: (0, 0))],
        out_specs=pl.BlockSpec((tr, d), lambda i: (i, 0)),
        out_shape=jax.ShapeDtypeStruct((m, d), out_dtype),
        compiler_params=_params("arbitrary"),
        name="rmsnorm",
    )(x, g.reshape(1, d))


def _log_sigmoid(z):
    return jnp.minimum(z, 0.0) - jnp.log(1.0 + jnp.exp(-jnp.abs(z)))


def _side_cast_specs(w, grid):
    steps = 1
    for g in grid:
        steps *= g
    rows, cols = w.shape
    if rows % steps or (rows // steps) % 16:
        return None

    def index_map(*idx):
        lin = idx[0]
        for g, i in zip(grid[1:], idx[1:]):
            lin = lin * g + i
        return (lin, 0)

    spec = pl.BlockSpec((rows // steps, cols), index_map)
    return spec, jax.ShapeDtypeStruct((rows, cols), BF16)


def _mm_kernel(a_ref, w_ref, *rest, epilogue, side):
    if side:
        *rest, side_out = rest
        side_in = rest.pop(-2)
        side_out[...] = side_in[...].astype(BF16)
    acc = jnp.dot(a_ref[...], w_ref[...], preferred_element_type=F32)
    if epilogue == "logsig":
        b_ref, o_ref = rest
        o_ref[...] = _log_sigmoid(acc + b_ref[...])
    elif epilogue == "relu2":
        (o_ref,) = rest
        u = jnp.maximum(acc, 0.0)
        o_ref[...] = (u * u).astype(o_ref.dtype)
    else:
        (o_ref,) = rest
        o_ref[...] = acc.astype(o_ref.dtype)


def _matmul(a, w, out_dtype, epilogue="cast", bias=None, tm_pref=1024, tn_pref=1024, cast_w=None):
    m, k = a.shape
    n = w.shape[1]
    tm = _tile(m, tm_pref, 16)
    tn = _tile(n, tn_pref)
    grid = (m // tm, n // tn)
    in_specs = [pl.BlockSpec((tm, k), lambda i, j: (i, 0)),
                pl.BlockSpec((k, tn), lambda i, j: (0, j))]
    args = [a, w]
    if bias is not None:
        in_specs.append(pl.BlockSpec((1, tn), lambda i, j: (0, j)))
        args.append(bias)
    out_specs = [pl.BlockSpec((tm, tn), lambda i, j: (i, j))]
    out_shape = [jax.ShapeDtypeStruct((m, n), out_dtype)]
    side = _side_cast_specs(cast_w, grid) if cast_w is not None else None
    if side:
        in_specs.append(side[0])
        args.append(cast_w)
        out_specs.append(side[0])
        out_shape.append(side[1])
    outs = pl.pallas_call(
        functools.partial(_mm_kernel, epilogue=epilogue, side=bool(side)),
        grid=grid,
        in_specs=in_specs,
        out_specs=out_specs,
        out_shape=out_shape,
        compiler_params=_params("arbitrary", "arbitrary"),
        name="matmul_" + epilogue,
    )(*args)
    if cast_w is None:
        return outs[0]
    return outs[0], (outs[1] if side else cast_w.astype(BF16))


def _mix_mm_kernel(o_ref, gf_ref, r_ref, w_ref, res_ref, out_ref, a_scr, *, fw):
    @pl.when(pl.program_id(1) == 0)
    def _():
        o = o_ref[...]
        ms = jnp.mean(o * o, axis=-1, keepdims=True)
        a_scr[:, :fw] = (o * lax.rsqrt(ms + EPS) * gf_ref[...]).astype(BF16)
        a_scr[:, fw:] = r_ref[...]

    out_ref[...] = res_ref[...] + jnp.dot(a_scr[...], w_ref[...], preferred_element_type=F32)


def _out_proj(o_fox, g_fox, r, w, res):
    m, fw = o_fox.shape
    rw = r.shape[1]
    n = w.shape[1]
    tm = _tile(m, 512, 16)
    tn = _tile(n, 1024)
    return pl.pallas_call(
        functools.partial(_mix_mm_kernel, fw=fw),
        grid=(m // tm, n // tn),
        in_specs=[pl.BlockSpec((tm, fw), lambda i, j: (i, 0)),
                  pl.BlockSpec((1, fw), lambda i, j: (0, 0)),
                  pl.BlockSpec((tm, rw), lambda i, j: (i, 0)),
                  pl.BlockSpec((fw + rw, tn), lambda i, j: (0, j)),
                  pl.BlockSpec((tm, tn), lambda i, j: (i, j))],
        out_specs=pl.BlockSpec((tm, tn), lambda i, j: (i, j)),
        out_shape=jax.ShapeDtypeStruct((m, n), F32),
        scratch_shapes=[pltpu.VMEM((tm, fw + rw), BF16)],
        compiler_params=_params("arbitrary", "arbitrary"),
        name="out_proj",
    )(o_fox, g_fox.reshape(1, fw), r, w, res)


def _mm_acc_kernel(a_ref, w_ref, res_ref, out_ref):
    part = jnp.dot(a_ref[...], w_ref[...], preferred_element_type=F32)

    @pl.when(pl.program_id(2) == 0)
    def _():
        out_ref[...] = res_ref[...] + part

    @pl.when(pl.program_id(2) != 0)
    def _():
        out_ref[...] += part


def _matmul_residual_ksplit(a, w, res):
    m, k = a.shape
    n = w.shape[1]
    tm = _tile(m, 1024, 16)
    tn = _tile(n, 1024)
    tk = _tile(k, 4096)
    return pl.pallas_call(
        _mm_acc_kernel,
        grid=(m // tm, n // tn, k // tk),
        in_specs=[pl.BlockSpec((tm, tk), lambda i, j, kk: (i, kk)),
                  pl.BlockSpec((tk, tn), lambda i, j, kk: (kk, j)),
                  pl.BlockSpec((tm, tn), lambda i, j, kk: (i, j))],
        out_specs=pl.BlockSpec((tm, tn), lambda i, j, kk: (i, j)),
        out_shape=jax.ShapeDtypeStruct((m, n), F32),
        compiler_params=_params("arbitrary", "arbitrary", "arbitrary"),
        name="down_proj",
    )(a, w, res)


def _cumsum_kernel(x_ref, o_ref):
    x = x_ref[...]
    t = x.shape[1]
    lane = lax.broadcasted_iota(jnp.int32, x.shape, 1)
    d = 1
    while d < t:
        x = x + jnp.where(lane >= d, pltpu.roll(x, d, axis=1), 0.0)
        d *= 2
    o_ref[...] = x


def _cumsum_lanes(x):
    b, h, t = x.shape
    return pl.pallas_call(
        _cumsum_kernel,
        grid=(b,),
        in_specs=[pl.BlockSpec((None, h, t), lambda i: (i, 0, 0))],
        out_specs=pl.BlockSpec((None, h, t), lambda i: (i, 0, 0)),
        out_shape=jax.ShapeDtypeStruct((b, h, t), F32),
        compiler_params=_params("arbitrary"),
        name="cumsum_logf",
    )(x)


def _qk(q_bf16, k_bf16):
    return lax.dot_general(q_bf16, k_bf16, (((1,), (1,)), ((), ())), preferred_element_type=F32)


def _head_column(c_ref, h):
    c_all = c_ref[...]
    head = lax.broadcasted_iota(jnp.int32, c_all.shape, 1)
    return jnp.sum(jnp.where(head == h, c_all, 0.0), axis=1, keepdims=True)


def _fox_prompt_kernel(q_ref, k_ref, v_ref, cq_ref, ck_ref, *rest, tq, side):
    if side:
        side_in, o_ref, side_out, kt_scr, vb_scr = rest
        side_out[...] = side_in[...].astype(BF16)
    else:
        o_ref, kt_scr, vb_scr = rest
    h = pl.program_id(1)
    seq = q_ref.shape[0]
    kt_scr[...] = k_ref[...].T.astype(BF16)
    vb_scr[...] = v_ref[...].astype(BF16)
    cq = _head_column(cq_ref, h)
    ck = ck_ref[...]
    scale = HD_FOX ** -0.5
    row = lax.broadcasted_iota(jnp.int32, (tq, tq), 0)
    col = lax.broadcasted_iota(jnp.int32, (tq, tq), 1)
    causal = col <= row
    for qi in range(seq // tq):
        lo, hi = qi * tq, (qi + 1) * tq
        q = q_ref[lo:hi, :]
        cq_b = cq[lo:hi]
        s_d = jnp.dot(q, kt_scr[:, lo:hi], preferred_element_type=F32) * scale + cq_b - ck[:, lo:hi]
        s_d = jnp.where(causal, s_d, -jnp.inf)
        m = jnp.max(s_d, axis=1, keepdims=True)
        if qi > 0:
            s_o = jnp.dot(q, kt_scr[:, :lo], preferred_element_type=F32) * scale + cq_b - ck[:, :lo]
            m = jnp.maximum(m, jnp.max(s_o, axis=1, keepdims=True))
            p_o = jnp.exp(s_o - m)
            l = jnp.sum(p_o, axis=1, keepdims=True)
            acc = jnp.dot(p_o.astype(BF16), vb_scr[:lo, :], preferred_element_type=F32)
        p_d = jnp.exp(s_d - m)
        l_d = jnp.sum(p_d, axis=1, keepdims=True)
        acc_d = jnp.dot(p_d.astype(BF16), vb_scr[lo:hi, :], preferred_element_type=F32)
        if qi > 0:
            l_d = l_d + l
            acc_d = acc_d + acc
        o_ref[lo:hi, :] = acc_d / l_d


def _fox_prompt(q, k, v, c_col, c_row, batch, seq, heads, cast_w):
    m, fw = k.shape
    tq = _tile(seq, 256)
    grid = (batch, heads)
    head_blk = lambda b, h: (b, h)
    in_specs = [pl.BlockSpec((seq, HD_FOX), head_blk),
                pl.BlockSpec((seq, HD_FOX), head_blk),
                pl.BlockSpec((seq, HD_FOX), head_blk),
                pl.BlockSpec((seq, heads), lambda b, h: (b, 0)),
                pl.BlockSpec((None, 1, seq), lambda b, h: (b * heads + h, 0, 0))]
    args = [q, k, v, c_col, c_row]
    out_specs = [pl.BlockSpec((seq, HD_FOX), head_blk)]
    out_shape = [jax.ShapeDtypeStruct((m, fw), F32)]
    side = _side_cast_specs(cast_w, grid)
    if side:
        in_specs.append(side[0])
        args.append(cast_w)
        out_specs.append(side[0])
        out_shape.append(side[1])
    outs = pl.pallas_call(
        functools.partial(_fox_prompt_kernel, tq=tq, side=bool(side)),
        grid=grid,
        in_specs=in_specs,
        out_specs=out_specs,
        out_shape=out_shape,
        scratch_shapes=[pltpu.VMEM((HD_FOX, seq), BF16), pltpu.VMEM((seq, HD_FOX), BF16)],
        compiler_params=_params("arbitrary", "arbitrary"),
        name="fox_prompt",
    )(*args)
    return outs[0], (outs[1] if side else cast_w.astype(BF16))


def _every_eighth_row(ref, j):
    n, s, d = ref.shape
    return ref.reshape(n * s, d)[pl.ds(j, n, stride=s), :]


def _fox_sample_kernel(*refs, heads):
    ng = heads // SUBLANES
    q_ref = refs[0]
    k_refs = refs[1:1 + ng]
    v_refs = refs[1 + ng:1 + 2 * ng]
    kn_ref, vn_ref, cq_ref, ckc_ref, ckn_ref, o_ref, m_scr, l_scr, acc_scr = refs[1 + 2 * ng:]
    ki = pl.program_id(1)
    last = pl.num_programs(1) - 1
    scale = HD_FOX ** -0.5

    @pl.when(ki == 0)
    def _():
        m_scr[...] = jnp.full(m_scr.shape, -jnp.inf, F32)
        l_scr[...] = jnp.zeros(l_scr.shape, F32)
        acc_scr[...] = jnp.zeros(acc_scr.shape, F32)

    def cached(group_refs, h):
        return _every_eighth_row(group_refs[h // SUBLANES], h % SUBLANES).astype(BF16)

    def update(load_k, load_v, ck_ref, causal):
        new_state = []
        for h in range(heads):
            cols = slice(h * HD_FOX, (h + 1) * HD_FOX)
            s = _qk(q_ref[:, cols], load_k(h)) * scale + cq_ref[:, h:h + 1] - ck_ref[h:h + 1, :]
            if causal:
                row = lax.broadcasted_iota(jnp.int32, s.shape, 0)
                col = lax.broadcasted_iota(jnp.int32, s.shape, 1)
                s = jnp.where(col <= row, s, -jnp.inf)
            m_prev = m_scr[h, :, :1]
            m_new = jnp.maximum(m_prev, jnp.max(s, axis=1, keepdims=True))
            alpha = jnp.exp(m_prev - m_new)
            p = jnp.exp(s - m_new)
            l_new = alpha * l_scr[h, :, :1] + jnp.sum(p, axis=1, keepdims=True)
            acc_new = alpha * acc_scr[:, cols] + jnp.dot(p.astype(BF16), load_v(h), preferred_element_type=F32)
            new_state.append((m_new, l_new, acc_new))
        for h, (m_new, l_new, acc_new) in enumerate(new_state):
            m_scr[h] = jnp.broadcast_to(m_new, m_scr.shape[1:])
            l_scr[h] = jnp.broadcast_to(l_new, l_scr.shape[1:])
            acc_scr[:, h * HD_FOX:(h + 1) * HD_FOX] = acc_new

    @pl.when(ki < last)
    def _():
        update(functools.partial(cached, k_refs), functools.partial(cached, v_refs), ckc_ref, causal=False)

    @pl.when(ki == last)
    def _():
        new_rows = lambda ref, h: ref[:, h * HD_FOX:(h + 1) * HD_FOX].astype(BF16)
        update(functools.partial(new_rows, kn_ref), functools.partial(new_rows, vn_ref), ckn_ref, causal=True)
        for h in range(heads):
            cols = slice(h * HD_FOX, (h + 1) * HD_FOX)
            o_ref[:, cols] = acc_scr[:, cols] / l_scr[h, :, :1]


def _fox_sample(q, kc, vc, kn, vn, cq, ckc, ckn, batch, t_new, heads):
    past = kc.shape[1]
    fw = heads * HD_FOX
    assert heads % SUBLANES == 0, heads
    ng = heads // SUBLANES
    tk = _tile(past, 1024)
    nkc = past // tk
    kc = kc.reshape(batch, past, ng, SUBLANES, HD_FOX)
    vc = vc.reshape(batch, past, ng, SUBLANES, HD_FOX)
    row_blk = pl.BlockSpec((t_new, fw), lambda b, ki: (b, 0))
    group_blk = [pl.BlockSpec((None, tk, None, SUBLANES, HD_FOX),
                              functools.partial(lambda b, ki, g: (b, jnp.minimum(ki, nkc - 1), g, 0, 0), g=g))
                 for g in range(ng)]
    return pl.pallas_call(
        functools.partial(_fox_sample_kernel, heads=heads),
        grid=(batch, nkc + 1),
        in_specs=[row_blk] + group_blk + group_blk + [
            row_blk, row_blk,
            pl.BlockSpec((t_new, heads), lambda b, ki: (b, 0)),
            pl.BlockSpec((None, heads, tk), lambda b, ki: (b, 0, jnp.minimum(ki, nkc - 1))),
            pl.BlockSpec((None, heads, t_new), lambda b, ki: (b, 0, 0))],
        out_specs=row_blk,
        out_shape=jax.ShapeDtypeStruct((batch * t_new, fw), F32),
        scratch_shapes=[pltpu.VMEM((heads, t_new, LANES), F32), pltpu.VMEM((heads, t_new, LANES), F32),
                        pltpu.VMEM((t_new, fw), F32)],
        compiler_params=_params("arbitrary", "arbitrary"),
        name="fox_sample",
    )(q, *([kc] * ng), *([vc] * ng), kn, vn, cq, ckc, ckn)


def _retention_kernel(q_ref, k_ref, v_ref, g_ref, cos_ref, sin_ref, gro_ref, lg_ref, s0_ref, *rest,
                      blk, heads, side):
    if side:
        side_in, r_ref, sout_ref, side_out = rest
        side_out[...] = side_in[...].astype(BF16)
    else:
        r_ref, sout_ref = rest
    seq = q_ref.shape[0]
    half = DK_RET // 2
    ri = lax.broadcasted_iota(jnp.int32, (blk, blk), 0)
    ci = lax.broadcasted_iota(jnp.int32, (blk, blk), 1)
    dist = jnp.abs(ri - ci).astype(F32)
    visible = ci // CHUNK <= ri // CHUNK
    idx = lax.broadcasted_iota(jnp.int32, (blk, 1), 0).astype(F32)
    for hh in range(heads):
        lg = lg_ref[hh, :, :1]
        decay = jnp.where(visible, jnp.exp(lg * dist), 0.0)
        cross = jnp.exp(lg * (idx + 1.0))
        k_decay = jnp.exp(lg * (blk - 1.0 - idx))
        s_decay = jnp.exp(lg * blk)
        qk_cols = slice(hh * DK_RET, (hh + 1) * DK_RET)
        v_cols = slice(hh * DV_RET, (hh + 1) * DV_RET)
        gro = gro_ref[:, v_cols]
        s = s0_ref[hh]
        for t in range(seq // blk):
            rows = slice(t * blk, (t + 1) * blk)
            cos = cos_ref[rows, :]
            sin = sin_ref[rows, :]
            q = q_ref[rows, qk_cols]
            k = k_ref[rows, qk_cols]
            q = q * cos + pltpu.roll(q, half, axis=1) * sin
            k = (k * cos + pltpu.roll(k, half, axis=1) * sin) * (DK_RET ** -0.5)
            v = v_ref[rows, v_cols]
            qb = q.astype(BF16)
            a = _qk(qb, k.astype(BF16)) * decay
            o = jnp.dot(a.astype(BF16), v, preferred_element_type=F32)
            o = o + jnp.dot(qb, s.astype(BF16), preferred_element_type=F32) * cross
            s = s_decay * s + lax.dot_general((k * k_decay).astype(BF16), v, (((0,), (0,)), ((), ())),
                                              preferred_element_type=F32)
            ms = jnp.mean(o * o, axis=-1, keepdims=True)
            g = g_ref[rows, v_cols]
            silu = g / (1.0 + jnp.exp(-g))
            r_ref[rows, v_cols] = (o * lax.rsqrt(ms + EPS) * gro * silu).astype(r_ref.dtype)
        sout_ref[hh] = s


def _retention(q_r, k_r, v_r, g_r, cos, sin, g_ret_out, lg, s0, batch, seq, heads, blk, heads_per_step,
               cast_w=None):
    m = q_r.shape[0]
    hps = heads_per_step
    grid = (batch, heads // hps)
    row = lambda b, h: (b, h)
    state = pl.BlockSpec((None, hps, DK_RET, DV_RET), lambda b, h: (b, h, 0, 0))
    in_specs = [pl.BlockSpec((seq, hps * DK_RET), row),
                pl.BlockSpec((seq, hps * DK_RET), row),
                pl.BlockSpec((seq, hps * DV_RET), row),
                pl.BlockSpec((seq, hps * DV_RET), row),
                pl.BlockSpec((seq, DK_RET), lambda b, h: (0, 0)),
                pl.BlockSpec((seq, DK_RET), lambda b, h: (0, 0)),
                pl.BlockSpec((1, hps * DV_RET), lambda b, h: (0, h)),
                pl.BlockSpec((hps, 1, LANES), lambda b, h: (h, 0, 0)),
                state]
    args = [q_r, k_r, v_r, g_r, cos, sin, g_ret_out.reshape(1, -1), lg, s0]
    out_specs = [pl.BlockSpec((seq, hps * DV_RET), row), state]
    out_shape = [jax.ShapeDtypeStruct((m, heads * DV_RET), BF16),
                 jax.ShapeDtypeStruct((batch, heads, DK_RET, DV_RET), F32)]
    side = _side_cast_specs(cast_w, grid) if cast_w is not None else None
    if side:
        in_specs.append(side[0])
        args.append(cast_w)
        out_specs.append(side[0])
        out_shape.append(side[1])
    outs = pl.pallas_call(
        functools.partial(_retention_kernel, blk=blk, heads=hps, side=bool(side)),
        grid=grid,
        in_specs=in_specs,
        out_specs=out_specs,
        out_shape=out_shape,
        compiler_params=_params("arbitrary", "arbitrary"),
        name="retention",
    )(*args)
    if cast_w is None:
        return outs[0], outs[1]
    return outs[0], outs[1], (outs[2] if side else cast_w.astype(BF16))


def _rotary_tables(pos):
    half = DK_RET // 2
    inv = ROPE_BASE ** (-jnp.arange(half, dtype=F32) / half)
    ang = pos.astype(F32)[:, None] * inv[None, :]
    cos = jnp.cos(ang)
    sin = jnp.sin(ang)
    return jnp.concatenate([cos, cos], axis=-1), jnp.concatenate([-sin, sin], axis=-1)


def _in_proj_weights(w_in, b_forget, h_fox, h_ret):
    fw = h_fox * HD_FOX
    qkw = h_ret * DK_RET
    vw = h_ret * DV_RET
    o1, o2, o3 = fw, 2 * fw, 3 * fw
    o4 = o3 + h_fox
    o5 = o4 + qkw
    o6 = o5 + qkw
    o7 = o6 + vw
    seg = lambda a, b: w_in[:, a:b].astype(BF16)
    pad = LANES - h_fox
    return dict(
        q_f=seg(0, o1), k_f=seg(o1, o2), v_f=seg(o2, o3),
        logf=jnp.pad(w_in[:, o3:o4], ((0, 0), (0, pad))).astype(BF16),
        b_logf=jnp.pad(b_forget, (0, pad)).reshape(1, LANES).astype(F32),
        q_r=seg(o4, o5), k_r=seg(o5, o6), v_r=seg(o6, o7), g_r=seg(o7, o7 + vw))


def _project(x2d, g_attn, w):
    h = _rmsnorm(x2d, g_attn, BF16)
    tm = 1024
    return dict(
        q_f=_matmul(h, w["q_f"], BF16, tm_pref=tm),
        k_f=_matmul(h, w["k_f"], F32, tm_pref=tm),
        v_f=_matmul(h, w["v_f"], F32, tm_pref=tm),
        logf=_matmul(h, w["logf"], F32, epilogue="logsig", bias=w["b_logf"], tm_pref=tm),
        q_r=_matmul(h, w["q_r"], F32, tm_pref=tm),
        k_r=_matmul(h, w["k_r"], F32, tm_pref=tm),
        v_r=_matmul(h, w["v_r"], BF16, tm_pref=tm),
        g_r=_matmul(h, w["g_r"], F32, tm_pref=tm))


def _finish_layer(x2d, o_fox, r, g_fox_out, g_mlp, w_out_b, w_up_b, w_down):
    x1 = _out_proj(o_fox, g_fox_out, r, w_out_b, x2d)
    h2 = _rmsnorm(x1, g_mlp, BF16)
    if w_down.dtype == BF16:
        u = _matmul(h2, w_up_b, BF16, epilogue="relu2")
    else:
        u, w_down = _matmul(h2, w_up_b, BF16, epilogue="relu2", cast_w=w_down)
    return _matmul_residual_ksplit(u, w_down, x1), w_down


def _retention_log_gamma(h_ret):
    lg = jnp.log1p(-jnp.exp2(-5.0 - jnp.arange(h_ret, dtype=F32)))
    return jnp.broadcast_to(lg[:, None, None], (h_ret, 1, LANES))


def kernel(x_prompt, x_sample, cache_fox_k, cache_fox_v, cache_fox_logf, state_ret,
           g_attn, w_in, b_forget, g_fox_out, g_ret_out, w_out, g_mlp, w_up, w_down, g_final):
    bp, seq, d = x_prompt.shape
    bs, t_new, _ = x_sample.shape
    depth, _, past, h_fox, _ = cache_fox_k.shape
    h_ret = state_ret.shape[2]
    dtype = x_prompt.dtype

    lg = _retention_log_gamma(h_ret)
    cos_p, sin_p = _rotary_tables(jnp.arange(seq))
    cos_s, sin_s = _rotary_tables(past + jnp.arange(t_new))
    blk_p = _tile(seq, 256, CHUNK)

    yp = x_prompt.reshape(bp * seq, d)
    ys = x_sample.reshape(bs * t_new, d)
    outs = [[] for _ in range(8)]
    for l in range(depth):
        w = _in_proj_weights(w_in[l], b_forget[l], h_fox, h_ret)

        p = _project(yp, g_attn[l], w)
        logf_p = p["logf"][:, :h_fox]
        c_row = _cumsum_lanes(jnp.swapaxes(logf_p.reshape(bp, seq, h_fox), 1, 2))
        c_col = jnp.swapaxes(c_row, 1, 2).reshape(bp * seq, h_fox)
        o_fox, w_up_b = _fox_prompt(p["q_f"], p["k_f"], p["v_f"], c_col, c_row.reshape(bp * h_fox, 1, seq),
                                    bp, seq, h_fox, w_up[l])
        s0 = jnp.zeros((bp, h_ret, DK_RET, DV_RET), F32)
        r, s_fin, w_out_b = _retention(p["q_r"], p["k_r"], p["v_r"], p["g_r"], cos_p, sin_p, g_ret_out[l], lg,
                                       s0, bp, seq, h_ret, blk_p, 1, cast_w=w_out[l])
        yp, w_down_b = _finish_layer(yp, o_fox, r, g_fox_out[l], g_mlp[l], w_out_b, w_up_b, w_down[l])
        outs[0].append(p["k_f"].reshape(bp, seq, h_fox, HD_FOX))
        outs[1].append(p["v_f"].reshape(bp, seq, h_fox, HD_FOX))
        outs[2].append(logf_p.reshape(bp, seq, h_fox).astype(dtype))
        outs[3].append(s_fin.astype(dtype))

        p = _project(ys, g_attn[l], w)
        logf_s = p["logf"][:, :h_fox]
        total = past + t_new
        padded = -(-total // LANES) * LANES
        logf_all = jnp.concatenate(
            [jnp.swapaxes(cache_fox_logf[l].astype(F32), 1, 2),
             jnp.swapaxes(logf_s.reshape(bs, t_new, h_fox), 1, 2),
             jnp.zeros((bs, h_fox, padded - total), F32)], axis=2)
        c_all = _cumsum_lanes(logf_all)
        ckc = c_all[:, :, :past]
        ckn = c_all[:, :, past:total]
        cq = jnp.swapaxes(ckn, 1, 2).reshape(bs * t_new, h_fox)
        o_fox = _fox_sample(p["q_f"], cache_fox_k[l], cache_fox_v[l], p["k_f"], p["v_f"],
                            cq, ckc, ckn, bs, t_new, h_fox)
        r, s_new = _retention(p["q_r"], p["k_r"], p["v_r"], p["g_r"], cos_s, sin_s, g_ret_out[l], lg,
                              state_ret[l].astype(F32), bs, t_new, h_ret, t_new, h_ret)
        ys, _ = _finish_layer(ys, o_fox, r, g_fox_out[l], g_mlp[l], w_out_b, w_up_b, w_down_b)
        outs[4].append(p["k_f"].reshape(bs, t_new, h_fox, HD_FOX))
        outs[5].append(p["v_f"].reshape(bs, t_new, h_fox, HD_FOX))
        outs[6].append(logf_s.reshape(bs, t_new, h_fox).astype(dtype))
        outs[7].append(s_new.astype(dtype))

    y_prompt = _rmsnorm(yp, g_final, dtype).reshape(bp, seq, d)
    y_sample = _rmsnorm(ys, g_final, dtype).reshape(bs, t_new, d)
    return (y_prompt, y_sample) + tuple(jnp.stack(o) for o in outs)
```

```python
import functools

import jax
import jax.numpy as jnp
from jax import lax
from jax.experimental import pallas as pl
from jax.experimental.pallas import tpu as pltpu

CHUNK = 64
HD_FOX = 128
DK_RET = 128
DV_RET = 256
ROPE_BASE = 10000.0
EPS = 1e-6

LANES = 128
SUBLANES = 8
VMEM_LIMIT_BYTES = 62 * 1024 * 1024

F32 = jnp.float32
BF16 = jnp.bfloat16


def _params(*sem):
    return pltpu.CompilerParams(dimension_semantics=sem, vmem_limit_bytes=VMEM_LIMIT_BYTES)


def _tile(n, pref, mult=LANES):
    if n <= pref:
        return n
    t = (pref // mult) * mult
    while t > mult and n % t:
        t -= mult
    assert n % t == 0, (n, pref)
    return t


def _rms_kernel(x_ref, g_ref, o_ref):
    x = x_ref[...]
    ms = jnp.mean(x * x, axis=-1, keepdims=True)
    o_ref[...] = (x * lax.rsqrt(ms + EPS) * g_ref[...]).astype(o_ref.dtype)


def _rmsnorm(x, g, out_dtype):
    m, d = x.shape
    tr = _tile(m, 256, 8)
    return pl.pallas_call(
        _rms_kernel,
        grid=(m // tr,),
        in_specs=[pl.BlockSpec((tr, d), lambda i: (i, 0)),
                  pl.BlockSpec((1, d), lambda i: (0, 0))],
        out_specs=pl.BlockSpec((tr, d), lambda i: (i, 0)),
        out_shape=jax.ShapeDtypeStruct((m, d), out_dtype),
        compiler_params=_params("arbitrary"),
        name="rmsnorm",
    )(x, g.reshape(1, d))


def _log_sigmoid(z):
    return jnp.minimum(z, 0.0) - jnp.log(1.0 + jnp.exp(-jnp.abs(z)))


def _side_cast_specs(w, grid):
    steps = 1
    for g in grid:
        steps *= g
    rows, cols = w.shape
    if rows % steps or (rows // steps) % 16:
        return None

    def index_map(*idx):
        lin = idx[0]
        for g, i in zip(grid[1:], idx[1:]):
            lin = lin * g + i
        return (lin, 0)

    spec = pl.BlockSpec((rows // steps, cols), index_map)
    return spec, jax.ShapeDtypeStruct((rows, cols), BF16)


def _mm_kernel(a_ref, w_ref, *rest, epilogue, side, logf):
    rest = list(rest)
    side_out = rest.pop() if side else None
    lf_out = rest.pop() if logf else None
    o_ref = rest.pop()
    if side:
        side_out[...] = rest.pop()[...].astype(BF16)
    a = a_ref[...]
    if logf:
        wlf_ref, b_ref = rest

        @pl.when(pl.program_id(1) == 0)
        def _():
            lf_out[...] = _log_sigmoid(jnp.dot(a, wlf_ref[...], preferred_element_type=F32) + b_ref[...])

    acc = jnp.dot(a, w_ref[...], preferred_element_type=F32)
    if epilogue == "relu2":
        u = jnp.maximum(acc, 0.0)
        o_ref[...] = (u * u).astype(o_ref.dtype)
    else:
        o_ref[...] = acc.astype(o_ref.dtype)


def _matmul(a, w, out_dtype, epilogue="cast", tm_pref=1024, tn_pref=1024, cast_w=None, logf=None):
    m, k = a.shape
    n = w.shape[1]
    tm = _tile(m, tm_pref, 16)
    tn = _tile(n, tn_pref)
    grid = (m // tm, n // tn)
    in_specs = [pl.BlockSpec((tm, k), lambda i, j: (i, 0)),
                pl.BlockSpec((k, tn), lambda i, j: (0, j))]
    args = [a, w]
    out_specs = [pl.BlockSpec((tm, tn), lambda i, j: (i, j))]
    out_shape = [jax.ShapeDtypeStruct((m, n), out_dtype)]
    if logf is not None:
        in_specs += [pl.BlockSpec((k, LANES), lambda i, j: (0, 0)), pl.BlockSpec((1, LANES), lambda i, j: (0, 0))]
        args += list(logf)
        out_specs.append(pl.BlockSpec((tm, LANES), lambda i, j: (i, 0)))
        out_shape.append(jax.ShapeDtypeStruct((m, LANES), F32))
    side = _side_cast_specs(cast_w, grid) if cast_w is not None else None
    if side:
        in_specs.append(side[0])
        args.append(cast_w)
        out_specs.append(side[0])
        out_shape.append(side[1])
    outs = list(pl.pallas_call(
        functools.partial(_mm_kernel, epilogue=epilogue, side=bool(side), logf=logf is not None),
        grid=grid,
        in_specs=in_specs,
        out_specs=out_specs,
        out_shape=out_shape,
        compiler_params=_params("arbitrary", "arbitrary"),
        name="matmul_" + epilogue,
    )(*args))
    if cast_w is not None and not side:
        outs.append(cast_w.astype(BF16))
    return outs[0] if len(outs) == 1 else tuple(outs)


def _mix_mm_kernel(o_ref, gf_ref, r_ref, w_ref, res_ref, gn_ref, out_ref, h_ref, a_scr, x_scr, *, fw):
    j = pl.program_id(1)
    nj, _, tn = x_scr.shape

    @pl.when(j == 0)
    def _():
        o = o_ref[...]
        ms = jnp.mean(o * o, axis=-1, keepdims=True)
        a_scr[:, :fw] = (o * lax.rsqrt(ms + EPS) * gf_ref[...]).astype(BF16)
        a_scr[:, fw:] = r_ref[...]

    x = res_ref[...] + jnp.dot(a_scr[...], w_ref[...], preferred_element_type=F32)
    out_ref[...] = x
    x_scr[j] = x

    @pl.when(j == nj - 1)
    def _():
        sq = x_scr[0] * x_scr[0]
        ssum = jnp.sum(sq, axis=-1, keepdims=True)
        for t in range(1, nj):
            xt = x_scr[t]
            ssum = ssum + jnp.sum(xt * xt, axis=-1, keepdims=True)
        rstd = lax.rsqrt(ssum / (nj * tn) + EPS)
        for t in range(nj):
            cols = slice(t * tn, (t + 1) * tn)
            h_ref[:, cols] = (x_scr[t] * rstd * gn_ref[:, cols]).astype(h_ref.dtype)


def _out_proj(o_fox, g_fox, r, w, res, g_next):
    m, fw = o_fox.shape
    rw = r.shape[1]
    n = w.shape[1]
    tm = _tile(m, 512, 16)
    tn = _tile(n, 1024)
    return pl.pallas_call(
        functools.partial(_mix_mm_kernel, fw=fw),
        grid=(m // tm, n // tn),
        in_specs=[pl.BlockSpec((tm, fw), lambda i, j: (i, 0)),
                  pl.BlockSpec((1, fw), lambda i, j: (0, 0)),
                  pl.BlockSpec((tm, rw), lambda i, j: (i, 0)),
                  pl.BlockSpec((fw + rw, tn), lambda i, j: (0, j)),
                  pl.BlockSpec((tm, tn), lambda i, j: (i, j)),
                  pl.BlockSpec((1, n), lambda i, j: (0, 0))],
        out_specs=[pl.BlockSpec((tm, tn), lambda i, j: (i, j)),
                   pl.BlockSpec((tm, n), lambda i, j: (i, 0))],
        out_shape=[jax.ShapeDtypeStruct((m, n), F32), jax.ShapeDtypeStruct((m, n), BF16)],
        scratch_shapes=[pltpu.VMEM((tm, fw + rw), BF16), pltpu.VMEM((n // tn, tm, tn), F32)],
        compiler_params=_params("arbitrary", "arbitrary"),
        name="out_proj",
    )(o_fox, g_fox.reshape(1, fw), r, w, res, g_next.reshape(1, n))


def _mm_acc_kernel(a_ref, w_ref, res_ref, out_ref):
    part = jnp.dot(a_ref[...], w_ref[...], preferred_element_type=F32)

    @pl.when(pl.program_id(2) == 0)
    def _():
        out_ref[...] = res_ref[...] + part

    @pl.when(pl.program_id(2) != 0)
    def _():
        out_ref[...] += part


def _matmul_residual_ksplit(a, w, res):
    m, k = a.shape
    n = w.shape[1]
    tm = _tile(m, 1024, 16)
    tn = _tile(n, 1024)
    tk = _tile(k, 4096)
    return pl.pallas_call(
        _mm_acc_kernel,
        grid=(m // tm, n // tn, k // tk),
        in_specs=[pl.BlockSpec((tm, tk), lambda i, j, kk: (i, kk)),
                  pl.BlockSpec((tk, tn), lambda i, j, kk: (kk, j)),
                  pl.BlockSpec((tm, tn), lambda i, j, kk: (i, j))],
        out_specs=pl.BlockSpec((tm, tn), lambda i, j, kk: (i, j)),
        out_shape=jax.ShapeDtypeStruct((m, n), F32),
        compiler_params=_params("arbitrary", "arbitrary", "arbitrary"),
        name="down_proj",
    )(a, w, res)


def _cumsum_kernel(x_ref, o_ref):
    x = x_ref[...]
    t = x.shape[1]
    lane = lax.broadcasted_iota(jnp.int32, x.shape, 1)
    d = 1
    while d < t:
        x = x + jnp.where(lane >= d, pltpu.roll(x, d, axis=1), 0.0)
        d *= 2
    o_ref[...] = x


def _cumsum_lanes(x):
    b, h, t = x.shape
    return pl.pallas_call(
        _cumsum_kernel,
        grid=(b,),
        in_specs=[pl.BlockSpec((None, h, t), lambda i: (i, 0, 0))],
        out_specs=pl.BlockSpec((None, h, t), lambda i: (i, 0, 0)),
        out_shape=jax.ShapeDtypeStruct((b, h, t), F32),
        compiler_params=_params("arbitrary"),
        name="cumsum_logf",
    )(x)


def _qk(q_bf16, k_bf16):
    return lax.dot_general(q_bf16, k_bf16, (((1,), (1,)), ((), ())), preferred_element_type=F32)


def _head_column(c_ref, h):
    c_all = c_ref[...]
    head = lax.broadcasted_iota(jnp.int32, c_all.shape, 1)
    return jnp.sum(jnp.where(head == h, c_all, 0.0), axis=1, keepdims=True)


def _fox_prompt_kernel(q_ref, k_ref, v_ref, cq_ref, ck_ref, *rest, tq, side):
    if side:
        side_in, o_ref, side_out, kt_scr, vb_scr = rest
        side_out[...] = side_in[...].astype(BF16)
    else:
        o_ref, kt_scr, vb_scr = rest
    h = pl.program_id(1)
    seq = q_ref.shape[0]
    kt_scr[...] = k_ref[...].T.astype(BF16)
    vb_scr[...] = v_ref[...].astype(BF16)
    cq = _head_column(cq_ref, h)
    ck = ck_ref[...]
    scale = HD_FOX ** -0.5
    row = lax.broadcasted_iota(jnp.int32, (tq, tq), 0)
    col = lax.broadcasted_iota(jnp.int32, (tq, tq), 1)
    causal = col <= row
    for qi in range(seq // tq):
        lo, hi = qi * tq, (qi + 1) * tq
        q = q_ref[lo:hi, :]
        cq_b = cq[lo:hi]
        s_d = jnp.dot(q, kt_scr[:, lo:hi], preferred_element_type=F32) * scale + cq_b - ck[:, lo:hi]
        s_d = jnp.where(causal, s_d, -jnp.inf)
        m = jnp.max(s_d, axis=1, keepdims=True)
        if qi > 0:
            s_o = jnp.dot(q, kt_scr[:, :lo], preferred_element_type=F32) * scale + cq_b - ck[:, :lo]
            m = jnp.maximum(m, jnp.max(s_o, axis=1, keepdims=True))
            p_o = jnp.exp(s_o - m)
            l = jnp.sum(p_o, axis=1, keepdims=True)
            acc = jnp.dot(p_o.astype(BF16), vb_scr[:lo, :], preferred_element_type=F32)
        p_d = jnp.exp(s_d - m)
        l_d = jnp.sum(p_d, axis=1, keepdims=True)
        acc_d = jnp.dot(p_d.astype(BF16), vb_scr[lo:hi, :], preferred_element_type=F32)
        if qi > 0:
            l_d = l_d + l
            acc_d = acc_d + acc
        o_ref[lo:hi, :] = acc_d / l_d


def _fox_prompt(q, k, v, c_col, c_row, batch, seq, heads, cast_w):
    m, fw = k.shape
    tq = _tile(seq, 256)
    grid = (batch, heads)
    head_blk = lambda b, h: (b, h)
    in_specs = [pl.BlockSpec((seq, HD_FOX), head_blk),
                pl.BlockSpec((seq, HD_FOX), head_blk),
                pl.BlockSpec((seq, HD_FOX), head_blk),
                pl.BlockSpec((seq, heads), lambda b, h: (b, 0)),
                pl.BlockSpec((None, 1, seq), lambda b, h: (b * heads + h, 0, 0))]
    args = [q, k, v, c_col, c_row]
    out_specs = [pl.BlockSpec((seq, HD_FOX), head_blk)]
    out_shape = [jax.ShapeDtypeStruct((m, fw), F32)]
    side = _side_cast_specs(cast_w, grid)
    if side:
        in_specs.append(side[0])
        args.append(cast_w)
        out_specs.append(side[0])
        out_shape.append(side[1])
    outs = pl.pallas_call(
        functools.partial(_fox_prompt_kernel, tq=tq, side=bool(side)),
        grid=grid,
        in_specs=in_specs,
        out_specs=out_specs,
        out_shape=out_shape,
        scratch_shapes=[pltpu.VMEM((HD_FOX, seq), BF16), pltpu.VMEM((seq, HD_FOX), BF16)],
        compiler_params=_params("arbitrary", "arbitrary"),
        name="fox_prompt",
    )(*args)
    return outs[0], (outs[1] if side else cast_w.astype(BF16))


def _every_eighth_row(ref, j):
    n, s, d = ref.shape
    return ref.reshape(n * s, d)[pl.ds(j, n, stride=s), :]


def _fox_sample_kernel(*refs, heads):
    ng = heads // SUBLANES
    q_ref = refs[0]
    k_refs = refs[1:1 + ng]
    v_refs = refs[1 + ng:1 + 2 * ng]
    kn_ref, vn_ref, cq_ref, ckc_ref, ckn_ref, o_ref, m_scr, l_scr, acc_scr, s_scr, p_scr = refs[1 + 2 * ng:]
    ki = pl.program_id(1)
    last = pl.num_programs(1) - 1
    scale = HD_FOX ** -0.5
    t = q_ref.shape[0]
    head_cols = lambda h: slice(h * HD_FOX, (h + 1) * HD_FOX)
    head_rows = lambda h: slice(h * t, (h + 1) * t)

    @pl.when(ki == 0)
    def _():
        m_scr[...] = jnp.full(m_scr.shape, -jnp.inf, F32)
        l_scr[...] = jnp.zeros(l_scr.shape, F32)
        acc_scr[...] = jnp.zeros(acc_scr.shape, F32)

    def cached(group_refs, h):
        return _every_eighth_row(group_refs[h // SUBLANES], h % SUBLANES).astype(BF16)

    def update(load_k, load_v, ck_ref, n, causal):
        for h in range(heads):
            s = _qk(q_ref[:, head_cols(h)], load_k(h)) * scale + cq_ref[:, h:h + 1] - ck_ref[h:h + 1, :]
            if causal:
                row = lax.broadcasted_iota(jnp.int32, s.shape, 0)
                col = lax.broadcasted_iota(jnp.int32, s.shape, 1)
                s = jnp.where(col <= row, s, -jnp.inf)
            s_scr[head_rows(h), :n] = s
        s = s_scr[:, :n]
        m_prev = m_scr[:, :1]
        m_new = jnp.maximum(m_prev, jnp.max(s, axis=1, keepdims=True))
        alpha = jnp.exp(m_prev - m_new)
        p = jnp.exp(s - m_new)
        l_new = alpha * l_scr[:, :1] + jnp.sum(p, axis=1, keepdims=True)
        m_scr[...] = jnp.broadcast_to(m_new, m_scr.shape)
        l_scr[...] = jnp.broadcast_to(l_new, l_scr.shape)
        p_scr[:, :n] = p.astype(BF16)
        for h in range(heads):
            pv = jnp.dot(p_scr[head_rows(h), :n], load_v(h), preferred_element_type=F32)
            acc_scr[:, head_cols(h)] = alpha[head_rows(h)] * acc_scr[:, head_cols(h)] + pv

    @pl.when(ki < last)
    def _():
        update(functools.partial(cached, k_refs), functools.partial(cached, v_refs), ckc_ref,
               s_scr.shape[1], causal=False)

    @pl.when(ki == last)
    def _():
        new_rows = lambda ref, h: ref[:, head_cols(h)].astype(BF16)
        update(functools.partial(new_rows, kn_ref), functools.partial(new_rows, vn_ref), ckn_ref, t, causal=True)
        for h in range(heads):
            o_ref[:, head_cols(h)] = acc_scr[:, head_cols(h)] / l_scr[head_rows(h), :1]


def _fox_sample(q, kc, vc, kn, vn, cq, ckc, ckn, batch, t_new, heads):
    past = kc.shape[1]
    fw = heads * HD_FOX
    assert heads % SUBLANES == 0, heads
    ng = heads // SUBLANES
    tk = _tile(past, 1024)
    nkc = past // tk
    kc = kc.reshape(batch, past, ng, SUBLANES, HD_FOX)
    vc = vc.reshape(batch, past, ng, SUBLANES, HD_FOX)
    row_blk = pl.BlockSpec((t_new, fw), lambda b, ki: (b, 0))
    group_blk = [pl.BlockSpec((None, tk, None, SUBLANES, HD_FOX),
                              functools.partial(lambda b, ki, g: (b, jnp.minimum(ki, nkc - 1), g, 0, 0), g=g))
                 for g in range(ng)]
    return pl.pallas_call(
        functools.partial(_fox_sample_kernel, heads=heads),
        grid=(batch, nkc + 1),
        in_specs=[row_blk] + group_blk + group_blk + [
            row_blk, row_blk,
            pl.BlockSpec((t_new, heads), lambda b, ki: (b, 0)),
            pl.BlockSpec((None, heads, tk), lambda b, ki: (b, 0, jnp.minimum(ki, nkc - 1))),
            pl.BlockSpec((None, heads, t_new), lambda b, ki: (b, 0, 0))],
        out_specs=row_blk,
        out_shape=jax.ShapeDtypeStruct((batch * t_new, fw), F32),
        scratch_shapes=[pltpu.VMEM((heads * t_new, LANES), F32), pltpu.VMEM((heads * t_new, LANES), F32),
                        pltpu.VMEM((t_new, fw), F32),
                        pltpu.VMEM((heads * t_new, tk), F32), pltpu.VMEM((heads * t_new, tk), BF16)],
        compiler_params=_params("arbitrary", "arbitrary"),
        name="fox_sample",
    )(q, *([kc] * ng), *([vc] * ng), kn, vn, cq, ckc, ckn)


def _retention_kernel(q_ref, k_ref, v_ref, g_ref, cos_ref, sin_ref, gro_ref, lg_ref, s0_ref, *rest,
                      blk, heads, side):
    if side:
        side_in, r_ref, sout_ref, side_out = rest
        side_out[...] = side_in[...].astype(BF16)
    else:
        r_ref, sout_ref = rest
    seq = q_ref.shape[0]
    half = DK_RET // 2
    ri = lax.broadcasted_iota(jnp.int32, (blk, blk), 0)
    ci = lax.broadcasted_iota(jnp.int32, (blk, blk), 1)
    dist = jnp.abs(ri - ci).astype(F32)
    visible = ci // CHUNK <= ri // CHUNK
    idx = lax.broadcasted_iota(jnp.int32, (blk, 1), 0).astype(F32)
    for hh in range(heads):
        lg = lg_ref[hh, :, :1]
        decay = jnp.where(visible, jnp.exp(lg * dist), 0.0)
        cross = jnp.exp(lg * (idx + 1.0))
        k_decay = jnp.exp(lg * (blk - 1.0 - idx))
        s_decay = jnp.exp(lg * blk)
        qk_cols = slice(hh * DK_RET, (hh + 1) * DK_RET)
        v_cols = slice(hh * DV_RET, (hh + 1) * DV_RET)
        gro = gro_ref[:, v_cols]
        s = s0_ref[hh]
        for t in range(seq // blk):
            rows = slice(t * blk, (t + 1) * blk)
            cos = cos_ref[rows, :]
            sin = sin_ref[rows, :]
            q = q_ref[rows, qk_cols]
            k = k_ref[rows, qk_cols]
            q = q * cos + pltpu.roll(q, half, axis=1) * sin
            k = (k * cos + pltpu.roll(k, half, axis=1) * sin) * (DK_RET ** -0.5)
            v = v_ref[rows, v_cols]
            qb = q.astype(BF16)
            a = _qk(qb, k.astype(BF16)) * decay
            o = jnp.dot(a.astype(BF16), v, preferred_element_type=F32)
            o = o + jnp.dot(qb, s.astype(BF16), preferred_element_type=F32) * cross
            s = s_decay * s + lax.dot_general((k * k_decay).astype(BF16), v, (((0,), (0,)), ((), ())),
                                              preferred_element_type=F32)
            ms = jnp.mean(o * o, axis=-1, keepdims=True)
            g = g_ref[rows, v_cols]
            silu = g / (1.0 + jnp.exp(-g))
            r_ref[rows, v_cols] = (o * lax.rsqrt(ms + EPS) * gro * silu).astype(r_ref.dtype)
        sout_ref[hh] = s


def _retention(qkg, qv, cos, sin, g_ret_out, lg, s0, batch, seq, heads, blk, heads_per_step, fox_width,
               cast_w=None):
    m = qkg.shape[0]
    hps = heads_per_step
    groups = heads // hps
    assert fox_width % (hps * DV_RET) == 0, (fox_width, hps)
    v_off = fox_width // (hps * DV_RET)
    grid = (batch, groups)
    row = lambda b, h: (b, h)
    state = pl.BlockSpec((None, hps, DK_RET, DV_RET), lambda b, h: (b, h, 0, 0))
    in_specs = [pl.BlockSpec((seq, hps * DK_RET), row),
                pl.BlockSpec((seq, hps * DK_RET), lambda b, h: (b, groups + h)),
                pl.BlockSpec((seq, hps * DV_RET), lambda b, h: (b, v_off + h)),
                pl.BlockSpec((seq, hps * DV_RET), lambda b, h: (b, groups + h)),
                pl.BlockSpec((seq, DK_RET), lambda b, h: (0, 0)),
                pl.BlockSpec((seq, DK_RET), lambda b, h: (0, 0)),
                pl.BlockSpec((1, hps * DV_RET), lambda b, h: (0, h)),
                pl.BlockSpec((hps, 1, LANES), lambda b, h: (h, 0, 0)),
                state]
    args = [qkg, qkg, qv, qkg, cos, sin, g_ret_out.reshape(1, -1), lg, s0]
    out_specs = [pl.BlockSpec((seq, hps * DV_RET), row), state]
    out_shape = [jax.ShapeDtypeStruct((m, heads * DV_RET), BF16),
                 jax.ShapeDtypeStruct((batch, heads, DK_RET, DV_RET), F32)]
    side = _side_cast_specs(cast_w, grid) if cast_w is not None else None
    if side:
        in_specs.append(side[0])
        args.append(cast_w)
        out_specs.append(side[0])
        out_shape.append(side[1])
    outs = pl.pallas_call(
        functools.partial(_retention_kernel, blk=blk, heads=hps, side=bool(side)),
        grid=grid,
        in_specs=in_specs,
        out_specs=out_specs,
        out_shape=out_shape,
        compiler_params=_params("arbitrary", "arbitrary"),
        name="retention",
    )(*args)
    if cast_w is None:
        return outs[0], outs[1]
    return outs[0], outs[1], (outs[2] if side else cast_w.astype(BF16))


def _rotary_tables(pos):
    half = DK_RET // 2
    inv = ROPE_BASE ** (-jnp.arange(half, dtype=F32) / half)
    ang = pos.astype(F32)[:, None] * inv[None, :]
    cos = jnp.cos(ang)
    sin = jnp.sin(ang)
    return jnp.concatenate([cos, cos], axis=-1), jnp.concatenate([-sin, sin], axis=-1)


def _in_proj_weights(w_in, b_forget, h_fox, h_ret):
    fw = h_fox * HD_FOX
    qkw = h_ret * DK_RET
    vw = h_ret * DV_RET
    o1, o2, o3 = fw, 2 * fw, 3 * fw
    o4 = o3 + h_fox
    o6 = o4 + 2 * qkw
    o7 = o6 + vw
    pad = LANES - h_fox
    return dict(
        qv=jnp.concatenate([w_in[:, :o1], w_in[:, o6:o7]], axis=1).astype(BF16),
        k_f=w_in[:, o1:o2].astype(BF16), v_f=w_in[:, o2:o3].astype(BF16),
        qkg=jnp.concatenate([w_in[:, o4:o6], w_in[:, o7:o7 + vw]], axis=1).astype(BF16),
        logf=jnp.pad(w_in[:, o3:o4], ((0, 0), (0, pad))).astype(BF16),
        b_logf=jnp.pad(b_forget, (0, pad)).reshape(1, LANES).astype(F32))


def _project(x2d, g_attn, w):
    h = _rmsnorm(x2d, g_attn, BF16)
    qkg, logf = _matmul(h, w["qkg"], F32, logf=(w["logf"], w["b_logf"]))
    return dict(qv=_matmul(h, w["qv"], BF16), k_f=_matmul(h, w["k_f"], F32), v_f=_matmul(h, w["v_f"], F32),
                qkg=qkg, logf=logf)


def _finish_layer(x2d, o_fox, r, g_fox_out, g_mlp, w_out_b, w_up_b, w_down):
    x1, h2 = _out_proj(o_fox, g_fox_out, r, w_out_b, x2d, g_mlp)
    if w_down.dtype == BF16:
        u = _matmul(h2, w_up_b, BF16, epilogue="relu2")
    else:
        u, w_down = _matmul(h2, w_up_b, BF16, epilogue="relu2", cast_w=w_down)
    return _matmul_residual_ksplit(u, w_down, x1), w_down


def _retention_log_gamma(h_ret):
    lg = jnp.log1p(-jnp.exp2(-5.0 - jnp.arange(h_ret, dtype=F32)))
    return jnp.broadcast_to(lg[:, None, None], (h_ret, 1, LANES))


def kernel(x_prompt, x_sample, cache_fox_k, cache_fox_v, cache_fox_logf, state_ret,
           g_attn, w_in, b_forget, g_fox_out, g_ret_out, w_out, g_mlp, w_up, w_down, g_final):
    bp, seq, d = x_prompt.shape
    bs, t_new, _ = x_sample.shape
    depth, _, past, h_fox, _ = cache_fox_k.shape
    h_ret = state_ret.shape[2]
    fw = h_fox * HD_FOX
    dtype = x_prompt.dtype

    lg = _retention_log_gamma(h_ret)
    cos_p, sin_p = _rotary_tables(jnp.arange(seq))
    cos_s, sin_s = _rotary_tables(past + jnp.arange(t_new))
    blk_p = _tile(seq, 256, CHUNK)

    yp = x_prompt.reshape(bp * seq, d)
    ys = x_sample.reshape(bs * t_new, d)
    outs = [[] for _ in range(8)]
    for l in range(depth):
        w = _in_proj_weights(w_in[l], b_forget[l], h_fox, h_ret)

        p = _project(yp, g_attn[l], w)
        logf_p = p["logf"][:, :h_fox]
        c_row = _cumsum_lanes(jnp.swapaxes(logf_p.reshape(bp, seq, h_fox), 1, 2))
        c_col = jnp.swapaxes(c_row, 1, 2).reshape(bp * seq, h_fox)
        o_fox, w_up_b = _fox_prompt(p["qv"], p["k_f"], p["v_f"], c_col, c_row.reshape(bp * h_fox, 1, seq),
                                    bp, seq, h_fox, w_up[l])
        s0 = jnp.zeros((bp, h_ret, DK_RET, DV_RET), F32)
        r, s_fin, w_out_b = _retention(p["qkg"], p["qv"], cos_p, sin_p, g_ret_out[l], lg,
                                       s0, bp, seq, h_ret, blk_p, 1, fw, cast_w=w_out[l])
        yp, w_down_b = _finish_layer(yp, o_fox, r, g_fox_out[l], g_mlp[l], w_out_b, w_up_b, w_down[l])
        outs[0].append(p["k_f"].reshape(bp, seq, h_fox, HD_FOX))
        outs[1].append(p["v_f"].reshape(bp, seq, h_fox, HD_FOX))
        outs[2].append(logf_p.reshape(bp, seq, h_fox).astype(dtype))
        outs[3].append(s_fin.astype(dtype))

        p = _project(ys, g_attn[l], w)
        logf_s = p["logf"][:, :h_fox]
        total = past + t_new
        padded = -(-total // LANES) * LANES
        logf_all = jnp.concatenate(
            [jnp.swapaxes(cache_fox_logf[l].astype(F32), 1, 2),
             jnp.swapaxes(logf_s.reshape(bs, t_new, h_fox), 1, 2),
             jnp.zeros((bs, h_fox, padded - total), F32)], axis=2)
        c_all = _cumsum_lanes(logf_all)
        ckc = c_all[:, :, :past]
        ckn = c_all[:, :, past:total]
        cq = jnp.swapaxes(ckn, 1, 2).reshape(bs * t_new, h_fox)
        o_fox = _fox_sample(p["qv"], cache_fox_k[l], cache_fox_v[l], p["k_f"], p["v_f"],
                            cq, ckc, ckn, bs, t_new, h_fox)
        r, s_new = _retention(p["qkg"], p["qv"], cos_s, sin_s, g_ret_out[l], lg,
                              state_ret[l].astype(F32), bs, t_new, h_ret, t_new, h_ret, fw)
        ys, _ = _finish_layer(ys, o_fox, r, g_fox_out[l], g_mlp[l], w_out_b, w_up_b, w_down_b)
        outs[4].append(p["k_f"].reshape(bs, t_new, h_fox, HD_FOX))
        outs[5].append(p["v_f"].reshape(bs, t_new, h_fox, HD_FOX))
        outs[6].append(logf_s.reshape(bs, t_new, h_fox).astype(dtype))
        outs[7].append(s_new.astype(dtype))

    y_prompt = _rmsnorm(yp, g_final, dtype).reshape(bp, seq, d)
    y_sample = _rmsnorm(ys, g_final, dtype).reshape(bs, t_new, d)
    return (y_prompt, y_sample) + tuple(jnp.stack(o) for o in outs)
```

```python
import functools

import jax
import jax.numpy as jnp
from jax import lax
from jax.experimental import pallas as pl
from jax.experimental.pallas import tpu as pltpu

CHUNK = 64
HD_FOX = 128
DK_RET = 128
DV_RET = 256
ROPE_BASE = 10000.0
EPS = 1e-6

LANES = 128
SUBLANES = 8
VMEM_LIMIT_BYTES = 62 * 1024 * 1024

F32 = jnp.float32
BF16 = jnp.bfloat16


def _params(*sem):
    return pltpu.CompilerParams(dimension_semantics=sem, vmem_limit_bytes=VMEM_LIMIT_BYTES)


def _tile(n, pref, mult=LANES):
    if n <= pref:
        return n
    t = (pref // mult) * mult
    while t > mult and n % t:
        t -= mult
    assert n % t == 0, (n, pref)
    return t


def _rms_kernel(x_ref, g_ref, o_ref):
    x = x_ref[...]
    ms = jnp.mean(x * x, axis=-1, keepdims=True)
    o_ref[...] = (x * lax.rsqrt(ms + EPS) * g_ref[...]).astype(o_ref.dtype)


def _rmsnorm(x, g, out_dtype):
    m, d = x.shape
    tr = _tile(m, 256, 8)
    return pl.pallas_call(
        _rms_kernel,
        grid=(m // tr,),
        in_specs=[pl.BlockSpec((tr, d), lambda i: (i, 0)),
                  pl.BlockSpec((1, d), lambda i: (0, 0))],
        out_specs=pl.BlockSpec((tr, d), lambda i: (i, 0)),
        out_shape=jax.ShapeDtypeStruct((m, d), out_dtype),
        compiler_params=_params("arbitrary"),
        name="rmsnorm",
    )(x, g.reshape(1, d))


def _log_sigmoid(z):
    return jnp.minimum(z, 0.0) - jnp.log(1.0 + jnp.exp(-jnp.abs(z)))


def _side_cast_specs(w, grid):
    steps = 1
    for g in grid:
        steps *= g
    rows, cols = w.shape
    if rows % steps or (rows // steps) % 16:
        return None

    def index_map(*idx):
        lin = idx[0]
        for g, i in zip(grid[1:], idx[1:]):
            lin = lin * g + i
        return (lin, 0)

    spec = pl.BlockSpec((rows // steps, cols), index_map)
    return spec, jax.ShapeDtypeStruct((rows, cols), BF16)


def _mm_kernel(a_ref, w_ref, *rest, epilogue, side, logf):
    rest = list(rest)
    side_out = rest.pop() if side else None
    lf_out = rest.pop() if logf else None
    o_ref = rest.pop()
    if side:
        side_out[...] = rest.pop()[...].astype(BF16)
    a = a_ref[...]
    if logf:
        wlf_ref, b_ref = rest

        @pl.when(pl.program_id(1) == 0)
        def _():
            lf_out[...] = _log_sigmoid(jnp.dot(a, wlf_ref[...], preferred_element_type=F32) + b_ref[...])

    acc = jnp.dot(a, w_ref[...], preferred_element_type=F32)
    if epilogue == "relu2":
        u = jnp.maximum(acc, 0.0)
        o_ref[...] = (u * u).astype(o_ref.dtype)
    else:
        o_ref[...] = acc.astype(o_ref.dtype)


def _matmul(a, w, out_dtype, epilogue="cast", tm_pref=1024, tn_pref=1024, cast_w=None, logf=None):
    m, k = a.shape
    n = w.shape[1]
    tm = _tile(m, tm_pref, 16)
    tn = _tile(n, tn_pref)
    grid = (m // tm, n // tn)
    in_specs = [pl.BlockSpec((tm, k), lambda i, j: (i, 0)),
                pl.BlockSpec((k, tn), lambda i, j: (0, j))]
    args = [a, w]
    out_specs = [pl.BlockSpec((tm, tn), lambda i, j: (i, j))]
    out_shape = [jax.ShapeDtypeStruct((m, n), out_dtype)]
    if logf is not None:
        in_specs += [pl.BlockSpec((k, LANES), lambda i, j: (0, 0)), pl.BlockSpec((1, LANES), lambda i, j: (0, 0))]
        args += list(logf)
        out_specs.append(pl.BlockSpec((tm, LANES), lambda i, j: (i, 0)))
        out_shape.append(jax.ShapeDtypeStruct((m, LANES), F32))
    side = _side_cast_specs(cast_w, grid) if cast_w is not None else None
    if side:
        in_specs.append(side[0])
        args.append(cast_w)
        out_specs.append(side[0])
        out_shape.append(side[1])
    outs = list(pl.pallas_call(
        functools.partial(_mm_kernel, epilogue=epilogue, side=bool(side), logf=logf is not None),
        grid=grid,
        in_specs=in_specs,
        out_specs=out_specs,
        out_shape=out_shape,
        compiler_params=_params("arbitrary", "arbitrary"),
        name="matmul_" + epilogue,
    )(*args))
    if cast_w is not None and not side:
        outs.append(cast_w.astype(BF16))
    return outs[0] if len(outs) == 1 else tuple(outs)


def _mix_mm_kernel(o_ref, gf_ref, r_ref, w_ref, res_ref, gn_ref, out_ref, h_ref, a_scr, x_scr, *, fw):
    j = pl.program_id(1)
    nj, _, tn = x_scr.shape

    @pl.when(j == 0)
    def _():
        o = o_ref[...]
        ms = jnp.mean(o * o, axis=-1, keepdims=True)
        a_scr[:, :fw] = (o * lax.rsqrt(ms + EPS) * gf_ref[...]).astype(BF16)
        a_scr[:, fw:] = r_ref[...]

    x = res_ref[...] + jnp.dot(a_scr[...], w_ref[...], preferred_element_type=F32)
    out_ref[...] = x
    x_scr[j] = x

    @pl.when(j == nj - 1)
    def _():
        sq = x_scr[0] * x_scr[0]
        ssum = jnp.sum(sq, axis=-1, keepdims=True)
        for t in range(1, nj):
            xt = x_scr[t]
            ssum = ssum + jnp.sum(xt * xt, axis=-1, keepdims=True)
        rstd = lax.rsqrt(ssum / (nj * tn) + EPS)
        for t in range(nj):
            cols = slice(t * tn, (t + 1) * tn)
            h_ref[:, cols] = (x_scr[t] * rstd * gn_ref[:, cols]).astype(h_ref.dtype)


def _out_proj(o_fox, g_fox, r, w, res, g_next):
    m, fw = o_fox.shape
    rw = r.shape[1]
    n = w.shape[1]
    tm = _tile(m, 512, 16)
    tn = _tile(n, 1024)
    return pl.pallas_call(
        functools.partial(_mix_mm_kernel, fw=fw),
        grid=(m // tm, n // tn),
        in_specs=[pl.BlockSpec((tm, fw), lambda i, j: (i, 0)),
                  pl.BlockSpec((1, fw), lambda i, j: (0, 0)),
                  pl.BlockSpec((tm, rw), lambda i, j: (i, 0)),
                  pl.BlockSpec((fw + rw, tn), lambda i, j: (0, j)),
                  pl.BlockSpec((tm, tn), lambda i, j: (i, j)),
                  pl.BlockSpec((1, n), lambda i, j: (0, 0))],
        out_specs=[pl.BlockSpec((tm, tn), lambda i, j: (i, j)),
                   pl.BlockSpec((tm, n), lambda i, j: (i, 0))],
        out_shape=[jax.ShapeDtypeStruct((m, n), F32), jax.ShapeDtypeStruct((m, n), BF16)],
        scratch_shapes=[pltpu.VMEM((tm, fw + rw), BF16), pltpu.VMEM((n // tn, tm, tn), F32)],
        compiler_params=_params("arbitrary", "arbitrary"),
        name="out_proj",
    )(o_fox, g_fox.reshape(1, fw), r, w, res, g_next.reshape(1, n))


def _mm_acc_kernel(a_ref, w_ref, res_ref, out_ref):
    @pl.when(pl.program_id(2) == 0)
    def _():
        out_ref[...] = res_ref[...]

    out_ref[...] += jnp.dot(a_ref[...], w_ref[...], preferred_element_type=F32)


def _matmul_residual_ksplit(a, w, res):
    m, k = a.shape
    n = w.shape[1]
    tm = _tile(m, 1024, 16)
    tn = _tile(n, 1024)
    tk = _tile(k, 4096)
    return pl.pallas_call(
        _mm_acc_kernel,
        grid=(m // tm, n // tn, k // tk),
        in_specs=[pl.BlockSpec((tm, tk), lambda i, j, kk: (i, kk)),
                  pl.BlockSpec((tk, tn), lambda i, j, kk: (kk, j)),
                  pl.BlockSpec((tm, tn), lambda i, j, kk: (i, j))],
        out_specs=pl.BlockSpec((tm, tn), lambda i, j, kk: (i, j)),
        out_shape=jax.ShapeDtypeStruct((m, n), F32),
        compiler_params=_params("arbitrary", "arbitrary", "arbitrary"),
        name="down_proj",
    )(a, w, res)


def _cumsum_kernel(x_ref, o_ref):
    x = x_ref[...]
    t = x.shape[1]
    lane = lax.broadcasted_iota(jnp.int32, x.shape, 1)
    d = 1
    while d < t:
        x = x + jnp.where(lane >= d, pltpu.roll(x, d, axis=1), 0.0)
        d *= 2
    o_ref[...] = x


def _cumsum_lanes(x):
    b, h, t = x.shape
    return pl.pallas_call(
        _cumsum_kernel,
        grid=(b,),
        in_specs=[pl.BlockSpec((None, h, t), lambda i: (i, 0, 0))],
        out_specs=pl.BlockSpec((None, h, t), lambda i: (i, 0, 0)),
        out_shape=jax.ShapeDtypeStruct((b, h, t), F32),
        compiler_params=_params("arbitrary"),
        name="cumsum_logf",
    )(x)


def _qk(q_bf16, k_bf16):
    return lax.dot_general(q_bf16, k_bf16, (((1,), (1,)), ((), ())), preferred_element_type=F32)


def _head_column(c_ref, h):
    c_all = c_ref[...]
    head = lax.broadcasted_iota(jnp.int32, c_all.shape, 1)
    return jnp.sum(jnp.where(head == h, c_all, 0.0), axis=1, keepdims=True)


def _fox_prompt_kernel(q_ref, k_ref, v_ref, cq_ref, ck_ref, *rest, tq, side):
    if side:
        side_in, o_ref, side_out, kt_scr, vb_scr = rest
        side_out[...] = side_in[...].astype(BF16)
    else:
        o_ref, kt_scr, vb_scr = rest
    h = pl.program_id(1)
    seq = q_ref.shape[0]
    kt_scr[...] = k_ref[...].T.astype(BF16)
    vb_scr[...] = v_ref[...].astype(BF16)
    cq = _head_column(cq_ref, h)
    ck = ck_ref[...]
    scale = HD_FOX ** -0.5
    row = lax.broadcasted_iota(jnp.int32, (tq, tq), 0)
    col = lax.broadcasted_iota(jnp.int32, (tq, tq), 1)
    causal = col <= row
    for qi in range(seq // tq):
        lo, hi = qi * tq, (qi + 1) * tq
        q = q_ref[lo:hi, :]
        cq_b = cq[lo:hi]
        s_d = jnp.dot(q, kt_scr[:, lo:hi], preferred_element_type=F32) * scale + cq_b - ck[:, lo:hi]
        s_d = jnp.where(causal, s_d, -jnp.inf)
        m = jnp.max(s_d, axis=1, keepdims=True)
        if qi > 0:
            s_o = jnp.dot(q, kt_scr[:, :lo], preferred_element_type=F32) * scale + cq_b - ck[:, :lo]
            m = jnp.maximum(m, jnp.max(s_o, axis=1, keepdims=True))
            p_o = jnp.exp(s_o - m)
            l = jnp.sum(p_o, axis=1, keepdims=True)
            acc = jnp.dot(p_o.astype(BF16), vb_scr[:lo, :], preferred_element_type=F32)
        p_d = jnp.exp(s_d - m)
        l_d = jnp.sum(p_d, axis=1, keepdims=True)
        acc_d = jnp.dot(p_d.astype(BF16), vb_scr[lo:hi, :], preferred_element_type=F32)
        if qi > 0:
            l_d = l_d + l
            acc_d = acc_d + acc
        o_ref[lo:hi, :] = acc_d / l_d


def _fox_prompt(q, k, v, c_col, c_row, batch, seq, heads, cast_w):
    m, fw = k.shape
    tq = _tile(seq, 256)
    grid = (batch, heads)
    head_blk = lambda b, h: (b, h)
    in_specs = [pl.BlockSpec((seq, HD_FOX), head_blk),
                pl.BlockSpec((seq, HD_FOX), head_blk),
                pl.BlockSpec((seq, HD_FOX), head_blk),
                pl.BlockSpec((seq, heads), lambda b, h: (b, 0)),
                pl.BlockSpec((None, 1, seq), lambda b, h: (b * heads + h, 0, 0))]
    args = [q, k, v, c_col, c_row]
    out_specs = [pl.BlockSpec((seq, HD_FOX), head_blk)]
    out_shape = [jax.ShapeDtypeStruct((m, fw), F32)]
    side = _side_cast_specs(cast_w, grid)
    if side:
        in_specs.append(side[0])
        args.append(cast_w)
        out_specs.append(side[0])
        out_shape.append(side[1])
    outs = pl.pallas_call(
        functools.partial(_fox_prompt_kernel, tq=tq, side=bool(side)),
        grid=grid,
        in_specs=in_specs,
        out_specs=out_specs,
        out_shape=out_shape,
        scratch_shapes=[pltpu.VMEM((HD_FOX, seq), BF16), pltpu.VMEM((seq, HD_FOX), BF16)],
        compiler_params=_params("arbitrary", "arbitrary"),
        name="fox_prompt",
    )(*args)
    return outs[0], (outs[1] if side else cast_w.astype(BF16))


def _every_eighth_row(ref, j):
    n, s, d = ref.shape
    return ref.reshape(n * s, d)[pl.ds(j, n, stride=s), :]


def _fox_sample_kernel(*refs, heads):
    ng = heads // SUBLANES
    q_ref = refs[0]
    k_refs = refs[1:1 + ng]
    v_refs = refs[1 + ng:1 + 2 * ng]
    kn_ref, vn_ref, cq_ref, ckc_ref, ckn_ref, o_ref, m_scr, l_scr, acc_scr, s_scr, p_scr = refs[1 + 2 * ng:]
    ki = pl.program_id(1)
    last = pl.num_programs(1) - 1
    scale = HD_FOX ** -0.5
    t = q_ref.shape[0]
    head_cols = lambda h: slice(h * HD_FOX, (h + 1) * HD_FOX)
    head_rows = lambda h: slice(h * t, (h + 1) * t)

    @pl.when(ki == 0)
    def _():
        m_scr[...] = jnp.full(m_scr.shape, -jnp.inf, F32)
        l_scr[...] = jnp.zeros(l_scr.shape, F32)
        acc_scr[...] = jnp.zeros(acc_scr.shape, F32)

    def cached(group_refs, h):
        return _every_eighth_row(group_refs[h // SUBLANES], h % SUBLANES).astype(BF16)

    def update(load_k, load_v, ck_ref, n, causal):
        for h in range(heads):
            s = _qk(q_ref[:, head_cols(h)], load_k(h)) * scale + cq_ref[:, h:h + 1] - ck_ref[h:h + 1, :]
            if causal:
                row = lax.broadcasted_iota(jnp.int32, s.shape, 0)
                col = lax.broadcasted_iota(jnp.int32, s.shape, 1)
                s = jnp.where(col <= row, s, -jnp.inf)
            s_scr[head_rows(h), :n] = s
        s = s_scr[:, :n]
        m_prev = m_scr[:, :1]
        m_new = jnp.maximum(m_prev, jnp.max(s, axis=1, keepdims=True))
        alpha = jnp.exp(m_prev - m_new)
        p = jnp.exp(s - m_new)
        l_new = alpha * l_scr[:, :1] + jnp.sum(p, axis=1, keepdims=True)
        m_scr[...] = jnp.broadcast_to(m_new, m_scr.shape)
        l_scr[...] = jnp.broadcast_to(l_new, l_scr.shape)
        p_scr[:, :n] = p.astype(BF16)
        for h in range(heads):
            pv = jnp.dot(p_scr[head_rows(h), :n], load_v(h), preferred_element_type=F32)
            acc_scr[:, head_cols(h)] = alpha[head_rows(h)] * acc_scr[:, head_cols(h)] + pv

    @pl.when(ki < last)
    def _():
        update(functools.partial(cached, k_refs), functools.partial(cached, v_refs), ckc_ref,
               s_scr.shape[1], causal=False)

    @pl.when(ki == last)
    def _():
        new_rows = lambda ref, h: ref[:, head_cols(h)].astype(BF16)
        update(functools.partial(new_rows, kn_ref), functools.partial(new_rows, vn_ref), ckn_ref, t, causal=True)
        for h in range(heads):
            o_ref[:, head_cols(h)] = acc_scr[:, head_cols(h)] / l_scr[head_rows(h), :1]


def _fox_sample(q, kc, vc, kn, vn, cq, ckc, ckn, batch, t_new, heads):
    past = kc.shape[1]
    fw = heads * HD_FOX
    assert heads % SUBLANES == 0, heads
    ng = heads // SUBLANES
    tk = _tile(past, 1024)
    nkc = past // tk
    kc = kc.reshape(batch, past, ng, SUBLANES, HD_FOX)
    vc = vc.reshape(batch, past, ng, SUBLANES, HD_FOX)
    row_blk = pl.BlockSpec((t_new, fw), lambda b, ki: (b, 0))
    group_blk = [pl.BlockSpec((None, tk, None, SUBLANES, HD_FOX),
                              functools.partial(lambda b, ki, g: (b, jnp.minimum(ki, nkc - 1), g, 0, 0), g=g))
                 for g in range(ng)]
    return pl.pallas_call(
        functools.partial(_fox_sample_kernel, heads=heads),
        grid=(batch, nkc + 1),
        in_specs=[row_blk] + group_blk + group_blk + [
            row_blk, row_blk,
            pl.BlockSpec((t_new, heads), lambda b, ki: (b, 0)),
            pl.BlockSpec((None, heads, tk), lambda b, ki: (b, 0, jnp.minimum(ki, nkc - 1))),
            pl.BlockSpec((None, heads, t_new), lambda b, ki: (b, 0, 0))],
        out_specs=row_blk,
        out_shape=jax.ShapeDtypeStruct((batch * t_new, fw), F32),
        scratch_shapes=[pltpu.VMEM((heads * t_new, LANES), F32), pltpu.VMEM((heads * t_new, LANES), F32),
                        pltpu.VMEM((t_new, fw), F32),
                        pltpu.VMEM((heads * t_new, tk), F32), pltpu.VMEM((heads * t_new, tk), BF16)],
        compiler_params=_params("arbitrary", "arbitrary"),
        name="fox_sample",
    )(q, *([kc] * ng), *([vc] * ng), kn, vn, cq, ckc, ckn)


def _retention_kernel(q_ref, k_ref, v_ref, g_ref, cos_ref, sin_ref, gro_ref, lg_ref, s0_ref, *rest,
                      blk, heads, side):
    if side:
        side_in, r_ref, sout_ref, side_out = rest
        side_out[...] = side_in[...].astype(BF16)
    else:
        r_ref, sout_ref = rest
    seq = q_ref.shape[0]
    half = DK_RET // 2
    ri = lax.broadcasted_iota(jnp.int32, (blk, blk), 0)
    ci = lax.broadcasted_iota(jnp.int32, (blk, blk), 1)
    dist = jnp.abs(ri - ci).astype(F32)
    visible = ci // CHUNK <= ri // CHUNK
    idx = lax.broadcasted_iota(jnp.int32, (blk, 1), 0).astype(F32)
    for hh in range(heads):
        lg = lg_ref[hh, :, :1]
        decay = jnp.where(visible, jnp.exp(lg * dist), 0.0)
        cross = jnp.exp(lg * (idx + 1.0))
        k_decay = jnp.exp(lg * (blk - 1.0 - idx))
        s_decay = jnp.exp(lg * blk)
        qk_cols = slice(hh * DK_RET, (hh + 1) * DK_RET)
        v_cols = slice(hh * DV_RET, (hh + 1) * DV_RET)
        gro = gro_ref[:, v_cols]
        s = s0_ref[hh]
        for t in range(seq // blk):
            rows = slice(t * blk, (t + 1) * blk)
            cos = cos_ref[rows, :]
            sin = sin_ref[rows, :]
            q = q_ref[rows, qk_cols]
            k = k_ref[rows, qk_cols]
            q = q * cos + pltpu.roll(q, half, axis=1) * sin
            k = (k * cos + pltpu.roll(k, half, axis=1) * sin) * (DK_RET ** -0.5)
            v = v_ref[rows, v_cols]
            qb = q.astype(BF16)
            a = _qk(qb, k.astype(BF16)) * decay
            o = jnp.dot(a.astype(BF16), v, preferred_element_type=F32)
            o = o + jnp.dot(qb, s.astype(BF16), preferred_element_type=F32) * cross
            s = s_decay * s + lax.dot_general((k * k_decay).astype(BF16), v, (((0,), (0,)), ((), ())),
                                              preferred_element_type=F32)
            ms = jnp.mean(o * o, axis=-1, keepdims=True)
            g = g_ref[rows, v_cols]
            silu = g / (1.0 + jnp.exp(-g))
            r_ref[rows, v_cols] = (o * lax.rsqrt(ms + EPS) * gro * silu).astype(r_ref.dtype)
        sout_ref[hh] = s


def _retention(qkg, qv, cos, sin, g_ret_out, lg, s0, batch, seq, heads, blk, heads_per_step, fox_width,
               cast_w=None):
    m = qkg.shape[0]
    hps = heads_per_step
    groups = heads // hps
    assert fox_width % (hps * DV_RET) == 0, (fox_width, hps)
    v_off = fox_width // (hps * DV_RET)
    grid = (batch, groups)
    row = lambda b, h: (b, h)
    state = pl.BlockSpec((None, hps, DK_RET, DV_RET), lambda b, h: (b, h, 0, 0))
    in_specs = [pl.BlockSpec((seq, hps * DK_RET), row),
                pl.BlockSpec((seq, hps * DK_RET), lambda b, h: (b, groups + h)),
                pl.BlockSpec((seq, hps * DV_RET), lambda b, h: (b, v_off + h)),
                pl.BlockSpec((seq, hps * DV_RET), lambda b, h: (b, groups + h)),
                pl.BlockSpec((seq, DK_RET), lambda b, h: (0, 0)),
                pl.BlockSpec((seq, DK_RET), lambda b, h: (0, 0)),
                pl.BlockSpec((1, hps * DV_RET), lambda b, h: (0, h)),
                pl.BlockSpec((hps, 1, LANES), lambda b, h: (h, 0, 0)),
                state]
    args = [qkg, qkg, qv, qkg, cos, sin, g_ret_out.reshape(1, -1), lg, s0]
    out_specs = [pl.BlockSpec((seq, hps * DV_RET), row), state]
    out_shape = [jax.ShapeDtypeStruct((m, heads * DV_RET), BF16),
                 jax.ShapeDtypeStruct((batch, heads, DK_RET, DV_RET), F32)]
    side = _side_cast_specs(cast_w, grid) if cast_w is not None else None
    if side:
        in_specs.append(side[0])
        args.append(cast_w)
        out_specs.append(side[0])
        out_shape.append(side[1])
    outs = pl.pallas_call(
        functools.partial(_retention_kernel, blk=blk, heads=hps, side=bool(side)),
        grid=grid,
        in_specs=in_specs,
        out_specs=out_specs,
        out_shape=out_shape,
        compiler_params=_params("arbitrary", "arbitrary"),
        name="retention",
    )(*args)
    if cast_w is None:
        return outs[0], outs[1]
    return outs[0], outs[1], (outs[2] if side else cast_w.astype(BF16))


def _rotary_tables(pos):
    half = DK_RET // 2
    inv = ROPE_BASE ** (-jnp.arange(half, dtype=F32) / half)
    ang = pos.astype(F32)[:, None] * inv[None, :]
    cos = jnp.cos(ang)
    sin = jnp.sin(ang)
    return jnp.concatenate([cos, cos], axis=-1), jnp.concatenate([-sin, sin], axis=-1)


def _w_in_split_kernel(w_ref, qv_ref, kf_ref, vf_ref, qkg_ref, lf_ref, *, fw, qkw, vw, h_fox):
    o1, o2, o3 = fw, 2 * fw, 3 * fw
    o4 = o3 + h_fox
    o6 = o4 + 2 * qkw
    o7 = o6 + vw
    qv_ref[:, :fw] = w_ref[:, :o1].astype(BF16)
    qv_ref[:, fw:] = w_ref[:, o6:o7].astype(BF16)
    kf_ref[...] = w_ref[:, o1:o2].astype(BF16)
    vf_ref[...] = w_ref[:, o2:o3].astype(BF16)
    qkg_ref[:, :2 * qkw] = w_ref[:, o4:o6].astype(BF16)
    qkg_ref[:, 2 * qkw:] = w_ref[:, o7:o7 + vw].astype(BF16)
    head = w_ref[:, o3:o3 + LANES]
    lane = lax.broadcasted_iota(jnp.int32, head.shape, 1)
    lf_ref[...] = jnp.where(lane < h_fox, head, 0.0).astype(BF16)


def _in_proj_weights(w_in, b_forget, h_fox, h_ret):
    d, n_in = w_in.shape
    fw = h_fox * HD_FOX
    qkw = h_ret * DK_RET
    vw = h_ret * DV_RET
    tr = _tile(d, 64, 16)
    widths = (fw + vw, fw, fw, 2 * qkw + vw, LANES)
    qv, k_f, v_f, qkg, logf = pl.pallas_call(
        functools.partial(_w_in_split_kernel, fw=fw, qkw=qkw, vw=vw, h_fox=h_fox),
        grid=(d // tr,),
        in_specs=[pl.BlockSpec((tr, n_in), lambda i: (i, 0))],
        out_specs=[pl.BlockSpec((tr, n), lambda i: (i, 0)) for n in widths],
        out_shape=[jax.ShapeDtypeStruct((d, n), BF16) for n in widths],
        compiler_params=_params("arbitrary"),
        name="w_in_split",
    )(w_in)
    b_logf = jnp.pad(b_forget, (0, LANES - h_fox)).reshape(1, LANES).astype(F32)
    return dict(qv=qv, k_f=k_f, v_f=v_f, qkg=qkg, logf=logf, b_logf=b_logf)


def _project(x2d, g_attn, w):
    h = _rmsnorm(x2d, g_attn, BF16)
    qkg, logf = _matmul(h, w["qkg"], F32, logf=(w["logf"], w["b_logf"]))
    return dict(qv=_matmul(h, w["qv"], BF16), k_f=_matmul(h, w["k_f"], F32), v_f=_matmul(h, w["v_f"], F32),
                qkg=qkg, logf=logf)


def _finish_layer(x2d, o_fox, r, g_fox_out, g_mlp, w_out_b, w_up_b, w_down):
    x1, h2 = _out_proj(o_fox, g_fox_out, r, w_out_b, x2d, g_mlp)
    if w_down.dtype == BF16:
        u = _matmul(h2, w_up_b, BF16, epilogue="relu2")
    else:
        u, w_down = _matmul(h2, w_up_b, BF16, epilogue="relu2", cast_w=w_down)
    return _matmul_residual_ksplit(u, w_down, x1), w_down


def _retention_log_gamma(h_ret):
    lg = jnp.log1p(-jnp.exp2(-5.0 - jnp.arange(h_ret, dtype=F32)))
    return jnp.broadcast_to(lg[:, None, None], (h_ret, 1, LANES))


def kernel(x_prompt, x_sample, cache_fox_k, cache_fox_v, cache_fox_logf, state_ret,
           g_attn, w_in, b_forget, g_fox_out, g_ret_out, w_out, g_mlp, w_up, w_down, g_final):
    bp, seq, d = x_prompt.shape
    bs, t_new, _ = x_sample.shape
    depth, _, past, h_fox, _ = cache_fox_k.shape
    h_ret = state_ret.shape[2]
    fw = h_fox * HD_FOX
    dtype = x_prompt.dtype

    lg = _retention_log_gamma(h_ret)
    cos_p, sin_p = _rotary_tables(jnp.arange(seq))
    cos_s, sin_s = _rotary_tables(past + jnp.arange(t_new))
    blk_p = _tile(seq, 256, CHUNK)

    yp = x_prompt.reshape(bp * seq, d)
    ys = x_sample.reshape(bs * t_new, d)
    outs = [[] for _ in range(8)]
    for l in range(depth):
        w = _in_proj_weights(w_in[l], b_forget[l], h_fox, h_ret)

        p = _project(yp, g_attn[l], w)
        logf_p = p["logf"][:, :h_fox]
        c_row = _cumsum_lanes(jnp.swapaxes(logf_p.reshape(bp, seq, h_fox), 1, 2))
        c_col = jnp.swapaxes(c_row, 1, 2).reshape(bp * seq, h_fox)
        o_fox, w_up_b = _fox_prompt(p["qv"], p["k_f"], p["v_f"], c_col, c_row.reshape(bp * h_fox, 1, seq),
                                    bp, seq, h_fox, w_up[l])
        s0 = jnp.zeros((bp, h_ret, DK_RET, DV_RET), F32)
        r, s_fin, w_out_b = _retention(p["qkg"], p["qv"], cos_p, sin_p, g_ret_out[l], lg,
                                       s0, bp, seq, h_ret, blk_p, 1, fw, cast_w=w_out[l])
        yp, w_down_b = _finish_layer(yp, o_fox, r, g_fox_out[l], g_mlp[l], w_out_b, w_up_b, w_down[l])
        outs[0].append(p["k_f"].reshape(bp, seq, h_fox, HD_FOX))
        outs[1].append(p["v_f"].reshape(bp, seq, h_fox, HD_FOX))
        outs[2].append(logf_p.reshape(bp, seq, h_fox).astype(dtype))
        outs[3].append(s_fin.astype(dtype))

        p = _project(ys, g_attn[l], w)
        logf_s = p["logf"][:, :h_fox]
        total = past + t_new
        padded = -(-total // LANES) * LANES
        logf_all = jnp.concatenate(
            [jnp.swapaxes(cache_fox_logf[l].astype(F32), 1, 2),
             jnp.swapaxes(logf_s.reshape(bs, t_new, h_fox), 1, 2),
             jnp.zeros((bs, h_fox, padded - total), F32)], axis=2)
        c_all = _cumsum_lanes(logf_all)
        ckc = c_all[:, :, :past]
        ckn = c_all[:, :, past:total]
        cq = jnp.swapaxes(ckn, 1, 2).reshape(bs * t_new, h_fox)
        o_fox = _fox_sample(p["qv"], cache_fox_k[l], cache_fox_v[l], p["k_f"], p["v_f"],
                            cq, ckc, ckn, bs, t_new, h_fox)
        r, s_new = _retention(p["qkg"], p["qv"], cos_s, sin_s, g_ret_out[l], lg,
                              state_ret[l].astype(F32), bs, t_new, h_ret, t_new, h_ret, fw)
        ys, _ = _finish_layer(ys, o_fox, r, g_fox_out[l], g_mlp[l], w_out_b, w_up_b, w_down_b)
        outs[4].append(p["k_f"].reshape(bs, t_new, h_fox, HD_FOX))
        outs[5].append(p["v_f"].reshape(bs, t_new, h_fox, HD_FOX))
        outs[6].append(logf_s.reshape(bs, t_new, h_fox).astype(dtype))
        outs[7].append(s_new.astype(dtype))

    y_prompt = _rmsnorm(yp, g_final, dtype).reshape(bp, seq, d)
    y_sample = _rmsnorm(ys, g_final, dtype).reshape(bs, t_new, d)
    return (y_prompt, y_sample) + tuple(jnp.stack(o) for o in outs)
```

```python
import functools

import jax
import jax.numpy as jnp
from jax import lax
from jax.experimental import pallas as pl
from jax.experimental.pallas import tpu as pltpu

CHUNK = 64
HD_FOX = 128
DK_RET = 128
DV_RET = 256
ROPE_BASE = 10000.0
EPS = 1e-6

LANES = 128
SUBLANES = 8
VMEM_LIMIT_BYTES = 62 * 1024 * 1024

F32 = jnp.float32
BF16 = jnp.bfloat16


def _params(*sem):
    return pltpu.CompilerParams(dimension_semantics=sem, vmem_limit_bytes=VMEM_LIMIT_BYTES)


def _tile(n, pref, mult=LANES):
    if n <= pref:
        return n
    t = (pref // mult) * mult
    while t > mult and n % t:
        t -= mult
    assert n % t == 0, (n, pref)
    return t


def _rms_kernel(x_ref, g_ref, o_ref):
    x = x_ref[...]
    ms = jnp.mean(x * x, axis=-1, keepdims=True)
    o_ref[...] = (x * lax.rsqrt(ms + EPS) * g_ref[...]).astype(o_ref.dtype)


def _rmsnorm(x, g, out_dtype):
    m, d = x.shape
    tr = _tile(m, 256, 8)
    return pl.pallas_call(
        _rms_kernel,
        grid=(m // tr,),
        in_specs=[pl.BlockSpec((tr, d), lambda i: (i, 0)),
                  pl.BlockSpec((1, d), lambda i: (0, 0))],
        out_specs=pl.BlockSpec((tr, d), lambda i: (i, 0)),
        out_shape=jax.ShapeDtypeStruct((m, d), out_dtype),
        compiler_params=_params("arbitrary"),
        name="rmsnorm",
    )(x, g.reshape(1, d))


def _log_sigmoid(z):
    return jnp.minimum(z, 0.0) - jnp.log(1.0 + jnp.exp(-jnp.abs(z)))


def _side_cast_specs(w, grid):
    steps = 1
    for g in grid:
        steps *= g
    rows, cols = w.shape
    if rows % steps or (rows // steps) % 16:
        return None

    def index_map(*idx):
        lin = idx[0]
        for g, i in zip(grid[1:], idx[1:]):
            lin = lin * g + i
        return (lin, 0)

    spec = pl.BlockSpec((rows // steps, cols), index_map)
    return spec, jax.ShapeDtypeStruct((rows, cols), BF16)


def _mm_kernel(a_ref, w_ref, *rest, epilogue, side, logf):
    rest = list(rest)
    side_out = rest.pop() if side else None
    lf_out = rest.pop() if logf else None
    o_ref = rest.pop()
    if side:
        side_out[...] = rest.pop()[...].astype(BF16)
    a = a_ref[...]
    if logf:
        wlf_ref, b_ref = rest

        @pl.when(pl.program_id(1) == 0)
        def _():
            lf_out[...] = _log_sigmoid(jnp.dot(a, wlf_ref[...], preferred_element_type=F32) + b_ref[...])

    acc = jnp.dot(a, w_ref[...], preferred_element_type=F32)
    if epilogue == "relu2":
        u = jnp.maximum(acc, 0.0)
        o_ref[...] = (u * u).astype(o_ref.dtype)
    else:
        o_ref[...] = acc.astype(o_ref.dtype)


def _matmul(a, w, out_dtype, epilogue="cast", tm_pref=1024, tn_pref=1024, cast_w=None, logf=None):
    m, k = a.shape
    n = w.shape[1]
    tm = _tile(m, tm_pref, 16)
    tn = _tile(n, tn_pref)
    grid = (m // tm, n // tn)
    in_specs = [pl.BlockSpec((tm, k), lambda i, j: (i, 0)),
                pl.BlockSpec((k, tn), lambda i, j: (0, j))]
    args = [a, w]
    out_specs = [pl.BlockSpec((tm, tn), lambda i, j: (i, j))]
    out_shape = [jax.ShapeDtypeStruct((m, n), out_dtype)]
    if logf is not None:
        in_specs += [pl.BlockSpec((k, LANES), lambda i, j: (0, 0)), pl.BlockSpec((1, LANES), lambda i, j: (0, 0))]
        args += list(logf)
        out_specs.append(pl.BlockSpec((tm, LANES), lambda i, j: (i, 0)))
        out_shape.append(jax.ShapeDtypeStruct((m, LANES), F32))
    side = _side_cast_specs(cast_w, grid) if cast_w is not None else None
    if side:
        in_specs.append(side[0])
        args.append(cast_w)
        out_specs.append(side[0])
        out_shape.append(side[1])
    outs = list(pl.pallas_call(
        functools.partial(_mm_kernel, epilogue=epilogue, side=bool(side), logf=logf is not None),
        grid=grid,
        in_specs=in_specs,
        out_specs=out_specs,
        out_shape=out_shape,
        compiler_params=_params("arbitrary", "arbitrary"),
        name="matmul_" + epilogue,
    )(*args))
    if cast_w is not None and not side:
        outs.append(cast_w.astype(BF16))
    return outs[0] if len(outs) == 1 else tuple(outs)


def _mix_mm_kernel(o_ref, gf_ref, r_ref, w_ref, res_ref, gn_ref, out_ref, h_ref, a_scr, x_scr, *, fw):
    j = pl.program_id(1)
    nj, _, tn = x_scr.shape

    @pl.when(j == 0)
    def _():
        o = o_ref[...]
        ms = jnp.mean(o * o, axis=-1, keepdims=True)
        a_scr[:, :fw] = (o * lax.rsqrt(ms + EPS) * gf_ref[...]).astype(BF16)
        a_scr[:, fw:] = r_ref[...]

    x = res_ref[...] + jnp.dot(a_scr[...], w_ref[...], preferred_element_type=F32)
    out_ref[...] = x
    x_scr[j] = x

    @pl.when(j == nj - 1)
    def _():
        sq = x_scr[0] * x_scr[0]
        ssum = jnp.sum(sq, axis=-1, keepdims=True)
        for t in range(1, nj):
            xt = x_scr[t]
            ssum = ssum + jnp.sum(xt * xt, axis=-1, keepdims=True)
        rstd = lax.rsqrt(ssum / (nj * tn) + EPS)
        for t in range(nj):
            cols = slice(t * tn, (t + 1) * tn)
            h_ref[:, cols] = (x_scr[t] * rstd * gn_ref[:, cols]).astype(h_ref.dtype)


def _out_proj(o_fox, g_fox, r, w, res, g_next):
    m, fw = o_fox.shape
    rw = r.shape[1]
    n = w.shape[1]
    tm = _tile(m, 512, 16)
    tn = _tile(n, 1024)
    return pl.pallas_call(
        functools.partial(_mix_mm_kernel, fw=fw),
        grid=(m // tm, n // tn),
        in_specs=[pl.BlockSpec((tm, fw), lambda i, j: (i, 0)),
                  pl.BlockSpec((1, fw), lambda i, j: (0, 0)),
                  pl.BlockSpec((tm, rw), lambda i, j: (i, 0)),
                  pl.BlockSpec((fw + rw, tn), lambda i, j: (0, j)),
                  pl.BlockSpec((tm, tn), lambda i, j: (i, j)),
                  pl.BlockSpec((1, n), lambda i, j: (0, 0))],
        out_specs=[pl.BlockSpec((tm, tn), lambda i, j: (i, j)),
                   pl.BlockSpec((tm, n), lambda i, j: (i, 0))],
        out_shape=[jax.ShapeDtypeStruct((m, n), F32), jax.ShapeDtypeStruct((m, n), BF16)],
        scratch_shapes=[pltpu.VMEM((tm, fw + rw), BF16), pltpu.VMEM((n // tn, tm, tn), F32)],
        compiler_params=_params("arbitrary", "arbitrary"),
        name="out_proj",
    )(o_fox, g_fox.reshape(1, fw), r, w, res, g_next.reshape(1, n))


def _mm_acc_kernel(a_ref, w_ref, res_ref, out_ref):
    @pl.when(pl.program_id(2) == 0)
    def _():
        out_ref[...] = res_ref[...]

    out_ref[...] += jnp.dot(a_ref[...], w_ref[...], preferred_element_type=F32)


def _matmul_residual_ksplit(a, w, res):
    m, k = a.shape
    n = w.shape[1]
    tm = _tile(m, 1024, 16)
    tn = _tile(n, 1024)
    tk = _tile(k, 4096)
    return pl.pallas_call(
        _mm_acc_kernel,
        grid=(m // tm, n // tn, k // tk),
        in_specs=[pl.BlockSpec((tm, tk), lambda i, j, kk: (i, kk)),
                  pl.BlockSpec((tk, tn), lambda i, j, kk: (kk, j)),
                  pl.BlockSpec((tm, tn), lambda i, j, kk: (i, j))],
        out_specs=pl.BlockSpec((tm, tn), lambda i, j, kk: (i, j)),
        out_shape=jax.ShapeDtypeStruct((m, n), F32),
        compiler_params=_params("arbitrary", "arbitrary", "arbitrary"),
        name="down_proj",
    )(a, w, res)


def _cumsum_kernel(x_ref, o_ref):
    x = x_ref[...]
    t = x.shape[1]
    lane = lax.broadcasted_iota(jnp.int32, x.shape, 1)
    d = 1
    while d < t:
        x = x + jnp.where(lane >= d, pltpu.roll(x, d, axis=1), 0.0)
        d *= 2
    o_ref[...] = x


def _cumsum_lanes(x):
    b, h, t = x.shape
    return pl.pallas_call(
        _cumsum_kernel,
        grid=(b,),
        in_specs=[pl.BlockSpec((None, h, t), lambda i: (i, 0, 0))],
        out_specs=pl.BlockSpec((None, h, t), lambda i: (i, 0, 0)),
        out_shape=jax.ShapeDtypeStruct((b, h, t), F32),
        compiler_params=_params("arbitrary"),
        name="cumsum_logf",
    )(x)


def _qk(q_bf16, k_bf16):
    return lax.dot_general(q_bf16, k_bf16, (((1,), (1,)), ((), ())), preferred_element_type=F32)


def _head_column(c_ref, h):
    c_all = c_ref[...]
    head = lax.broadcasted_iota(jnp.int32, c_all.shape, 1)
    return jnp.sum(jnp.where(head == h, c_all, 0.0), axis=1, keepdims=True)


def _fox_prompt_kernel(q_ref, k_ref, v_ref, cq_ref, ck_ref, *rest, tq, side):
    if side:
        side_in, o_ref, side_out, kt_scr, vb_scr = rest
        side_out[...] = side_in[...].astype(BF16)
    else:
        o_ref, kt_scr, vb_scr = rest
    h = pl.program_id(1)
    seq = q_ref.shape[0]
    kt_scr[...] = k_ref[...].T.astype(BF16)
    vb_scr[...] = v_ref[...].astype(BF16)
    cq = _head_column(cq_ref, h)
    ck = ck_ref[...]
    scale = HD_FOX ** -0.5
    row = lax.broadcasted_iota(jnp.int32, (tq, tq), 0)
    col = lax.broadcasted_iota(jnp.int32, (tq, tq), 1)
    causal = col <= row
    for qi in range(seq // tq):
        lo, hi = qi * tq, (qi + 1) * tq
        q = q_ref[lo:hi, :]
        cq_b = cq[lo:hi]
        s_d = jnp.dot(q, kt_scr[:, lo:hi], preferred_element_type=F32) * scale + cq_b - ck[:, lo:hi]
        s_d = jnp.where(causal, s_d, -jnp.inf)
        m = jnp.max(s_d, axis=1, keepdims=True)
        if qi > 0:
            s_o = jnp.dot(q, kt_scr[:, :lo], preferred_element_type=F32) * scale + cq_b - ck[:, :lo]
            m = jnp.maximum(m, jnp.max(s_o, axis=1, keepdims=True))
            p_o = jnp.exp(s_o - m)
            l = jnp.sum(p_o, axis=1, keepdims=True)
            acc = jnp.dot(p_o.astype(BF16), vb_scr[:lo, :], preferred_element_type=F32)
        p_d = jnp.exp(s_d - m)
        l_d = jnp.sum(p_d, axis=1, keepdims=True)
        acc_d = jnp.dot(p_d.astype(BF16), vb_scr[lo:hi, :], preferred_element_type=F32)
        if qi > 0:
            l_d = l_d + l
            acc_d = acc_d + acc
        o_ref[lo:hi, :] = acc_d / l_d


def _fox_prompt(q, k, v, c_col, c_row, batch, seq, heads, cast_w):
    m, fw = k.shape
    tq = _tile(seq, 256)
    grid = (batch, heads)
    head_blk = lambda b, h: (b, h)
    in_specs = [pl.BlockSpec((seq, HD_FOX), head_blk),
                pl.BlockSpec((seq, HD_FOX), head_blk),
                pl.BlockSpec((seq, HD_FOX), head_blk),
                pl.BlockSpec((seq, heads), lambda b, h: (b, 0)),
                pl.BlockSpec((None, 1, seq), lambda b, h: (b * heads + h, 0, 0))]
    args = [q, k, v, c_col, c_row]
    out_specs = [pl.BlockSpec((seq, HD_FOX), head_blk)]
    out_shape = [jax.ShapeDtypeStruct((m, fw), F32)]
    side = _side_cast_specs(cast_w, grid)
    if side:
        in_specs.append(side[0])
        args.append(cast_w)
        out_specs.append(side[0])
        out_shape.append(side[1])
    outs = pl.pallas_call(
        functools.partial(_fox_prompt_kernel, tq=tq, side=bool(side)),
        grid=grid,
        in_specs=in_specs,
        out_specs=out_specs,
        out_shape=out_shape,
        scratch_shapes=[pltpu.VMEM((HD_FOX, seq), BF16), pltpu.VMEM((seq, HD_FOX), BF16)],
        compiler_params=_params("arbitrary", "arbitrary"),
        name="fox_prompt",
    )(*args)
    return outs[0], (outs[1] if side else cast_w.astype(BF16))


def _every_eighth_row(ref, j):
    n, s, d = ref.shape
    return ref.reshape(n * s, d)[pl.ds(j, n, stride=s), :]


def _fox_sample_kernel(*refs, heads):
    ng = heads // SUBLANES
    q_ref = refs[0]
    k_refs = refs[1:1 + ng]
    v_refs = refs[1 + ng:1 + 2 * ng]
    kn_ref, vn_ref, cq_ref, ckc_ref, ckn_ref, o_ref, m_scr, l_scr, acc_scr, s_scr, p_scr = refs[1 + 2 * ng:]
    ki = pl.program_id(1)
    last = pl.num_programs(1) - 1
    scale = HD_FOX ** -0.5
    t = q_ref.shape[0]
    head_cols = lambda h: slice(h * HD_FOX, (h + 1) * HD_FOX)
    head_rows = lambda h: slice(h * t, (h + 1) * t)

    @pl.when(ki == 0)
    def _():
        m_scr[...] = jnp.full(m_scr.shape, -jnp.inf, F32)
        l_scr[...] = jnp.zeros(l_scr.shape, F32)
        acc_scr[...] = jnp.zeros(acc_scr.shape, F32)

    def cached(group_refs, h):
        return _every_eighth_row(group_refs[h // SUBLANES], h % SUBLANES).astype(BF16)

    def update(load_k, load_v, ck_ref, n, causal):
        for h in range(heads):
            s = _qk(q_ref[:, head_cols(h)], load_k(h)) * scale + cq_ref[:, h:h + 1] - ck_ref[h:h + 1, :]
            if causal:
                row = lax.broadcasted_iota(jnp.int32, s.shape, 0)
                col = lax.broadcasted_iota(jnp.int32, s.shape, 1)
                s = jnp.where(col <= row, s, -jnp.inf)
            s_scr[head_rows(h), :n] = s
        s = s_scr[:, :n]
        m_prev = m_scr[:, :1]
        m_new = jnp.maximum(m_prev, jnp.max(s, axis=1, keepdims=True))
        alpha = jnp.exp(m_prev - m_new)
        p = jnp.exp(s - m_new)
        l_new = alpha * l_scr[:, :1] + jnp.sum(p, axis=1, keepdims=True)
        m_scr[...] = jnp.broadcast_to(m_new, m_scr.shape)
        l_scr[...] = jnp.broadcast_to(l_new, l_scr.shape)
        p_scr[:, :n] = p.astype(BF16)
        for h in range(heads):
            pv = jnp.dot(p_scr[head_rows(h), :n], load_v(h), preferred_element_type=F32)
            acc_scr[:, head_cols(h)] = alpha[head_rows(h)] * acc_scr[:, head_cols(h)] + pv

    @pl.when(ki < last)
    def _():
        update(functools.partial(cached, k_refs), functools.partial(cached, v_refs), ckc_ref,
               s_scr.shape[1], causal=False)

    @pl.when(ki == last)
    def _():
        new_rows = lambda ref, h: ref[:, head_cols(h)].astype(BF16)
        update(functools.partial(new_rows, kn_ref), functools.partial(new_rows, vn_ref), ckn_ref, t, causal=True)
        for h in range(heads):
            o_ref[:, head_cols(h)] = acc_scr[:, head_cols(h)] / l_scr[head_rows(h), :1]


def _fox_sample(q, kc, vc, kn, vn, cq, ckc, ckn, batch, t_new, heads):
    past = kc.shape[1]
    fw = heads * HD_FOX
    assert heads % SUBLANES == 0, heads
    ng = heads // SUBLANES
    tk = _tile(past, 1024)
    nkc = past // tk
    kc = kc.reshape(batch, past, ng, SUBLANES, HD_FOX)
    vc = vc.reshape(batch, past, ng, SUBLANES, HD_FOX)
    row_blk = pl.BlockSpec((t_new, fw), lambda b, ki: (b, 0))
    group_blk = [pl.BlockSpec((None, tk, None, SUBLANES, HD_FOX),
                              functools.partial(lambda b, ki, g: (b, jnp.minimum(ki, nkc - 1), g, 0, 0), g=g))
                 for g in range(ng)]
    return pl.pallas_call(
        functools.partial(_fox_sample_kernel, heads=heads),
        grid=(batch, nkc + 1),
        in_specs=[row_blk] + group_blk + group_blk + [
            row_blk, row_blk,
            pl.BlockSpec((t_new, heads), lambda b, ki: (b, 0)),
            pl.BlockSpec((None, heads, tk), lambda b, ki: (b, 0, jnp.minimum(ki, nkc - 1))),
            pl.BlockSpec((None, heads, t_new), lambda b, ki: (b, 0, 0))],
        out_specs=row_blk,
        out_shape=jax.ShapeDtypeStruct((batch * t_new, fw), F32),
        scratch_shapes=[pltpu.VMEM((heads * t_new, LANES), F32), pltpu.VMEM((heads * t_new, LANES), F32),
                        pltpu.VMEM((t_new, fw), F32),
                        pltpu.VMEM((heads * t_new, tk), F32), pltpu.VMEM((heads * t_new, tk), BF16)],
        compiler_params=_params("arbitrary", "arbitrary"),
        name="fox_sample",
    )(q, *([kc] * ng), *([vc] * ng), kn, vn, cq, ckc, ckn)


def _retention_kernel(q_ref, k_ref, v_ref, g_ref, cos_ref, sin_ref, gro_ref, lg_ref, s0_ref, *rest,
                      blk, heads, side):
    if side:
        side_in, r_ref, sout_ref, side_out = rest
        side_out[...] = side_in[...].astype(BF16)
    else:
        r_ref, sout_ref = rest
    seq = q_ref.shape[0]
    half = DK_RET // 2
    ri = lax.broadcasted_iota(jnp.int32, (blk, blk), 0)
    ci = lax.broadcasted_iota(jnp.int32, (blk, blk), 1)
    dist = jnp.abs(ri - ci).astype(F32)
    visible = ci // CHUNK <= ri // CHUNK
    idx = lax.broadcasted_iota(jnp.int32, (blk, 1), 0).astype(F32)
    for hh in range(heads):
        lg = lg_ref[hh, :, :1]
        decay = jnp.where(visible, jnp.exp(lg * dist), 0.0)
        cross = jnp.exp(lg * (idx + 1.0))
        k_decay = jnp.exp(lg * (blk - 1.0 - idx))
        s_decay = jnp.exp(lg * blk)
        qk_cols = slice(hh * DK_RET, (hh + 1) * DK_RET)
        v_cols = slice(hh * DV_RET, (hh + 1) * DV_RET)
        gro = gro_ref[:, v_cols]
        s = s0_ref[hh]
        for t in range(seq // blk):
            rows = slice(t * blk, (t + 1) * blk)
            cos = cos_ref[rows, :]
            sin = sin_ref[rows, :]
            q = q_ref[rows, qk_cols]
            k = k_ref[rows, qk_cols]
            q = q * cos + pltpu.roll(q, half, axis=1) * sin
            k = (k * cos + pltpu.roll(k, half, axis=1) * sin) * (DK_RET ** -0.5)
            v = v_ref[rows, v_cols]
            qb = q.astype(BF16)
            a = _qk(qb, k.astype(BF16)) * decay
            o = jnp.dot(a.astype(BF16), v, preferred_element_type=F32)
            o = o + jnp.dot(qb, s.astype(BF16), preferred_element_type=F32) * cross
            s = s_decay * s + lax.dot_general((k * k_decay).astype(BF16), v, (((0,), (0,)), ((), ())),
                                              preferred_element_type=F32)
            ms = jnp.mean(o * o, axis=-1, keepdims=True)
            g = g_ref[rows, v_cols]
            silu = g / (1.0 + jnp.exp(-g))
            r_ref[rows, v_cols] = (o * lax.rsqrt(ms + EPS) * gro * silu).astype(r_ref.dtype)
        sout_ref[hh] = s


def _retention(qkg, qv, cos, sin, g_ret_out, lg, s0, batch, seq, heads, blk, heads_per_step, fox_width,
               cast_w=None):
    m = qkg.shape[0]
    hps = heads_per_step
    groups = heads // hps
    assert fox_width % (hps * DV_RET) == 0, (fox_width, hps)
    v_off = fox_width // (hps * DV_RET)
    grid = (batch, groups)
    row = lambda b, h: (b, h)
    state = pl.BlockSpec((None, hps, DK_RET, DV_RET), lambda b, h: (b, h, 0, 0))
    in_specs = [pl.BlockSpec((seq, hps * DK_RET), row),
                pl.BlockSpec((seq, hps * DK_RET), lambda b, h: (b, groups + h)),
                pl.BlockSpec((seq, hps * DV_RET), lambda b, h: (b, v_off + h)),
                pl.BlockSpec((seq, hps * DV_RET), lambda b, h: (b, groups + h)),
                pl.BlockSpec((seq, DK_RET), lambda b, h: (0, 0)),
                pl.BlockSpec((seq, DK_RET), lambda b, h: (0, 0)),
                pl.BlockSpec((1, hps * DV_RET), lambda b, h: (0, h)),
                pl.BlockSpec((hps, 1, LANES), lambda b, h: (h, 0, 0)),
                state]
    args = [qkg, qkg, qv, qkg, cos, sin, g_ret_out.reshape(1, -1), lg, s0]
    out_specs = [pl.BlockSpec((seq, hps * DV_RET), row), state]
    out_shape = [jax.ShapeDtypeStruct((m, heads * DV_RET), BF16),
                 jax.ShapeDtypeStruct((batch, heads, DK_RET, DV_RET), F32)]
    side = _side_cast_specs(cast_w, grid) if cast_w is not None else None
    if side:
        in_specs.append(side[0])
        args.append(cast_w)
        out_specs.append(side[0])
        out_shape.append(side[1])
    outs = pl.pallas_call(
        functools.partial(_retention_kernel, blk=blk, heads=hps, side=bool(side)),
        grid=grid,
        in_specs=in_specs,
        out_specs=out_specs,
        out_shape=out_shape,
        compiler_params=_params("arbitrary", "arbitrary"),
        name="retention",
    )(*args)
    if cast_w is None:
        return outs[0], outs[1]
    return outs[0], outs[1], (outs[2] if side else cast_w.astype(BF16))


def _rotary_tables(pos):
    half = DK_RET // 2
    inv = ROPE_BASE ** (-jnp.arange(half, dtype=F32) / half)
    ang = pos.astype(F32)[:, None] * inv[None, :]
    cos = jnp.cos(ang)
    sin = jnp.sin(ang)
    return jnp.concatenate([cos, cos], axis=-1), jnp.concatenate([-sin, sin], axis=-1)


def _feature_rows_kernel(w_ref, o_ref, *, valid):
    x = w_ref[...]
    if valid < x.shape[0]:
        row = lax.broadcasted_iota(jnp.int32, x.shape, 0)
        x = jnp.where(row < valid, x, 0.0)
    o_ref[...] = x.T.astype(BF16)


def _feature_rows(w_t, layer, segments, rows_per_step, valid=None):
    _, _, d = w_t.shape
    rb = rows_per_step
    starts = [s + k * rb for s, n in segments for k in range(n // rb)]
    assert all(n % rb == 0 for _, n in segments) and all(s % SUBLANES == 0 for s in starts), segments

    def index_map(i):
        tile_row = starts[-1] // SUBLANES
        for k in range(len(starts) - 2, -1, -1):
            tile_row = jnp.where(i <= k, starts[k] // SUBLANES, tile_row)
        return (layer, tile_row * SUBLANES, 0)

    return pl.pallas_call(
        functools.partial(_feature_rows_kernel, valid=rb if valid is None else valid),
        grid=(len(starts),),
        in_specs=[pl.BlockSpec((None, pl.Element(rb), pl.Element(d)), index_map)],
        out_specs=pl.BlockSpec((d, rb), lambda i: (0, i)),
        out_shape=jax.ShapeDtypeStruct((d, rb * len(starts)), BF16),
        compiler_params=_params("arbitrary"),
        name="w_in_rows",
    )(w_t)


def _in_proj_weights(w_in, layer, b_forget, h_fox, h_ret):
    fw = h_fox * HD_FOX
    qkw = h_ret * DK_RET
    vw = h_ret * DV_RET
    o1, o2, o3 = fw, 2 * fw, 3 * fw
    o4 = o3 + h_fox
    o6 = o4 + 2 * qkw
    o7 = o6 + vw
    w_t = jnp.swapaxes(w_in, 1, 2)
    rb = 256
    return dict(
        qv=_feature_rows(w_t, layer, [(0, fw), (o6, vw)], rb),
        k_f=_feature_rows(w_t, layer, [(o1, fw)], rb),
        v_f=_feature_rows(w_t, layer, [(o2, fw)], rb),
        qkg=_feature_rows(w_t, layer, [(o4, 2 * qkw), (o7, vw)], rb),
        logf=_feature_rows(w_t, layer, [(o3, LANES)], LANES, valid=h_fox),
        b_logf=jnp.pad(b_forget, (0, LANES - h_fox)).reshape(1, LANES).astype(F32))


def _project(x2d, g_attn, w):
    h = _rmsnorm(x2d, g_attn, BF16)
    qkg, logf = _matmul(h, w["qkg"], F32, logf=(w["logf"], w["b_logf"]))
    return dict(qv=_matmul(h, w["qv"], BF16), k_f=_matmul(h, w["k_f"], F32), v_f=_matmul(h, w["v_f"], F32),
                qkg=qkg, logf=logf)


def _finish_layer(x2d, o_fox, r, g_fox_out, g_mlp, w_out_b, w_up_b, w_down):
    x1, h2 = _out_proj(o_fox, g_fox_out, r, w_out_b, x2d, g_mlp)
    if w_down.dtype == BF16:
        u = _matmul(h2, w_up_b, BF16, epilogue="relu2")
    else:
        u, w_down = _matmul(h2, w_up_b, BF16, epilogue="relu2", cast_w=w_down)
    return _matmul_residual_ksplit(u, w_down, x1), w_down


def _retention_log_gamma(h_ret):
    lg = jnp.log1p(-jnp.exp2(-5.0 - jnp.arange(h_ret, dtype=F32)))
    return jnp.broadcast_to(lg[:, None, None], (h_ret, 1, LANES))


def kernel(x_prompt, x_sample, cache_fox_k, cache_fox_v, cache_fox_logf, state_ret,
           g_attn, w_in, b_forget, g_fox_out, g_ret_out, w_out, g_mlp, w_up, w_down, g_final):
    bp, seq, d = x_prompt.shape
    bs, t_new, _ = x_sample.shape
    depth, _, past, h_fox, _ = cache_fox_k.shape
    h_ret = state_ret.shape[2]
    fw = h_fox * HD_FOX
    dtype = x_prompt.dtype

    lg = _retention_log_gamma(h_ret)
    cos_p, sin_p = _rotary_tables(jnp.arange(seq))
    cos_s, sin_s = _rotary_tables(past + jnp.arange(t_new))
    blk_p = _tile(seq, 256, CHUNK)

    yp = x_prompt.reshape(bp * seq, d)
    ys = x_sample.reshape(bs * t_new, d)
    outs = [[] for _ in range(8)]
    for l in range(depth):
        w = _in_proj_weights(w_in, l, b_forget[l], h_fox, h_ret)

        p = _project(yp, g_attn[l], w)
        logf_p = p["logf"][:, :h_fox]
        c_row = _cumsum_lanes(jnp.swapaxes(logf_p.reshape(bp, seq, h_fox), 1, 2))
        c_col = jnp.swapaxes(c_row, 1, 2).reshape(bp * seq, h_fox)
        o_fox, w_up_b = _fox_prompt(p["qv"], p["k_f"], p["v_f"], c_col, c_row.reshape(bp * h_fox, 1, seq),
                                    bp, seq, h_fox, w_up[l])
        s0 = jnp.zeros((bp, h_ret, DK_RET, DV_RET), F32)
        r, s_fin, w_out_b = _retention(p["qkg"], p["qv"], cos_p, sin_p, g_ret_out[l], lg,
                                       s0, bp, seq, h_ret, blk_p, 1, fw, cast_w=w_out[l])
        yp, w_down_b = _finish_layer(yp, o_fox, r, g_fox_out[l], g_mlp[l], w_out_b, w_up_b, w_down[l])
        outs[0].append(p["k_f"].reshape(bp, seq, h_fox, HD_FOX))
        outs[1].append(p["v_f"].reshape(bp, seq, h_fox, HD_FOX))
        outs[2].append(logf_p.reshape(bp, seq, h_fox).astype(dtype))
        outs[3].append(s_fin.astype(dtype))

        p = _project(ys, g_attn[l], w)
        logf_s = p["logf"][:, :h_fox]
        total = past + t_new
        padded = -(-total // LANES) * LANES
        logf_all = jnp.concatenate(
            [jnp.swapaxes(cache_fox_logf[l].astype(F32), 1, 2),
             jnp.swapaxes(logf_s.reshape(bs, t_new, h_fox), 1, 2),
             jnp.zeros((bs, h_fox, padded - total), F32)], axis=2)
        c_all = _cumsum_lanes(logf_all)
        ckc = c_all[:, :, :past]
        ckn = c_all[:, :, past:total]
        cq = jnp.swapaxes(ckn, 1, 2).reshape(bs * t_new, h_fox)
        o_fox = _fox_sample(p["qv"], cache_fox_k[l], cache_fox_v[l], p["k_f"], p["v_f"],
                            cq, ckc, ckn, bs, t_new, h_fox)
        r, s_new = _retention(p["qkg"], p["qv"], cos_s, sin_s, g_ret_out[l], lg,
                              state_ret[l].astype(F32), bs, t_new, h_ret, t_new, h_ret, fw)
        ys, _ = _finish_layer(ys, o_fox, r, g_fox_out[l], g_mlp[l], w_out_b, w_up_b, w_down_b)
        outs[4].append(p["k_f"].reshape(bs, t_new, h_fox, HD_FOX))
        outs[5].append(p["v_f"].reshape(bs, t_new, h_fox, HD_FOX))
        outs[6].append(logf_s.reshape(bs, t_new, h_fox).astype(dtype))
        outs[7].append(s_new.astype(dtype))

    y_prompt = _rmsnorm(yp, g_final, dtype).reshape(bp, seq, d)
    y_sample = _rmsnorm(ys, g_final, dtype).reshape(bs, t_new, d)
    return (y_prompt, y_sample) + tuple(jnp.stack(o) for o in outs)
```

```python
import functools

import jax
import jax.numpy as jnp
from jax import lax
from jax.experimental import pallas as pl
from jax.experimental.pallas import tpu as pltpu

CHUNK = 64
HD_FOX = 128
DK_RET = 128
DV_RET = 256
ROPE_BASE = 10000.0
EPS = 1e-6

LANES = 128
SUBLANES = 8
VMEM_LIMIT_BYTES = 62 * 1024 * 1024

F32 = jnp.float32
BF16 = jnp.bfloat16


def _params(*sem):
    return pltpu.CompilerParams(dimension_semantics=sem, vmem_limit_bytes=VMEM_LIMIT_BYTES)


def _tile(n, pref, mult=LANES):
    if n <= pref:
        return n
    t = (pref // mult) * mult
    while t > mult and n % t:
        t -= mult
    assert n % t == 0, (n, pref)
    return t


def _rms_kernel(x_ref, g_ref, o_ref):
    x = x_ref[...]
    ms = jnp.mean(x * x, axis=-1, keepdims=True)
    o_ref[...] = (x * lax.rsqrt(ms + EPS) * g_ref[...]).astype(o_ref.dtype)


def _rmsnorm(x, g, out_dtype):
    m, d = x.shape
    tr = _tile(m, 256, 8)
    return pl.pallas_call(
        _rms_kernel,
        grid=(m // tr,),
        in_specs=[pl.BlockSpec((tr, d), lambda i: (i, 0)),
                  pl.BlockSpec((1, d), lambda i: (0, 0))],
        out_specs=pl.BlockSpec((tr, d), lambda i: (i, 0)),
        out_shape=jax.ShapeDtypeStruct((m, d), out_dtype),
        compiler_params=_params("arbitrary"),
        name="rmsnorm",
    )(x, g.reshape(1, d))


def _log_sigmoid(z):
    return jnp.minimum(z, 0.0) - jnp.log(1.0 + jnp.exp(-jnp.abs(z)))


def _side_cast_specs(w, grid):
    steps = 1
    for g in grid:
        steps *= g
    rows, cols = w.shape
    if rows % steps or (rows // steps) % 16:
        return None

    def index_map(*idx):
        lin = idx[0]
        for g, i in zip(grid[1:], idx[1:]):
            lin = lin * g + i
        return (lin, 0)

    spec = pl.BlockSpec((rows // steps, cols), index_map)
    return spec, jax.ShapeDtypeStruct((rows, cols), BF16)


def _mm_kernel(a_ref, w_ref, *rest, epilogue, side, logf):
    rest = list(rest)
    side_out = rest.pop() if side else None
    lf_out = rest.pop() if logf else None
    o_ref = rest.pop()
    if side:
        side_out[...] = rest.pop()[...].astype(BF16)
    a = a_ref[...]
    if logf:
        wlf_ref, b_ref = rest

        @pl.when(pl.program_id(1) == 0)
        def _():
            lf_out[...] = _log_sigmoid(jnp.dot(a, wlf_ref[...], preferred_element_type=F32) + b_ref[...])

    acc = jnp.dot(a, w_ref[...], preferred_element_type=F32)
    if epilogue == "relu2":
        u = jnp.maximum(acc, 0.0)
        o_ref[...] = (u * u).astype(o_ref.dtype)
    else:
        o_ref[...] = acc.astype(o_ref.dtype)


def _matmul(a, w, out_dtype, epilogue="cast", tm_pref=1024, tn_pref=1024, cast_w=None, logf=None):
    m, k = a.shape
    n = w.shape[1]
    tm = _tile(m, tm_pref, 16)
    tn = _tile(n, tn_pref)
    grid = (m // tm, n // tn)
    in_specs = [pl.BlockSpec((tm, k), lambda i, j: (i, 0)),
                pl.BlockSpec((k, tn), lambda i, j: (0, j))]
    args = [a, w]
    out_specs = [pl.BlockSpec((tm, tn), lambda i, j: (i, j))]
    out_shape = [jax.ShapeDtypeStruct((m, n), out_dtype)]
    if logf is not None:
        in_specs += [pl.BlockSpec((k, LANES), lambda i, j: (0, 0)), pl.BlockSpec((1, LANES), lambda i, j: (0, 0))]
        args += list(logf)
        out_specs.append(pl.BlockSpec((tm, LANES), lambda i, j: (i, 0)))
        out_shape.append(jax.ShapeDtypeStruct((m, LANES), F32))
    side = _side_cast_specs(cast_w, grid) if cast_w is not None else None
    if side:
        in_specs.append(side[0])
        args.append(cast_w)
        out_specs.append(side[0])
        out_shape.append(side[1])
    outs = list(pl.pallas_call(
        functools.partial(_mm_kernel, epilogue=epilogue, side=bool(side), logf=logf is not None),
        grid=grid,
        in_specs=in_specs,
        out_specs=out_specs,
        out_shape=out_shape,
        compiler_params=_params("arbitrary", "arbitrary"),
        name="matmul_" + epilogue,
    )(*args))
    if cast_w is not None and not side:
        outs.append(cast_w.astype(BF16))
    return outs[0] if len(outs) == 1 else tuple(outs)


def _mix_mm_kernel(o_ref, gf_ref, r_ref, w_ref, res_ref, gn_ref, out_ref, h_ref, a_scr, x_scr, *, fw):
    j = pl.program_id(1)
    nj, _, tn = x_scr.shape

    @pl.when(j == 0)
    def _():
        o = o_ref[...]
        ms = jnp.mean(o * o, axis=-1, keepdims=True)
        a_scr[:, :fw] = (o * lax.rsqrt(ms + EPS) * gf_ref[...]).astype(BF16)
        a_scr[:, fw:] = r_ref[...]

    x = res_ref[...] + jnp.dot(a_scr[...], w_ref[...], preferred_element_type=F32)
    out_ref[...] = x
    x_scr[j] = x

    @pl.when(j == nj - 1)
    def _():
        sq = x_scr[0] * x_scr[0]
        ssum = jnp.sum(sq, axis=-1, keepdims=True)
        for t in range(1, nj):
            xt = x_scr[t]
            ssum = ssum + jnp.sum(xt * xt, axis=-1, keepdims=True)
        rstd = lax.rsqrt(ssum / (nj * tn) + EPS)
        for t in range(nj):
            cols = slice(t * tn, (t + 1) * tn)
            h_ref[:, cols] = (x_scr[t] * rstd * gn_ref[:, cols]).astype(h_ref.dtype)


def _out_proj(o_fox, g_fox, r, w, res, g_next):
    m, fw = o_fox.shape
    rw = r.shape[1]
    n = w.shape[1]
    tm = _tile(m, 512, 16)
    tn = _tile(n, 1024)
    return pl.pallas_call(
        functools.partial(_mix_mm_kernel, fw=fw),
        grid=(m // tm, n // tn),
        in_specs=[pl.BlockSpec((tm, fw), lambda i, j: (i, 0)),
                  pl.BlockSpec((1, fw), lambda i, j: (0, 0)),
                  pl.BlockSpec((tm, rw), lambda i, j: (i, 0)),
                  pl.BlockSpec((fw + rw, tn), lambda i, j: (0, j)),
                  pl.BlockSpec((tm, tn), lambda i, j: (i, j)),
                  pl.BlockSpec((1, n), lambda i, j: (0, 0))],
        out_specs=[pl.BlockSpec((tm, tn), lambda i, j: (i, j)),
                   pl.BlockSpec((tm, n), lambda i, j: (i, 0))],
        out_shape=[jax.ShapeDtypeStruct((m, n), F32), jax.ShapeDtypeStruct((m, n), BF16)],
        scratch_shapes=[pltpu.VMEM((tm, fw + rw), BF16), pltpu.VMEM((n // tn, tm, tn), F32)],
        compiler_params=_params("arbitrary", "arbitrary"),
        name="out_proj",
    )(o_fox, g_fox.reshape(1, fw), r, w, res, g_next.reshape(1, n))


def _mm_acc_kernel(a_ref, w_ref, res_ref, out_ref):
    @pl.when(pl.program_id(2) == 0)
    def _():
        out_ref[...] = res_ref[...]

    out_ref[...] += jnp.dot(a_ref[...], w_ref[...], preferred_element_type=F32)


def _matmul_residual_ksplit(a, w, res):
    m, k = a.shape
    n = w.shape[1]
    tm = _tile(m, 1024, 16)
    tn = _tile(n, 1024)
    tk = _tile(k, 4096)
    return pl.pallas_call(
        _mm_acc_kernel,
        grid=(m // tm, n // tn, k // tk),
        in_specs=[pl.BlockSpec((tm, tk), lambda i, j, kk: (i, kk)),
                  pl.BlockSpec((tk, tn), lambda i, j, kk: (kk, j)),
                  pl.BlockSpec((tm, tn), lambda i, j, kk: (i, j))],
        out_specs=pl.BlockSpec((tm, tn), lambda i, j, kk: (i, j)),
        out_shape=jax.ShapeDtypeStruct((m, n), F32),
        compiler_params=_params("arbitrary", "arbitrary", "arbitrary"),
        name="down_proj",
    )(a, w, res)


def _cumsum_kernel(x_ref, o_ref, *aug_ref, bias_scale):
    x = x_ref[...]
    h, t = x.shape
    lane = lax.broadcasted_iota(jnp.int32, x.shape, 1)
    d = 1
    while d < t:
        x = x + jnp.where(lane >= d, pltpu.roll(x, d, axis=1), 0.0)
        d *= 2
    o_ref[...] = x
    if aug_ref:
        (a_ref,) = aug_ref
        y = x * bias_scale
        hi = y.astype(BF16).astype(F32)
        r1 = y - hi
        mid = r1.astype(BF16).astype(F32)
        lo = r1 - mid
        rows = a_ref.shape[0]
        base = jnp.concatenate([hi, mid, lo, jnp.zeros((rows - 3 * h, t), F32)], axis=0)
        row = lax.broadcasted_iota(jnp.int32, base.shape, 0)
        a_ref[...] = jnp.where(row == 3 * h, 1.0, base).astype(BF16)


def _cumsum_lanes(x, bias_scale=None):
    b, h, t = x.shape
    out_specs = [pl.BlockSpec((None, h, t), lambda i: (i, 0, 0))]
    out_shape = [jax.ShapeDtypeStruct((b, h, t), F32)]
    if bias_scale is not None:
        assert 3 * h < LANES, h
        out_specs.append(pl.BlockSpec((None, LANES, t), lambda i: (i, 0, 0)))
        out_shape.append(jax.ShapeDtypeStruct((b, LANES, t), BF16))
    outs = pl.pallas_call(
        functools.partial(_cumsum_kernel, bias_scale=bias_scale),
        grid=(b,),
        in_specs=[pl.BlockSpec((None, h, t), lambda i: (i, 0, 0))],
        out_specs=out_specs,
        out_shape=out_shape,
        compiler_params=_params("arbitrary"),
        name="cumsum_logf",
    )(x)
    return outs[0] if bias_scale is None else tuple(outs)


def _qk(q_bf16, k_bf16):
    return lax.dot_general(q_bf16, k_bf16, (((1,), (1,)), ((), ())), preferred_element_type=F32)


def _head_column(c_ref, h):
    c_all = c_ref[...]
    head = lax.broadcasted_iota(jnp.int32, c_all.shape, 1)
    return jnp.sum(jnp.where(head == h, c_all, 0.0), axis=1, keepdims=True)


def _fox_prompt_kernel(q_ref, k_ref, v_ref, ac_ref, ar_ref, *rest, tq, heads, side):
    if side:
        side_in, o_ref, side_out, qa_scr, kt_scr, vb_scr = rest
        side_out[...] = side_in[...].astype(BF16)
    else:
        o_ref, qa_scr, kt_scr, vb_scr = rest
    hp = vb_scr.shape[0]
    seq = q_ref.shape[0]
    r = lax.broadcasted_iota(jnp.int32, (LANES, LANES), 0)
    c = lax.broadcasted_iota(jnp.int32, (LANES, LANES), 1)
    for hh in range(hp):
        h = pl.program_id(1) * hp + hh
        cols = slice(hh * HD_FOX, (hh + 1) * HD_FOX)
        pick_q = (((r == h) & (c == 0)) | ((r == heads + h) & (c == 1)) | ((r == 2 * heads + h) & (c == 2))
                  | ((r == 3 * heads) & (c >= 3) & (c < 6)))
        pick_k = (((r == 3) & (c == h)) | ((r == 4) & (c == heads + h)) | ((r == 5) & (c == 2 * heads + h)))
        sel_q = jnp.where(pick_q, 1.0, 0.0).astype(BF16)
        sel_k = (jnp.where((r < 3) & (c == 3 * heads), 1.0, 0.0) - jnp.where(pick_k, 1.0, 0.0)).astype(BF16)
        qa_scr[hh, :, :HD_FOX] = q_ref[:, cols]
        qa_scr[hh, :, HD_FOX:] = jnp.dot(ac_ref[...], sel_q, preferred_element_type=F32).astype(BF16)
        kt_scr[hh, :HD_FOX, :] = k_ref[:, cols].T.astype(BF16)
        kt_scr[hh, HD_FOX:, :] = jnp.dot(sel_k, ar_ref[...], preferred_element_type=F32).astype(BF16)
        vb_scr[hh] = v_ref[:, cols].astype(BF16)
    c2 = (HD_FOX ** -0.5) * 1.4426950408889634
    row = lax.broadcasted_iota(jnp.int32, (tq, tq), 0)
    col = lax.broadcasted_iota(jnp.int32, (tq, tq), 1)
    causal = col <= row
    heads_here = range(hp)
    for qi in range(seq // tq):
        lo, hi = qi * tq, (qi + 1) * tq
        s_d = [jnp.where(causal, jnp.dot(qa_scr[hh, lo:hi, :], kt_scr[hh, :, lo:hi], preferred_element_type=F32),
                         -jnp.inf) for hh in heads_here]
        m = [jnp.max(s, axis=1, keepdims=True) for s in s_d]
        if qi > 0:
            s_o = [jnp.dot(qa_scr[hh, lo:hi, :], kt_scr[hh, :, :lo], preferred_element_type=F32)
                   for hh in heads_here]
            m = [jnp.maximum(mm, jnp.max(s, axis=1, keepdims=True)) for mm, s in zip(m, s_o)]
            p_o = [jnp.exp2((s - mm) * c2) for s, mm in zip(s_o, m)]
        p_d = [jnp.exp2((s - mm) * c2) for s, mm in zip(s_d, m)]
        for hh in heads_here:
            l = jnp.sum(p_d[hh], axis=1, keepdims=True)
            acc = jnp.dot(p_d[hh].astype(BF16), vb_scr[hh, lo:hi, :], preferred_element_type=F32)
            if qi > 0:
                l = l + jnp.sum(p_o[hh], axis=1, keepdims=True)
                acc = acc + jnp.dot(p_o[hh].astype(BF16), vb_scr[hh, :lo, :], preferred_element_type=F32)
            o_ref[lo:hi, hh * HD_FOX:(hh + 1) * HD_FOX] = acc / l


def _fox_prompt(q, k, v, aug_cols, aug_rows, batch, seq, heads, cast_w):
    m, fw = k.shape
    tq = _tile(seq, 256)
    hp = 2 if heads % 2 == 0 else 1
    grid = (batch, heads // hp)
    head_blk = lambda b, h: (b, h)
    in_specs = [pl.BlockSpec((seq, hp * HD_FOX), head_blk),
                pl.BlockSpec((seq, hp * HD_FOX), head_blk),
                pl.BlockSpec((seq, hp * HD_FOX), head_blk),
                pl.BlockSpec((seq, LANES), lambda b, h: (b, 0)),
                pl.BlockSpec((None, LANES, seq), lambda b, h: (b, 0, 0))]
    args = [q, k, v, aug_cols, aug_rows]
    out_specs = [pl.BlockSpec((seq, hp * HD_FOX), head_blk)]
    out_shape = [jax.ShapeDtypeStruct((m, fw), F32)]
    side = _side_cast_specs(cast_w, grid)
    if side:
        in_specs.append(side[0])
        args.append(cast_w)
        out_specs.append(side[0])
        out_shape.append(side[1])
    outs = pl.pallas_call(
        functools.partial(_fox_prompt_kernel, tq=tq, heads=heads, side=bool(side)),
        grid=grid,
        in_specs=in_specs,
        out_specs=out_specs,
        out_shape=out_shape,
        scratch_shapes=[pltpu.VMEM((hp, seq, 2 * HD_FOX), BF16), pltpu.VMEM((hp, 2 * HD_FOX, seq), BF16),
                        pltpu.VMEM((hp, seq, HD_FOX), BF16)],
        compiler_params=_params("arbitrary", "arbitrary"),
        name="fox_prompt",
    )(*args)
    return outs[0], (outs[1] if side else cast_w.astype(BF16))


def _every_eighth_row(ref, j):
    n, s, d = ref.shape
    return ref.reshape(n * s, d)[pl.ds(j, n, stride=s), :]


def _fox_sample_kernel(*refs, heads):
    ng = heads // SUBLANES
    q_ref = refs[0]
    k_refs = refs[1:1 + ng]
    v_refs = refs[1 + ng:1 + 2 * ng]
    kn_ref, vn_ref, cq_ref, ckc_ref, ckn_ref, o_ref, m_scr, l_scr, acc_scr, s_scr, p_scr = refs[1 + 2 * ng:]
    ki = pl.program_id(1)
    last = pl.num_programs(1) - 1
    scale = HD_FOX ** -0.5
    t = q_ref.shape[0]
    head_cols = lambda h: slice(h * HD_FOX, (h + 1) * HD_FOX)
    head_rows = lambda h: slice(h * t, (h + 1) * t)

    @pl.when(ki == 0)
    def _():
        m_scr[...] = jnp.full(m_scr.shape, -jnp.inf, F32)
        l_scr[...] = jnp.zeros(l_scr.shape, F32)
        acc_scr[...] = jnp.zeros(acc_scr.shape, F32)

    def cached(group_refs, h):
        return _every_eighth_row(group_refs[h // SUBLANES], h % SUBLANES).astype(BF16)

    def update(load_k, load_v, ck_ref, n, causal):
        for h in range(heads):
            s = _qk(q_ref[:, head_cols(h)], load_k(h)) * scale + cq_ref[:, h:h + 1] - ck_ref[h:h + 1, :]
            if causal:
                row = lax.broadcasted_iota(jnp.int32, s.shape, 0)
                col = lax.broadcasted_iota(jnp.int32, s.shape, 1)
                s = jnp.where(col <= row, s, -jnp.inf)
            s_scr[head_rows(h), :n] = s
        s = s_scr[:, :n]
        m_prev = m_scr[:, :1]
        m_new = jnp.maximum(m_prev, jnp.max(s, axis=1, keepdims=True))
        alpha = jnp.exp(m_prev - m_new)
        p = jnp.exp(s - m_new)
        l_new = alpha * l_scr[:, :1] + jnp.sum(p, axis=1, keepdims=True)
        m_scr[...] = jnp.broadcast_to(m_new, m_scr.shape)
        l_scr[...] = jnp.broadcast_to(l_new, l_scr.shape)
        p_scr[:, :n] = p.astype(BF16)
        for h in range(heads):
            pv = jnp.dot(p_scr[head_rows(h), :n], load_v(h), preferred_element_type=F32)
            acc_scr[:, head_cols(h)] = alpha[head_rows(h)] * acc_scr[:, head_cols(h)] + pv

    @pl.when(ki < last)
    def _():
        update(functools.partial(cached, k_refs), functools.partial(cached, v_refs), ckc_ref,
               s_scr.shape[1], causal=False)

    @pl.when(ki == last)
    def _():
        new_rows = lambda ref, h: ref[:, head_cols(h)].astype(BF16)
        update(functools.partial(new_rows, kn_ref), functools.partial(new_rows, vn_ref), ckn_ref, t, causal=True)
        for h in range(heads):
            o_ref[:, head_cols(h)] = acc_scr[:, head_cols(h)] / l_scr[head_rows(h), :1]


def _fox_sample(q, kc, vc, kn, vn, cq, ckc, ckn, batch, t_new, heads):
    past = kc.shape[1]
    fw = heads * HD_FOX
    assert heads % SUBLANES == 0, heads
    ng = heads // SUBLANES
    tk = _tile(past, 1024)
    nkc = past // tk
    kc = kc.reshape(batch, past, ng, SUBLANES, HD_FOX)
    vc = vc.reshape(batch, past, ng, SUBLANES, HD_FOX)
    row_blk = pl.BlockSpec((t_new, fw), lambda b, ki: (b, 0))
    group_blk = [pl.BlockSpec((None, tk, None, SUBLANES, HD_FOX),
                              functools.partial(lambda b, ki, g: (b, jnp.minimum(ki, nkc - 1), g, 0, 0), g=g))
                 for g in range(ng)]
    return pl.pallas_call(
        functools.partial(_fox_sample_kernel, heads=heads),
        grid=(batch, nkc + 1),
        in_specs=[row_blk] + group_blk + group_blk + [
            row_blk, row_blk,
            pl.BlockSpec((t_new, heads), lambda b, ki: (b, 0)),
            pl.BlockSpec((None, heads, tk), lambda b, ki: (b, 0, jnp.minimum(ki, nkc - 1))),
            pl.BlockSpec((None, heads, t_new), lambda b, ki: (b, 0, 0))],
        out_specs=row_blk,
        out_shape=jax.ShapeDtypeStruct((batch * t_new, fw), F32),
        scratch_shapes=[pltpu.VMEM((heads * t_new, LANES), F32), pltpu.VMEM((heads * t_new, LANES), F32),
                        pltpu.VMEM((t_new, fw), F32),
                        pltpu.VMEM((heads * t_new, tk), F32), pltpu.VMEM((heads * t_new, tk), BF16)],
        compiler_params=_params("arbitrary", "arbitrary"),
        name="fox_sample",
    )(q, *([kc] * ng), *([vc] * ng), kn, vn, cq, ckc, ckn)


def _retention_kernel(q_ref, k_ref, v_ref, g_ref, cos_ref, sin_ref, gro_ref, lg_ref, s0_ref, *rest,
                      blk, heads, side):
    if side:
        side_in, r_ref, sout_ref, side_out = rest
        side_out[...] = side_in[...].astype(BF16)
    else:
        r_ref, sout_ref = rest
    seq = q_ref.shape[0]
    half = DK_RET // 2
    ri = lax.broadcasted_iota(jnp.int32, (blk, blk), 0)
    ci = lax.broadcasted_iota(jnp.int32, (blk, blk), 1)
    dist = jnp.abs(ri - ci).astype(F32)
    visible = ci // CHUNK <= ri // CHUNK
    idx = lax.broadcasted_iota(jnp.int32, (blk, 1), 0).astype(F32)
    for hh in range(heads):
        lg = lg_ref[hh, :, :1]
        decay = jnp.where(visible, jnp.exp(lg * dist), 0.0)
        cross = jnp.exp(lg * (idx + 1.0))
        k_decay = jnp.exp(lg * (blk - 1.0 - idx))
        s_decay = jnp.exp(lg * blk)
        qk_cols = slice(hh * DK_RET, (hh + 1) * DK_RET)
        v_cols = slice(hh * DV_RET, (hh + 1) * DV_RET)
        gro = gro_ref[:, v_cols]
        s = s0_ref[hh]
        for t in range(seq // blk):
            rows = slice(t * blk, (t + 1) * blk)
            cos = cos_ref[rows, :]
            sin = sin_ref[rows, :]
            q = q_ref[rows, qk_cols]
            k = k_ref[rows, qk_cols]
            q = q * cos + pltpu.roll(q, half, axis=1) * sin
            k = (k * cos + pltpu.roll(k, half, axis=1) * sin) * (DK_RET ** -0.5)
            v = v_ref[rows, v_cols]
            qb = q.astype(BF16)
            a = _qk(qb, k.astype(BF16)) * decay
            o = jnp.dot(a.astype(BF16), v, preferred_element_type=F32)
            o = o + jnp.dot(qb, s.astype(BF16), preferred_element_type=F32) * cross
            s = s_decay * s + lax.dot_general((k * k_decay).astype(BF16), v, (((0,), (0,)), ((), ())),
                                              preferred_element_type=F32)
            ms = jnp.mean(o * o, axis=-1, keepdims=True)
            g = g_ref[rows, v_cols]
            silu = g / (1.0 + jnp.exp(-g))
            r_ref[rows, v_cols] = (o * lax.rsqrt(ms + EPS) * gro * silu).astype(r_ref.dtype)
        sout_ref[hh] = s


def _retention(qkg, qv, cos, sin, g_ret_out, lg, s0, batch, seq, heads, blk, heads_per_step, fox_width,
               cast_w=None):
    m = qkg.shape[0]
    hps = heads_per_step
    groups = heads // hps
    assert fox_width % (hps * DV_RET) == 0, (fox_width, hps)
    v_off = fox_width // (hps * DV_RET)
    grid = (batch, groups)
    row = lambda b, h: (b, h)
    state = pl.BlockSpec((None, hps, DK_RET, DV_RET), lambda b, h: (b, h, 0, 0))
    in_specs = [pl.BlockSpec((seq, hps * DK_RET), row),
                pl.BlockSpec((seq, hps * DK_RET), lambda b, h: (b, groups + h)),
                pl.BlockSpec((seq, hps * DV_RET), lambda b, h: (b, v_off + h)),
                pl.BlockSpec((seq, hps * DV_RET), lambda b, h: (b, groups + h)),
                pl.BlockSpec((seq, DK_RET), lambda b, h: (0, 0)),
                pl.BlockSpec((seq, DK_RET), lambda b, h: (0, 0)),
                pl.BlockSpec((1, hps * DV_RET), lambda b, h: (0, h)),
                pl.BlockSpec((hps, 1, LANES), lambda b, h: (h, 0, 0)),
                state]
    args = [qkg, qkg, qv, qkg, cos, sin, g_ret_out.reshape(1, -1), lg, s0]
    out_specs = [pl.BlockSpec((seq, hps * DV_RET), row), state]
    out_shape = [jax.ShapeDtypeStruct((m, heads * DV_RET), BF16),
                 jax.ShapeDtypeStruct((batch, heads, DK_RET, DV_RET), F32)]
    side = _side_cast_specs(cast_w, grid) if cast_w is not None else None
    if side:
        in_specs.append(side[0])
        args.append(cast_w)
        out_specs.append(side[0])
        out_shape.append(side[1])
    outs = pl.pallas_call(
        functools.partial(_retention_kernel, blk=blk, heads=hps, side=bool(side)),
        grid=grid,
        in_specs=in_specs,
        out_specs=out_specs,
        out_shape=out_shape,
        compiler_params=_params("arbitrary", "arbitrary"),
        name="retention",
    )(*args)
    if cast_w is None:
        return outs[0], outs[1]
    return outs[0], outs[1], (outs[2] if side else cast_w.astype(BF16))


def _rotary_tables(pos):
    half = DK_RET // 2
    inv = ROPE_BASE ** (-jnp.arange(half, dtype=F32) / half)
    ang = pos.astype(F32)[:, None] * inv[None, :]
    cos = jnp.cos(ang)
    sin = jnp.sin(ang)
    return jnp.concatenate([cos, cos], axis=-1), jnp.concatenate([-sin, sin], axis=-1)


def _feature_rows_kernel(w_ref, o_ref, *, valid):
    x = w_ref[...]
    if valid < x.shape[0]:
        row = lax.broadcasted_iota(jnp.int32, x.shape, 0)
        x = jnp.where(row < valid, x, 0.0)
    o_ref[...] = x.T.astype(BF16)


def _feature_rows(w_t, layer, segments, rows_per_step, valid=None):
    _, _, d = w_t.shape
    rb = rows_per_step
    starts = [s + k * rb for s, n in segments for k in range(n // rb)]
    assert all(n % rb == 0 for _, n in segments) and all(s % SUBLANES == 0 for s in starts), segments

    def index_map(i):
        tile_row = starts[-1] // SUBLANES
        for k in range(len(starts) - 2, -1, -1):
            tile_row = jnp.where(i <= k, starts[k] // SUBLANES, tile_row)
        return (layer, tile_row * SUBLANES, 0)

    return pl.pallas_call(
        functools.partial(_feature_rows_kernel, valid=rb if valid is None else valid),
        grid=(len(starts),),
        in_specs=[pl.BlockSpec((None, pl.Element(rb), pl.Element(d)), index_map)],
        out_specs=pl.BlockSpec((d, rb), lambda i: (0, i)),
        out_shape=jax.ShapeDtypeStruct((d, rb * len(starts)), BF16),
        compiler_params=_params("arbitrary"),
        name="w_in_rows",
    )(w_t)


def _in_proj_weights(w_in, layer, b_forget, h_fox, h_ret):
    fw = h_fox * HD_FOX
    qkw = h_ret * DK_RET
    vw = h_ret * DV_RET
    o1, o2, o3 = fw, 2 * fw, 3 * fw
    o4 = o3 + h_fox
    o6 = o4 + 2 * qkw
    o7 = o6 + vw
    w_t = jnp.swapaxes(w_in, 1, 2)
    rb = 512
    return dict(
        qv=_feature_rows(w_t, layer, [(0, fw), (o6, vw)], rb),
        k_f=_feature_rows(w_t, layer, [(o1, fw)], rb),
        v_f=_feature_rows(w_t, layer, [(o2, fw)], rb),
        qkg=_feature_rows(w_t, layer, [(o4, 2 * qkw), (o7, vw)], rb),
        logf=_feature_rows(w_t, layer, [(o3, LANES)], LANES, valid=h_fox),
        b_logf=jnp.pad(b_forget, (0, LANES - h_fox)).reshape(1, LANES).astype(F32))


def _project(x2d, g_attn, w):
    h = _rmsnorm(x2d, g_attn, BF16)
    qkg, logf = _matmul(h, w["qkg"], F32, logf=(w["logf"], w["b_logf"]))
    return dict(qv=_matmul(h, w["qv"], BF16), k_f=_matmul(h, w["k_f"], F32), v_f=_matmul(h, w["v_f"], F32),
                qkg=qkg, logf=logf)


def _finish_layer(x2d, o_fox, r, g_fox_out, g_mlp, w_out_b, w_up_b, w_down):
    x1, h2 = _out_proj(o_fox, g_fox_out, r, w_out_b, x2d, g_mlp)
    if w_down.dtype == BF16:
        u = _matmul(h2, w_up_b, BF16, epilogue="relu2")
    else:
        u, w_down = _matmul(h2, w_up_b, BF16, epilogue="relu2", cast_w=w_down)
    return _matmul_residual_ksplit(u, w_down, x1), w_down


def _retention_log_gamma(h_ret):
    lg = jnp.log1p(-jnp.exp2(-5.0 - jnp.arange(h_ret, dtype=F32)))
    return jnp.broadcast_to(lg[:, None, None], (h_ret, 1, LANES))


def kernel(x_prompt, x_sample, cache_fox_k, cache_fox_v, cache_fox_logf, state_ret,
           g_attn, w_in, b_forget, g_fox_out, g_ret_out, w_out, g_mlp, w_up, w_down, g_final):
    bp, seq, d = x_prompt.shape
    bs, t_new, _ = x_sample.shape
    depth, _, past, h_fox, _ = cache_fox_k.shape
    h_ret = state_ret.shape[2]
    fw = h_fox * HD_FOX
    dtype = x_prompt.dtype

    lg = _retention_log_gamma(h_ret)
    cos_p, sin_p = _rotary_tables(jnp.arange(seq))
    cos_s, sin_s = _rotary_tables(past + jnp.arange(t_new))
    blk_p = _tile(seq, 256, CHUNK)

    yp = x_prompt.reshape(bp * seq, d)
    ys = x_sample.reshape(bs * t_new, d)
    outs = [[] for _ in range(8)]
    for l in range(depth):
        w = _in_proj_weights(w_in, l, b_forget[l], h_fox, h_ret)

        p = _project(yp, g_attn[l], w)
        logf_p = p["logf"][:, :h_fox]
        _, aug_rows = _cumsum_lanes(jnp.swapaxes(logf_p.reshape(bp, seq, h_fox), 1, 2),
                                    bias_scale=HD_FOX ** 0.5)
        aug_cols = jnp.swapaxes(aug_rows, 1, 2).reshape(bp * seq, LANES)
        o_fox, w_up_b = _fox_prompt(p["qv"], p["k_f"], p["v_f"], aug_cols, aug_rows, bp, seq, h_fox, w_up[l])
        s0 = jnp.zeros((bp, h_ret, DK_RET, DV_RET), F32)
        r, s_fin, w_out_b = _retention(p["qkg"], p["qv"], cos_p, sin_p, g_ret_out[l], lg,
                                       s0, bp, seq, h_ret, blk_p, 1, fw, cast_w=w_out[l])
        yp, w_down_b = _finish_layer(yp, o_fox, r, g_fox_out[l], g_mlp[l], w_out_b, w_up_b, w_down[l])
        outs[0].append(p["k_f"].reshape(bp, seq, h_fox, HD_FOX))
        outs[1].append(p["v_f"].reshape(bp, seq, h_fox, HD_FOX))
        outs[2].append(logf_p.reshape(bp, seq, h_fox).astype(dtype))
        outs[3].append(s_fin.astype(dtype))

        p = _project(ys, g_attn[l], w)
        logf_s = p["logf"][:, :h_fox]
        total = past + t_new
        padded = -(-total // LANES) * LANES
        logf_all = jnp.concatenate(
            [jnp.swapaxes(cache_fox_logf[l].astype(F32), 1, 2),
             jnp.swapaxes(logf_s.reshape(bs, t_new, h_fox), 1, 2),
             jnp.zeros((bs, h_fox, padded - total), F32)], axis=2)
        c_all = _cumsum_lanes(logf_all)
        ckc = c_all[:, :, :past]
        ckn = c_all[:, :, past:total]
        cq = jnp.swapaxes(ckn, 1, 2).reshape(bs * t_new, h_fox)
        o_fox = _fox_sample(p["qv"], cache_fox_k[l], cache_fox_v[l], p["k_f"], p["v_f"],
                            cq, ckc, ckn, bs, t_new, h_fox)
        r, s_new = _retention(p["qkg"], p["qv"], cos_s, sin_s, g_ret_out[l], lg,
                              state_ret[l].astype(F32), bs, t_new, h_ret, t_new, h_ret, fw)
        ys, _ = _finish_layer(ys, o_fox, r, g_fox_out[l], g_mlp[l], w_out_b, w_up_b, w_down_b)
        outs[4].append(p["k_f"].reshape(bs, t_new, h_fox, HD_FOX))
        outs[5].append(p["v_f"].reshape(bs, t_new, h_fox, HD_FOX))
        outs[6].append(logf_s.reshape(bs, t_new, h_fox).astype(dtype))
        outs[7].append(s_new.astype(dtype))

    y_prompt = _rmsnorm(yp, g_final, dtype).reshape(bp, seq, d)
    y_sample = _rmsnorm(ys, g_final, dtype).reshape(bs, t_new, d)
    return (y_prompt, y_sample) + tuple(jnp.stack(o) for o in outs)
```

```python
import functools

import jax
import jax.numpy as jnp
from jax import lax
from jax.experimental import pallas as pl
from jax.experimental.pallas import tpu as pltpu

CHUNK = 64
HD_FOX = 128
DK_RET = 128
DV_RET = 256
ROPE_BASE = 10000.0
EPS = 1e-6

LANES = 128
SUBLANES = 8
VMEM_LIMIT_BYTES = 62 * 1024 * 1024

F32 = jnp.float32
BF16 = jnp.bfloat16


def _params(*sem):
    return pltpu.CompilerParams(dimension_semantics=sem, vmem_limit_bytes=VMEM_LIMIT_BYTES)


def _tile(n, pref, mult=LANES):
    if n <= pref:
        return n
    t = (pref // mult) * mult
    while t > mult and n % t:
        t -= mult
    assert n % t == 0, (n, pref)
    return t


def _rms_kernel(x_ref, g_ref, o_ref):
    x = x_ref[...]
    ms = jnp.mean(x * x, axis=-1, keepdims=True)
    o_ref[...] = (x * lax.rsqrt(ms + EPS) * g_ref[...]).astype(o_ref.dtype)


def _rmsnorm(x, g, out_dtype):
    m, d = x.shape
    tr = _tile(m, 256, 8)
    return pl.pallas_call(
        _rms_kernel,
        grid=(m // tr,),
        in_specs=[pl.BlockSpec((tr, d), lambda i: (i, 0)),
                  pl.BlockSpec((1, d), lambda i: (0, 0))],
        out_specs=pl.BlockSpec((tr, d), lambda i: (i, 0)),
        out_shape=jax.ShapeDtypeStruct((m, d), out_dtype),
        compiler_params=_params("arbitrary"),
        name="rmsnorm",
    )(x, g.reshape(1, d))


def _log_sigmoid(z):
    return jnp.minimum(z, 0.0) - jnp.log(1.0 + jnp.exp(-jnp.abs(z)))


def _side_cast_specs(w, grid):
    steps = 1
    for g in grid:
        steps *= g
    rows, cols = w.shape
    if rows % steps or (rows // steps) % 16:
        return None

    def index_map(*idx):
        lin = idx[0]
        for g, i in zip(grid[1:], idx[1:]):
            lin = lin * g + i
        return (lin, 0)

    spec = pl.BlockSpec((rows // steps, cols), index_map)
    return spec, jax.ShapeDtypeStruct((rows, cols), BF16)


def _mm_kernel(a_ref, w_ref, *rest, epilogue, side):
    if side:
        side_in, o_ref, side_out = rest
        side_out[...] = side_in[...].astype(BF16)
    else:
        (o_ref,) = rest
    acc = jnp.dot(a_ref[...], w_ref[...], preferred_element_type=F32)
    if epilogue == "relu2":
        u = jnp.maximum(acc, 0.0)
        o_ref[...] = (u * u).astype(o_ref.dtype)
    else:
        o_ref[...] = acc.astype(o_ref.dtype)


def _matmul(a, w, out_dtype, epilogue="cast", tm_pref=1024, tn_pref=1024, cast_w=None):
    m, k = a.shape
    n = w.shape[1]
    tm = _tile(m, tm_pref, 16)
    tn = _tile(n, tn_pref)
    grid = (m // tm, n // tn)
    in_specs = [pl.BlockSpec((tm, k), lambda i, j: (i, 0)),
                pl.BlockSpec((k, tn), lambda i, j: (0, j))]
    args = [a, w]
    out_specs = [pl.BlockSpec((tm, tn), lambda i, j: (i, j))]
    out_shape = [jax.ShapeDtypeStruct((m, n), out_dtype)]
    side = _side_cast_specs(cast_w, grid) if cast_w is not None else None
    if side:
        in_specs.append(side[0])
        args.append(cast_w)
        out_specs.append(side[0])
        out_shape.append(side[1])
    outs = list(pl.pallas_call(
        functools.partial(_mm_kernel, epilogue=epilogue, side=bool(side)),
        grid=grid,
        in_specs=in_specs,
        out_specs=out_specs,
        out_shape=out_shape,
        compiler_params=_params("arbitrary", "arbitrary"),
        name="matmul_" + epilogue,
    )(*args))
    if cast_w is not None and not side:
        outs.append(cast_w.astype(BF16))
    return outs[0] if len(outs) == 1 else tuple(outs)


def _mix_mm_kernel(o_ref, gf_ref, r_ref, w_ref, res_ref, gn_ref, out_ref, h_ref, a_scr, x_scr, *, fw):
    j = pl.program_id(1)
    nj, _, tn = x_scr.shape

    @pl.when(j == 0)
    def _():
        o = o_ref[...]
        ms = jnp.mean(o * o, axis=-1, keepdims=True)
        a_scr[:, :fw] = (o * lax.rsqrt(ms + EPS) * gf_ref[...]).astype(BF16)
        a_scr[:, fw:] = r_ref[...]

    x = res_ref[...] + jnp.dot(a_scr[...], w_ref[...], preferred_element_type=F32)
    out_ref[...] = x
    x_scr[j] = x

    @pl.when(j == nj - 1)
    def _():
        sq = x_scr[0] * x_scr[0]
        ssum = jnp.sum(sq, axis=-1, keepdims=True)
        for t in range(1, nj):
            xt = x_scr[t]
            ssum = ssum + jnp.sum(xt * xt, axis=-1, keepdims=True)
        rstd = lax.rsqrt(ssum / (nj * tn) + EPS)
        for t in range(nj):
            cols = slice(t * tn, (t + 1) * tn)
            h_ref[:, cols] = (x_scr[t] * rstd * gn_ref[:, cols]).astype(h_ref.dtype)


def _out_proj(o_fox, g_fox, r, w, res, g_next):
    m, fw = o_fox.shape
    rw = r.shape[1]
    n = w.shape[1]
    tm = _tile(m, 512, 16)
    tn = _tile(n, 1024)
    return pl.pallas_call(
        functools.partial(_mix_mm_kernel, fw=fw),
        grid=(m // tm, n // tn),
        in_specs=[pl.BlockSpec((tm, fw), lambda i, j: (i, 0)),
                  pl.BlockSpec((1, fw), lambda i, j: (0, 0)),
                  pl.BlockSpec((tm, rw), lambda i, j: (i, 0)),
                  pl.BlockSpec((fw + rw, tn), lambda i, j: (0, j)),
                  pl.BlockSpec((tm, tn), lambda i, j: (i, j)),
                  pl.BlockSpec((1, n), lambda i, j: (0, 0))],
        out_specs=[pl.BlockSpec((tm, tn), lambda i, j: (i, j)),
                   pl.BlockSpec((tm, n), lambda i, j: (i, 0))],
        out_shape=[jax.ShapeDtypeStruct((m, n), F32), jax.ShapeDtypeStruct((m, n), BF16)],
        scratch_shapes=[pltpu.VMEM((tm, fw + rw), BF16), pltpu.VMEM((n // tn, tm, tn), F32)],
        compiler_params=_params("arbitrary", "arbitrary"),
        name="out_proj",
    )(o_fox, g_fox.reshape(1, fw), r, w, res, g_next.reshape(1, n))


def _mm_acc_kernel(a_ref, w_ref, res_ref, out_ref):
    @pl.when(pl.program_id(2) == 0)
    def _():
        out_ref[...] = res_ref[...]

    out_ref[...] += jnp.dot(a_ref[...], w_ref[...], preferred_element_type=F32)


def _matmul_residual_ksplit(a, w, res):
    m, k = a.shape
    n = w.shape[1]
    tm = _tile(m, 1024, 16)
    tn = _tile(n, 1024)
    tk = _tile(k, 4096)
    return pl.pallas_call(
        _mm_acc_kernel,
        grid=(m // tm, n // tn, k // tk),
        in_specs=[pl.BlockSpec((tm, tk), lambda i, j, kk: (i, kk)),
                  pl.BlockSpec((tk, tn), lambda i, j, kk: (kk, j)),
                  pl.BlockSpec((tm, tn), lambda i, j, kk: (i, j))],
        out_specs=pl.BlockSpec((tm, tn), lambda i, j, kk: (i, j)),
        out_shape=jax.ShapeDtypeStruct((m, n), F32),
        compiler_params=_params("arbitrary", "arbitrary", "arbitrary"),
        name="down_proj",
    )(a, w, res)


def _cumsum_kernel(x_ref, o_ref, *aug_ref, bias_scale):
    x = x_ref[...]
    h, t = x.shape
    lane = lax.broadcasted_iota(jnp.int32, x.shape, 1)
    d = 1
    while d < t:
        x = x + jnp.where(lane >= d, pltpu.roll(x, d, axis=1), 0.0)
        d *= 2
    o_ref[...] = x
    if aug_ref:
        (a_ref,) = aug_ref
        y = x * bias_scale
        hi = y.astype(BF16).astype(F32)
        r1 = y - hi
        mid = r1.astype(BF16).astype(F32)
        lo = r1 - mid
        rows = a_ref.shape[0]
        base = jnp.concatenate([hi, mid, lo, jnp.zeros((rows - 3 * h, t), F32)], axis=0)
        row = lax.broadcasted_iota(jnp.int32, base.shape, 0)
        a_ref[...] = jnp.where(row == 3 * h, 1.0, base).astype(BF16)


def _cumsum_lanes(x, bias_scale=None):
    b, h, t = x.shape
    out_specs = [pl.BlockSpec((None, h, t), lambda i: (i, 0, 0))]
    out_shape = [jax.ShapeDtypeStruct((b, h, t), F32)]
    if bias_scale is not None:
        assert 3 * h < LANES, h
        out_specs.append(pl.BlockSpec((None, LANES, t), lambda i: (i, 0, 0)))
        out_shape.append(jax.ShapeDtypeStruct((b, LANES, t), BF16))
    outs = pl.pallas_call(
        functools.partial(_cumsum_kernel, bias_scale=bias_scale),
        grid=(b,),
        in_specs=[pl.BlockSpec((None, h, t), lambda i: (i, 0, 0))],
        out_specs=out_specs,
        out_shape=out_shape,
        compiler_params=_params("arbitrary"),
        name="cumsum_logf",
    )(x)
    return outs[0] if bias_scale is None else tuple(outs)


def _qk(q_bf16, k_bf16):
    return lax.dot_general(q_bf16, k_bf16, (((1,), (1,)), ((), ())), preferred_element_type=F32)


def _fox_prompt_kernel(q_ref, k_ref, v_ref, ac_ref, ar_ref, *rest, tq, heads, side):
    if side:
        side_in, o_ref, side_out, qa_scr, kt_scr, vb_scr = rest
        side_out[...] = side_in[...].astype(BF16)
    else:
        o_ref, qa_scr, kt_scr, vb_scr = rest
    hp = vb_scr.shape[0]
    seq = q_ref.shape[0]
    r = lax.broadcasted_iota(jnp.int32, (LANES, LANES), 0)
    c = lax.broadcasted_iota(jnp.int32, (LANES, LANES), 1)
    for hh in range(hp):
        h = pl.program_id(1) * hp + hh
        cols = slice(hh * HD_FOX, (hh + 1) * HD_FOX)
        pick_q = (((r == h) & (c == 0)) | ((r == heads + h) & (c == 1)) | ((r == 2 * heads + h) & (c == 2))
                  | ((r == 3 * heads) & (c >= 3) & (c < 6)))
        pick_k = (((r == 3) & (c == h)) | ((r == 4) & (c == heads + h)) | ((r == 5) & (c == 2 * heads + h)))
        sel_q = jnp.where(pick_q, 1.0, 0.0).astype(BF16)
        sel_k = (jnp.where((r < 3) & (c == 3 * heads), 1.0, 0.0) - jnp.where(pick_k, 1.0, 0.0)).astype(BF16)
        qa_scr[hh, :, :HD_FOX] = q_ref[:, cols]
        qa_scr[hh, :, HD_FOX:] = jnp.dot(ac_ref[...], sel_q, preferred_element_type=F32).astype(BF16)
        kt_scr[hh, :HD_FOX, :] = k_ref[:, cols].T.astype(BF16)
        kt_scr[hh, HD_FOX:, :] = jnp.dot(sel_k, ar_ref[...], preferred_element_type=F32).astype(BF16)
        vb_scr[hh] = v_ref[:, cols].astype(BF16)
    c2 = (HD_FOX ** -0.5) * 1.4426950408889634
    row = lax.broadcasted_iota(jnp.int32, (tq, tq), 0)
    col = lax.broadcasted_iota(jnp.int32, (tq, tq), 1)
    causal = col <= row
    heads_here = range(hp)
    for qi in range(seq // tq):
        lo, hi = qi * tq, (qi + 1) * tq
        s_d = [jnp.where(causal, jnp.dot(qa_scr[hh, lo:hi, :], kt_scr[hh, :, lo:hi], preferred_element_type=F32),
                         -jnp.inf) for hh in heads_here]
        m = [jnp.max(s, axis=1, keepdims=True) for s in s_d]
        if qi > 0:
            s_o = [jnp.dot(qa_scr[hh, lo:hi, :], kt_scr[hh, :, :lo], preferred_element_type=F32)
                   for hh in heads_here]
            m = [jnp.maximum(mm, jnp.max(s, axis=1, keepdims=True)) for mm, s in zip(m, s_o)]
            p_o = [jnp.exp2((s - mm) * c2) for s, mm in zip(s_o, m)]
        p_d = [jnp.exp2((s - mm) * c2) for s, mm in zip(s_d, m)]
        for hh in heads_here:
            l = jnp.sum(p_d[hh], axis=1, keepdims=True)
            acc = jnp.dot(p_d[hh].astype(BF16), vb_scr[hh, lo:hi, :], preferred_element_type=F32)
            if qi > 0:
                l = l + jnp.sum(p_o[hh], axis=1, keepdims=True)
                acc = acc + jnp.dot(p_o[hh].astype(BF16), vb_scr[hh, :lo, :], preferred_element_type=F32)
            o_ref[lo:hi, hh * HD_FOX:(hh + 1) * HD_FOX] = acc / l


def _fox_prompt(q, k, v, aug_cols, aug_rows, batch, seq, heads, cast_w):
    m, fw = k.shape
    tq = _tile(seq, 256)
    hp = 2 if heads % 2 == 0 else 1
    grid = (batch, heads // hp)
    head_blk = lambda b, h: (b, h)
    in_specs = [pl.BlockSpec((seq, hp * HD_FOX), head_blk),
                pl.BlockSpec((seq, hp * HD_FOX), head_blk),
                pl.BlockSpec((seq, hp * HD_FOX), head_blk),
                pl.BlockSpec((seq, LANES), lambda b, h: (b, 0)),
                pl.BlockSpec((None, LANES, seq), lambda b, h: (b, 0, 0))]
    args = [q, k, v, aug_cols, aug_rows]
    out_specs = [pl.BlockSpec((seq, hp * HD_FOX), head_blk)]
    out_shape = [jax.ShapeDtypeStruct((m, fw), F32)]
    side = _side_cast_specs(cast_w, grid)
    if side:
        in_specs.append(side[0])
        args.append(cast_w)
        out_specs.append(side[0])
        out_shape.append(side[1])
    outs = pl.pallas_call(
        functools.partial(_fox_prompt_kernel, tq=tq, heads=heads, side=bool(side)),
        grid=grid,
        in_specs=in_specs,
        out_specs=out_specs,
        out_shape=out_shape,
        scratch_shapes=[pltpu.VMEM((hp, seq, 2 * HD_FOX), BF16), pltpu.VMEM((hp, 2 * HD_FOX, seq), BF16),
                        pltpu.VMEM((hp, seq, HD_FOX), BF16)],
        compiler_params=_params("arbitrary", "arbitrary"),
        name="fox_prompt",
    )(*args)
    return outs[0], (outs[1] if side else cast_w.astype(BF16))


def _every_eighth_row(ref, j):
    n, s, d = ref.shape
    return ref.reshape(n * s, d)[pl.ds(j, n, stride=s), :]


def _fox_sample_kernel(*refs, heads):
    ng = heads // SUBLANES
    q_ref = refs[0]
    k_refs = refs[1:1 + ng]
    v_refs = refs[1 + ng:1 + 2 * ng]
    kn_ref, vn_ref, cq_ref, ckc_ref, ckn_ref, o_ref, m_scr, l_scr, acc_scr, s_scr, p_scr = refs[1 + 2 * ng:]
    ki = pl.program_id(1)
    last = pl.num_programs(1) - 1
    scale = HD_FOX ** -0.5
    t = q_ref.shape[0]
    head_cols = lambda h: slice(h * HD_FOX, (h + 1) * HD_FOX)
    head_rows = lambda h: slice(h * t, (h + 1) * t)

    @pl.when(ki == 0)
    def _():
        m_scr[...] = jnp.full(m_scr.shape, -jnp.inf, F32)
        l_scr[...] = jnp.zeros(l_scr.shape, F32)
        acc_scr[...] = jnp.zeros(acc_scr.shape, F32)

    def cached(group_refs, h):
        return _every_eighth_row(group_refs[h // SUBLANES], h % SUBLANES).astype(BF16)

    def update(load_k, load_v, ck_ref, n, causal):
        for h in range(heads):
            s = _qk(q_ref[:, head_cols(h)], load_k(h)) * scale + cq_ref[:, h:h + 1] - ck_ref[h:h + 1, :]
            if causal:
                row = lax.broadcasted_iota(jnp.int32, s.shape, 0)
                col = lax.broadcasted_iota(jnp.int32, s.shape, 1)
                s = jnp.where(col <= row, s, -jnp.inf)
            s_scr[head_rows(h), :n] = s
        s = s_scr[:, :n]
        m_prev = m_scr[:, :1]
        m_new = jnp.maximum(m_prev, jnp.max(s, axis=1, keepdims=True))
        alpha = jnp.exp(m_prev - m_new)
        p = jnp.exp(s - m_new)
        l_new = alpha * l_scr[:, :1] + jnp.sum(p, axis=1, keepdims=True)
        m_scr[...] = jnp.broadcast_to(m_new, m_scr.shape)
        l_scr[...] = jnp.broadcast_to(l_new, l_scr.shape)
        p_scr[:, :n] = p.astype(BF16)
        for h in range(heads):
            pv = jnp.dot(p_scr[head_rows(h), :n], load_v(h), preferred_element_type=F32)
            acc_scr[:, head_cols(h)] = alpha[head_rows(h)] * acc_scr[:, head_cols(h)] + pv

    @pl.when(ki < last)
    def _():
        update(functools.partial(cached, k_refs), functools.partial(cached, v_refs), ckc_ref,
               s_scr.shape[1], causal=False)

    @pl.when(ki == last)
    def _():
        new_rows = lambda ref, h: ref[:, head_cols(h)].astype(BF16)
        update(functools.partial(new_rows, kn_ref), functools.partial(new_rows, vn_ref), ckn_ref, t, causal=True)
        for h in range(heads):
            o_ref[:, head_cols(h)] = acc_scr[:, head_cols(h)] / l_scr[head_rows(h), :1]


def _fox_sample(q, kc, vc, kn, vn, cq, ckc, ckn, batch, t_new, heads):
    past = kc.shape[1]
    fw = heads * HD_FOX
    assert heads % SUBLANES == 0, heads
    ng = heads // SUBLANES
    tk = _tile(past, 1024)
    nkc = past // tk
    kc = kc.reshape(batch, past, ng, SUBLANES, HD_FOX)
    vc = vc.reshape(batch, past, ng, SUBLANES, HD_FOX)
    row_blk = pl.BlockSpec((t_new, fw), lambda b, ki: (b, 0))
    group_blk = [pl.BlockSpec((None, tk, None, SUBLANES, HD_FOX),
                              functools.partial(lambda b, ki, g: (b, jnp.minimum(ki, nkc - 1), g, 0, 0), g=g))
                 for g in range(ng)]
    return pl.pallas_call(
        functools.partial(_fox_sample_kernel, heads=heads),
        grid=(batch, nkc + 1),
        in_specs=[row_blk] + group_blk + group_blk + [
            row_blk, row_blk,
            pl.BlockSpec((t_new, heads), lambda b, ki: (b, 0)),
            pl.BlockSpec((None, heads, tk), lambda b, ki: (b, 0, jnp.minimum(ki, nkc - 1))),
            pl.BlockSpec((None, heads, t_new), lambda b, ki: (b, 0, 0))],
        out_specs=row_blk,
        out_shape=jax.ShapeDtypeStruct((batch * t_new, fw), F32),
        scratch_shapes=[pltpu.VMEM((heads * t_new, LANES), F32), pltpu.VMEM((heads * t_new, LANES), F32),
                        pltpu.VMEM((t_new, fw), F32),
                        pltpu.VMEM((heads * t_new, tk), F32), pltpu.VMEM((heads * t_new, tk), BF16)],
        compiler_params=_params("arbitrary", "arbitrary"),
        name="fox_sample",
    )(q, *([kc] * ng), *([vc] * ng), kn, vn, cq, ckc, ckn)


def _retention_kernel(q_ref, k_ref, v_ref, g_ref, cos_ref, sin_ref, gro_ref, lg_ref, s0_ref, *rest,
                      blk, heads, side):
    if side:
        side_in, r_ref, sout_ref, side_out = rest
        side_out[...] = side_in[...].astype(BF16)
    else:
        r_ref, sout_ref = rest
    seq = q_ref.shape[0]
    half = DK_RET // 2
    ri = lax.broadcasted_iota(jnp.int32, (blk, blk), 0)
    ci = lax.broadcasted_iota(jnp.int32, (blk, blk), 1)
    dist = jnp.abs(ri - ci).astype(F32)
    visible = ci // CHUNK <= ri // CHUNK
    idx = lax.broadcasted_iota(jnp.int32, (blk, 1), 0).astype(F32)
    for hh in range(heads):
        lg = lg_ref[hh, :, :1]
        decay = jnp.where(visible, jnp.exp(lg * dist), 0.0)
        cross = jnp.exp(lg * (idx + 1.0))
        k_decay = jnp.exp(lg * (blk - 1.0 - idx))
        s_decay = jnp.exp(lg * blk)
        qk_cols = slice(hh * DK_RET, (hh + 1) * DK_RET)
        v_cols = slice(hh * DV_RET, (hh + 1) * DV_RET)
        gro = gro_ref[:, v_cols]
        s = s0_ref[hh]
        for t in range(seq // blk):
            rows = slice(t * blk, (t + 1) * blk)
            cos = cos_ref[rows, :]
            sin = sin_ref[rows, :]
            q = q_ref[rows, qk_cols]
            k = k_ref[rows, qk_cols]
            q = q * cos + pltpu.roll(q, half, axis=1) * sin
            k = (k * cos + pltpu.roll(k, half, axis=1) * sin) * (DK_RET ** -0.5)
            v = v_ref[rows, v_cols]
            qb = q.astype(BF16)
            a = _qk(qb, k.astype(BF16)) * decay
            o = jnp.dot(a.astype(BF16), v, preferred_element_type=F32)
            o = o + jnp.dot(qb, s.astype(BF16), preferred_element_type=F32) * cross
            s = s_decay * s + lax.dot_general((k * k_decay).astype(BF16), v, (((0,), (0,)), ((), ())),
                                              preferred_element_type=F32)
            ms = jnp.mean(o * o, axis=-1, keepdims=True)
            g = g_ref[rows, v_cols]
            silu = g / (1.0 + jnp.exp(-g))
            r_ref[rows, v_cols] = (o * lax.rsqrt(ms + EPS) * gro * silu).astype(r_ref.dtype)
        sout_ref[hh] = s


def _retention(qkg, qv, cos, sin, g_ret_out, lg, s0, batch, seq, heads, blk, heads_per_step, fox_width,
               cast_w=None):
    m = qkg.shape[0]
    hps = heads_per_step
    groups = heads // hps
    assert fox_width % (hps * DV_RET) == 0, (fox_width, hps)
    v_off = fox_width // (hps * DV_RET)
    grid = (batch, groups)
    row = lambda b, h: (b, h)
    state = pl.BlockSpec((None, hps, DK_RET, DV_RET), lambda b, h: (b, h, 0, 0))
    in_specs = [pl.BlockSpec((seq, hps * DK_RET), row),
                pl.BlockSpec((seq, hps * DK_RET), lambda b, h: (b, groups + h)),
                pl.BlockSpec((seq, hps * DV_RET), lambda b, h: (b, v_off + h)),
                pl.BlockSpec((seq, hps * DV_RET), lambda b, h: (b, groups + h)),
                pl.BlockSpec((seq, DK_RET), lambda b, h: (0, 0)),
                pl.BlockSpec((seq, DK_RET), lambda b, h: (0, 0)),
                pl.BlockSpec((1, hps * DV_RET), lambda b, h: (0, h)),
                pl.BlockSpec((hps, 1, LANES), lambda b, h: (h, 0, 0)),
                state]
    args = [qkg, qkg, qv, qkg, cos, sin, g_ret_out.reshape(1, -1), lg, s0]
    out_specs = [pl.BlockSpec((seq, hps * DV_RET), row), state]
    out_shape = [jax.ShapeDtypeStruct((m, heads * DV_RET), BF16),
                 jax.ShapeDtypeStruct((batch, heads, DK_RET, DV_RET), F32)]
    side = _side_cast_specs(cast_w, grid) if cast_w is not None else None
    if side:
        in_specs.append(side[0])
        args.append(cast_w)
        out_specs.append(side[0])
        out_shape.append(side[1])
    outs = pl.pallas_call(
        functools.partial(_retention_kernel, blk=blk, heads=hps, side=bool(side)),
        grid=grid,
        in_specs=in_specs,
        out_specs=out_specs,
        out_shape=out_shape,
        compiler_params=_params("arbitrary", "arbitrary"),
        name="retention",
    )(*args)
    if cast_w is None:
        return outs[0], outs[1]
    return outs[0], outs[1], (outs[2] if side else cast_w.astype(BF16))


def _rotary_tables(pos):
    half = DK_RET // 2
    inv = ROPE_BASE ** (-jnp.arange(half, dtype=F32) / half)
    ang = pos.astype(F32)[:, None] * inv[None, :]
    cos = jnp.cos(ang)
    sin = jnp.sin(ang)
    return jnp.concatenate([cos, cos], axis=-1), jnp.concatenate([-sin, sin], axis=-1)


def _feature_rows_kernel(w_ref, o_ref, *, valid):
    x = w_ref[...]
    if valid < x.shape[0]:
        row = lax.broadcasted_iota(jnp.int32, x.shape, 0)
        x = jnp.where(row < valid, x, 0.0)
    o_ref[...] = x.T.astype(BF16)


def _feature_rows(w_t, layer, segments, rows_per_step, valid=None):
    _, _, d = w_t.shape
    rb = rows_per_step
    starts = [s + k * rb for s, n in segments for k in range(n // rb)]
    assert all(n % rb == 0 for _, n in segments) and all(s % SUBLANES == 0 for s in starts), segments

    def index_map(i):
        tile_row = starts[-1] // SUBLANES
        for k in range(len(starts) - 2, -1, -1):
            tile_row = jnp.where(i <= k, starts[k] // SUBLANES, tile_row)
        return (layer, tile_row * SUBLANES, 0)

    return pl.pallas_call(
        functools.partial(_feature_rows_kernel, valid=rb if valid is None else valid),
        grid=(len(starts),),
        in_specs=[pl.BlockSpec((None, pl.Element(rb), pl.Element(d)), index_map)],
        out_specs=pl.BlockSpec((d, rb), lambda i: (0, i)),
        out_shape=jax.ShapeDtypeStruct((d, rb * len(starts)), BF16),
        compiler_params=_params("arbitrary"),
        name="w_in_rows",
    )(w_t)


def _in_proj_weights(w_in, layer, b_forget, h_fox, h_ret):
    fw = h_fox * HD_FOX
    qkw = h_ret * DK_RET
    vw = h_ret * DV_RET
    o1, o2, o3 = fw, 2 * fw, 3 * fw
    o4 = o3 + h_fox
    o6 = o4 + 2 * qkw
    o7 = o6 + vw
    w_t = jnp.swapaxes(w_in, 1, 2)
    rb = 512
    return dict(
        qv=_feature_rows(w_t, layer, [(0, fw), (o6, vw)], rb),
        k_f=_feature_rows(w_t, layer, [(o1, fw)], rb),
        v_f=_feature_rows(w_t, layer, [(o2, fw)], rb),
        qkg=_feature_rows(w_t, layer, [(o4, 2 * qkw), (o7, vw)], rb),
        logf=_feature_rows(w_t, layer, [(o3, LANES)], LANES, valid=h_fox),
        b_logf=jnp.pad(b_forget, (0, LANES - h_fox)).reshape(1, LANES).astype(F32))


def _norm_mm_kernel(x_ref, g_ref, w_ref, wlf_ref, b_ref, o_ref, lf_ref, h_ref):
    @pl.when(pl.program_id(1) == 0)
    def _():
        x = x_ref[...]
        ms = jnp.mean(x * x, axis=-1, keepdims=True)
        h_ref[...] = (x * lax.rsqrt(ms + EPS) * g_ref[...]).astype(h_ref.dtype)
        lf_ref[...] = _log_sigmoid(jnp.dot(h_ref[...], wlf_ref[...], preferred_element_type=F32) + b_ref[...])

    o_ref[...] = jnp.dot(h_ref[...], w_ref[...], preferred_element_type=F32).astype(o_ref.dtype)


def _norm_matmul(x, g, w, w_lf, b_lf):
    m, k = x.shape
    n = w.shape[1]
    tm = _tile(m, 512, 16)
    tn = _tile(n, 1024)
    return pl.pallas_call(
        _norm_mm_kernel,
        grid=(m // tm, n // tn),
        in_specs=[pl.BlockSpec((tm, k), lambda i, j: (i, 0)),
                  pl.BlockSpec((1, k), lambda i, j: (0, 0)),
                  pl.BlockSpec((k, tn), lambda i, j: (0, j)),
                  pl.BlockSpec((k, LANES), lambda i, j: (0, 0)),
                  pl.BlockSpec((1, LANES), lambda i, j: (0, 0))],
        out_specs=[pl.BlockSpec((tm, tn), lambda i, j: (i, j)),
                   pl.BlockSpec((tm, LANES), lambda i, j: (i, 0)),
                   pl.BlockSpec((tm, k), lambda i, j: (i, 0))],
        out_shape=[jax.ShapeDtypeStruct((m, n), F32), jax.ShapeDtypeStruct((m, LANES), F32),
                   jax.ShapeDtypeStruct((m, k), BF16)],
        compiler_params=_params("arbitrary", "arbitrary"),
        name="norm_matmul",
    )(x, g.reshape(1, k), w, w_lf, b_lf)


def _project(x2d, g_attn, w):
    qkg, logf, h = _norm_matmul(x2d, g_attn, w["qkg"], w["logf"], w["b_logf"])
    return dict(qv=_matmul(h, w["qv"], BF16), k_f=_matmul(h, w["k_f"], F32), v_f=_matmul(h, w["v_f"], F32),
                qkg=qkg, logf=logf)


def _finish_layer(x2d, o_fox, r, g_fox_out, g_mlp, w_out_b, w_up_b, w_down):
    x1, h2 = _out_proj(o_fox, g_fox_out, r, w_out_b, x2d, g_mlp)
    if w_down.dtype == BF16:
        u = _matmul(h2, w_up_b, BF16, epilogue="relu2")
    else:
        u, w_down = _matmul(h2, w_up_b, BF16, epilogue="relu2", cast_w=w_down)
    return _matmul_residual_ksplit(u, w_down, x1), w_down


def _retention_log_gamma(h_ret):
    lg = jnp.log1p(-jnp.exp2(-5.0 - jnp.arange(h_ret, dtype=F32)))
    return jnp.broadcast_to(lg[:, None, None], (h_ret, 1, LANES))


def kernel(x_prompt, x_sample, cache_fox_k, cache_fox_v, cache_fox_logf, state_ret,
           g_attn, w_in, b_forget, g_fox_out, g_ret_out, w_out, g_mlp, w_up, w_down, g_final):
    bp, seq, d = x_prompt.shape
    bs, t_new, _ = x_sample.shape
    depth, _, past, h_fox, _ = cache_fox_k.shape
    h_ret = state_ret.shape[2]
    fw = h_fox * HD_FOX
    dtype = x_prompt.dtype

    lg = _retention_log_gamma(h_ret)
    cos_p, sin_p = _rotary_tables(jnp.arange(seq))
    cos_s, sin_s = _rotary_tables(past + jnp.arange(t_new))
    blk_p = _tile(seq, 256, CHUNK)

    yp = x_prompt.reshape(bp * seq, d)
    ys = x_sample.reshape(bs * t_new, d)
    outs = [[] for _ in range(8)]
    for l in range(depth):
        w = _in_proj_weights(w_in, l, b_forget[l], h_fox, h_ret)

        p = _project(yp, g_attn[l], w)
        logf_p = p["logf"][:, :h_fox]
        _, aug_rows = _cumsum_lanes(jnp.swapaxes(logf_p.reshape(bp, seq, h_fox), 1, 2),
                                    bias_scale=HD_FOX ** 0.5)
        aug_cols = jnp.swapaxes(aug_rows, 1, 2).reshape(bp * seq, LANES)
        o_fox, w_up_b = _fox_prompt(p["qv"], p["k_f"], p["v_f"], aug_cols, aug_rows, bp, seq, h_fox, w_up[l])
        s0 = jnp.zeros((bp, h_ret, DK_RET, DV_RET), F32)
        r, s_fin, w_out_b = _retention(p["qkg"], p["qv"], cos_p, sin_p, g_ret_out[l], lg,
                                       s0, bp, seq, h_ret, blk_p, 1, fw, cast_w=w_out[l])
        yp, w_down_b = _finish_layer(yp, o_fox, r, g_fox_out[l], g_mlp[l], w_out_b, w_up_b, w_down[l])
        outs[0].append(p["k_f"].reshape(bp, seq, h_fox, HD_FOX))
        outs[1].append(p["v_f"].reshape(bp, seq, h_fox, HD_FOX))
        outs[2].append(logf_p.reshape(bp, seq, h_fox).astype(dtype))
        outs[3].append(s_fin.astype(dtype))

        p = _project(ys, g_attn[l], w)
        logf_s = p["logf"][:, :h_fox]
        total = past + t_new
        padded = -(-total // LANES) * LANES
        logf_all = jnp.concatenate(
            [jnp.swapaxes(cache_fox_logf[l].astype(F32), 1, 2),
             jnp.swapaxes(logf_s.reshape(bs, t_new, h_fox), 1, 2),
             jnp.zeros((bs, h_fox, padded - total), F32)], axis=2)
        c_all = _cumsum_lanes(logf_all)
        ckc = c_all[:, :, :past]
        ckn = c_all[:, :, past:total]
        cq = jnp.swapaxes(ckn, 1, 2).reshape(bs * t_new, h_fox)
        o_fox = _fox_sample(p["qv"], cache_fox_k[l], cache_fox_v[l], p["k_f"], p["v_f"],
                            cq, ckc, ckn, bs, t_new, h_fox)
        r, s_new = _retention(p["qkg"], p["qv"], cos_s, sin_s, g_ret_out[l], lg,
                              state_ret[l].astype(F32), bs, t_new, h_ret, t_new, h_ret, fw)
        ys, _ = _finish_layer(ys, o_fox, r, g_fox_out[l], g_mlp[l], w_out_b, w_up_b, w_down_b)
        outs[4].append(p["k_f"].reshape(bs, t_new, h_fox, HD_FOX))
        outs[5].append(p["v_f"].reshape(bs, t_new, h_fox, HD_FOX))
        outs[6].append(logf_s.reshape(bs, t_new, h_fox).astype(dtype))
        outs[7].append(s_new.astype(dtype))

    y_prompt = _rmsnorm(yp, g_final, dtype).reshape(bp, seq, d)
    y_sample = _rmsnorm(ys, g_final, dtype).reshape(bs, t_new, d)
    return (y_prompt, y_sample) + tuple(jnp.stack(o) for o in outs)
```

```python
import functools

import jax
import jax.numpy as jnp
from jax import lax
from jax.experimental import pallas as pl
from jax.experimental.pallas import tpu as pltpu

CHUNK = 64
HD_FOX = 128
DK_RET = 128
DV_RET = 256
ROPE_BASE = 10000.0
EPS = 1e-6

LANES = 128
SUBLANES = 8
VMEM_LIMIT_BYTES = 62 * 1024 * 1024

F32 = jnp.float32
BF16 = jnp.bfloat16


def _params(*sem):
    return pltpu.CompilerParams(dimension_semantics=sem, vmem_limit_bytes=VMEM_LIMIT_BYTES)


def _tile(n, pref, mult=LANES):
    if n <= pref:
        return n
    t = (pref // mult) * mult
    while t > mult and n % t:
        t -= mult
    assert n % t == 0, (n, pref)
    return t


def _rms_kernel(x_ref, g_ref, o_ref):
    x = x_ref[...]
    ms = jnp.mean(x * x, axis=-1, keepdims=True)
    o_ref[...] = (x * lax.rsqrt(ms + EPS) * g_ref[...]).astype(o_ref.dtype)


def _rmsnorm(x, g, out_dtype):
    m, d = x.shape
    tr = _tile(m, 256, 8)
    return pl.pallas_call(
        _rms_kernel,
        grid=(m // tr,),
        in_specs=[pl.BlockSpec((tr, d), lambda i: (i, 0)),
                  pl.BlockSpec((1, d), lambda i: (0, 0))],
        out_specs=pl.BlockSpec((tr, d), lambda i: (i, 0)),
        out_shape=jax.ShapeDtypeStruct((m, d), out_dtype),
        compiler_params=_params("arbitrary"),
        name="rmsnorm",
    )(x, g.reshape(1, d))


def _log_sigmoid(z):
    return jnp.minimum(z, 0.0) - jnp.log(1.0 + jnp.exp(-jnp.abs(z)))


def _side_cast_specs(w, grid):
    steps = 1
    for g in grid:
        steps *= g
    rows, cols = w.shape
    if rows % steps or (rows // steps) % 16:
        return None

    def index_map(*idx):
        lin = idx[0]
        for g, i in zip(grid[1:], idx[1:]):
            lin = lin * g + i
        return (lin, 0)

    spec = pl.BlockSpec((rows // steps, cols), index_map)
    return spec, jax.ShapeDtypeStruct((rows, cols), BF16)


def _mm_kernel(a_ref, w_ref, *rest, epilogue, side, logf):
    rest = list(rest)
    side_out = rest.pop() if side else None
    lf_out = rest.pop() if logf else None
    o_ref = rest.pop()
    if side:
        side_out[...] = rest.pop()[...].astype(BF16)
    a = a_ref[...]
    if logf:
        wlf_ref, b_ref = rest

        @pl.when(pl.program_id(1) == 0)
        def _():
            lf_out[...] = _log_sigmoid(jnp.dot(a, wlf_ref[...], preferred_element_type=F32) + b_ref[...])

    acc = jnp.dot(a, w_ref[...], preferred_element_type=F32)
    if epilogue == "relu2":
        u = jnp.maximum(acc, 0.0)
        o_ref[...] = (u * u).astype(o_ref.dtype)
    else:
        o_ref[...] = acc.astype(o_ref.dtype)


def _matmul(a, w, out_dtype, epilogue="cast", tm_pref=1024, tn_pref=1024, cast_w=None, logf=None, w_cols=None):
    m, k = a.shape
    col0, n = (0, w.shape[1]) if w_cols is None else w_cols
    tm = _tile(m, tm_pref, 16)
    tn = _tile(n, tn_pref)
    assert col0 % tn == 0, (col0, tn)
    j0 = col0 // tn
    grid = (m // tm, n // tn)
    in_specs = [pl.BlockSpec((tm, k), lambda i, j: (i, 0)),
                pl.BlockSpec((k, tn), lambda i, j: (0, j0 + j))]
    args = [a, w]
    out_specs = [pl.BlockSpec((tm, tn), lambda i, j: (i, j))]
    out_shape = [jax.ShapeDtypeStruct((m, n), out_dtype)]
    if logf is not None:
        in_specs += [pl.BlockSpec((k, LANES), lambda i, j: (0, 0)), pl.BlockSpec((1, LANES), lambda i, j: (0, 0))]
        args += list(logf)
        out_specs.append(pl.BlockSpec((tm, LANES), lambda i, j: (i, 0)))
        out_shape.append(jax.ShapeDtypeStruct((m, LANES), F32))
    side = _side_cast_specs(cast_w, grid) if cast_w is not None else None
    if side:
        in_specs.append(side[0])
        args.append(cast_w)
        out_specs.append(side[0])
        out_shape.append(side[1])
    outs = list(pl.pallas_call(
        functools.partial(_mm_kernel, epilogue=epilogue, side=bool(side), logf=logf is not None),
        grid=grid,
        in_specs=in_specs,
        out_specs=out_specs,
        out_shape=out_shape,
        compiler_params=_params("arbitrary", "arbitrary"),
        name="matmul_" + epilogue,
    )(*args))
    if cast_w is not None and not side:
        outs.append(cast_w.astype(BF16))
    return outs[0] if len(outs) == 1 else tuple(outs)


def _mix_mm_kernel(o_ref, gf_ref, r_ref, w_ref, res_ref, gn_ref, out_ref, h_ref, a_scr, x_scr, *, fw):
    j = pl.program_id(1)
    nj, _, tn = x_scr.shape

    @pl.when(j == 0)
    def _():
        o = o_ref[...]
        ms = jnp.mean(o * o, axis=-1, keepdims=True)
        a_scr[:, :fw] = (o * lax.rsqrt(ms + EPS) * gf_ref[...]).astype(BF16)
        a_scr[:, fw:] = r_ref[...]

    x = res_ref[...] + jnp.dot(a_scr[...], w_ref[...], preferred_element_type=F32)
    out_ref[...] = x
    x_scr[j] = x

    @pl.when(j == nj - 1)
    def _():
        sq = x_scr[0] * x_scr[0]
        ssum = jnp.sum(sq, axis=-1, keepdims=True)
        for t in range(1, nj):
            xt = x_scr[t]
            ssum = ssum + jnp.sum(xt * xt, axis=-1, keepdims=True)
        rstd = lax.rsqrt(ssum / (nj * tn) + EPS)
        for t in range(nj):
            cols = slice(t * tn, (t + 1) * tn)
            h_ref[:, cols] = (x_scr[t] * rstd * gn_ref[:, cols]).astype(h_ref.dtype)


def _out_proj(o_fox, g_fox, r, w, res, g_next):
    m, fw = o_fox.shape
    rw = r.shape[1]
    n = w.shape[1]
    tm = _tile(m, 512, 16)
    tn = _tile(n, 1024)
    return pl.pallas_call(
        functools.partial(_mix_mm_kernel, fw=fw),
        grid=(m // tm, n // tn),
        in_specs=[pl.BlockSpec((tm, fw), lambda i, j: (i, 0)),
                  pl.BlockSpec((1, fw), lambda i, j: (0, 0)),
                  pl.BlockSpec((tm, rw), lambda i, j: (i, 0)),
                  pl.BlockSpec((fw + rw, tn), lambda i, j: (0, j)),
                  pl.BlockSpec((tm, tn), lambda i, j: (i, j)),
                  pl.BlockSpec((1, n), lambda i, j: (0, 0))],
        out_specs=[pl.BlockSpec((tm, tn), lambda i, j: (i, j)),
                   pl.BlockSpec((tm, n), lambda i, j: (i, 0))],
        out_shape=[jax.ShapeDtypeStruct((m, n), F32), jax.ShapeDtypeStruct((m, n), BF16)],
        scratch_shapes=[pltpu.VMEM((tm, fw + rw), BF16), pltpu.VMEM((n // tn, tm, tn), F32)],
        compiler_params=_params("arbitrary", "arbitrary"),
        name="out_proj",
    )(o_fox, g_fox.reshape(1, fw), r, w, res, g_next.reshape(1, n))


def _mm_acc_kernel(a_ref, w_ref, res_ref, out_ref):
    @pl.when(pl.program_id(2) == 0)
    def _():
        out_ref[...] = res_ref[...]

    out_ref[...] += jnp.dot(a_ref[...], w_ref[...], preferred_element_type=F32)


def _matmul_residual_ksplit(a, w, res):
    m, k = a.shape
    n = w.shape[1]
    tm = _tile(m, 1024, 16)
    tn = _tile(n, 1024)
    tk = _tile(k, 4096)
    return pl.pallas_call(
        _mm_acc_kernel,
        grid=(m // tm, n // tn, k // tk),
        in_specs=[pl.BlockSpec((tm, tk), lambda i, j, kk: (i, kk)),
                  pl.BlockSpec((tk, tn), lambda i, j, kk: (kk, j)),
                  pl.BlockSpec((tm, tn), lambda i, j, kk: (i, j))],
        out_specs=pl.BlockSpec((tm, tn), lambda i, j, kk: (i, j)),
        out_shape=jax.ShapeDtypeStruct((m, n), F32),
        compiler_params=_params("arbitrary", "arbitrary", "arbitrary"),
        name="down_proj",
    )(a, w, res)


def _cumsum_kernel(x_ref, o_ref, *aug_ref, bias_scale):
    x = x_ref[...]
    h, t = x.shape
    lane = lax.broadcasted_iota(jnp.int32, x.shape, 1)
    d = 1
    while d < t:
        x = x + jnp.where(lane >= d, pltpu.roll(x, d, axis=1), 0.0)
        d *= 2
    o_ref[...] = x
    if aug_ref:
        (a_ref,) = aug_ref
        y = x * bias_scale
        hi = y.astype(BF16).astype(F32)
        r1 = y - hi
        mid = r1.astype(BF16).astype(F32)
        lo = r1 - mid
        rows = a_ref.shape[0]
        base = jnp.concatenate([hi, mid, lo, jnp.zeros((rows - 3 * h, t), F32)], axis=0)
        row = lax.broadcasted_iota(jnp.int32, base.shape, 0)
        a_ref[...] = jnp.where(row == 3 * h, 1.0, base).astype(BF16)


def _cumsum_lanes(x, bias_scale=None):
    b, h, t = x.shape
    out_specs = [pl.BlockSpec((None, h, t), lambda i: (i, 0, 0))]
    out_shape = [jax.ShapeDtypeStruct((b, h, t), F32)]
    if bias_scale is not None:
        assert 3 * h < LANES, h
        out_specs.append(pl.BlockSpec((None, LANES, t), lambda i: (i, 0, 0)))
        out_shape.append(jax.ShapeDtypeStruct((b, LANES, t), BF16))
    outs = pl.pallas_call(
        functools.partial(_cumsum_kernel, bias_scale=bias_scale),
        grid=(b,),
        in_specs=[pl.BlockSpec((None, h, t), lambda i: (i, 0, 0))],
        out_specs=out_specs,
        out_shape=out_shape,
        compiler_params=_params("arbitrary"),
        name="cumsum_logf",
    )(x)
    return outs[0] if bias_scale is None else tuple(outs)


def _qk(q_bf16, k_bf16):
    return lax.dot_general(q_bf16, k_bf16, (((1,), (1,)), ((), ())), preferred_element_type=F32)


def _head_column(c_ref, h):
    c_all = c_ref[...]
    head = lax.broadcasted_iota(jnp.int32, c_all.shape, 1)
    return jnp.sum(jnp.where(head == h, c_all, 0.0), axis=1, keepdims=True)


def _fox_prompt_kernel(q_ref, k_ref, v_ref, ac_ref, ar_ref, *rest, tq, heads, side):
    if side:
        side_in, o_ref, side_out, qa_scr, kt_scr, vb_scr = rest
        side_out[...] = side_in[...].astype(BF16)
    else:
        o_ref, qa_scr, kt_scr, vb_scr = rest
    hp = vb_scr.shape[0]
    seq = q_ref.shape[0]
    r = lax.broadcasted_iota(jnp.int32, (LANES, LANES), 0)
    c = lax.broadcasted_iota(jnp.int32, (LANES, LANES), 1)
    for hh in range(hp):
        h = pl.program_id(1) * hp + hh
        cols = slice(hh * HD_FOX, (hh + 1) * HD_FOX)
        pick_q = (((r == h) & (c == 0)) | ((r == heads + h) & (c == 1)) | ((r == 2 * heads + h) & (c == 2))
                  | ((r == 3 * heads) & (c >= 3) & (c < 6)))
        pick_k = (((r == 3) & (c == h)) | ((r == 4) & (c == heads + h)) | ((r == 5) & (c == 2 * heads + h)))
        sel_q = jnp.where(pick_q, 1.0, 0.0).astype(BF16)
        sel_k = (jnp.where((r < 3) & (c == 3 * heads), 1.0, 0.0) - jnp.where(pick_k, 1.0, 0.0)).astype(BF16)
        qa_scr[hh, :, :HD_FOX] = q_ref[:, cols]
        qa_scr[hh, :, HD_FOX:] = jnp.dot(ac_ref[...], sel_q, preferred_element_type=F32).astype(BF16)
        kt_scr[hh, :HD_FOX, :] = k_ref[:, cols].T.astype(BF16)
        kt_scr[hh, HD_FOX:, :] = jnp.dot(sel_k, ar_ref[...], preferred_element_type=F32).astype(BF16)
        vb_scr[hh] = v_ref[:, cols].astype(BF16)
    c2 = (HD_FOX ** -0.5) * 1.4426950408889634
    row = lax.broadcasted_iota(jnp.int32, (tq, tq), 0)
    col = lax.broadcasted_iota(jnp.int32, (tq, tq), 1)
    causal = col <= row
    heads_here = range(hp)
    for qi in range(seq // tq):
        lo, hi = qi * tq, (qi + 1) * tq
        s_d = [jnp.where(causal, jnp.dot(qa_scr[hh, lo:hi, :], kt_scr[hh, :, lo:hi], preferred_element_type=F32),
                         -jnp.inf) for hh in heads_here]
        m = [jnp.max(s, axis=1, keepdims=True) for s in s_d]
        if qi > 0:
            s_o = [jnp.dot(qa_scr[hh, lo:hi, :], kt_scr[hh, :, :lo], preferred_element_type=F32)
                   for hh in heads_here]
            m = [jnp.maximum(mm, jnp.max(s, axis=1, keepdims=True)) for mm, s in zip(m, s_o)]
            p_o = [jnp.exp2((s - mm) * c2) for s, mm in zip(s_o, m)]
        p_d = [jnp.exp2((s - mm) * c2) for s, mm in zip(s_d, m)]
        for hh in heads_here:
            l = jnp.sum(p_d[hh], axis=1, keepdims=True)
            acc = jnp.dot(p_d[hh].astype(BF16), vb_scr[hh, lo:hi, :], preferred_element_type=F32)
            if qi > 0:
                l = l + jnp.sum(p_o[hh], axis=1, keepdims=True)
                acc = acc + jnp.dot(p_o[hh].astype(BF16), vb_scr[hh, :lo, :], preferred_element_type=F32)
            o_ref[lo:hi, hh * HD_FOX:(hh + 1) * HD_FOX] = acc / l


def _fox_prompt(q, k, v, aug_cols, aug_rows, batch, seq, heads, cast_w):
    m, fw = k.shape
    tq = _tile(seq, 256)
    hp = 2 if heads % 2 == 0 else 1
    grid = (batch, heads // hp)
    head_blk = lambda b, h: (b, h)
    in_specs = [pl.BlockSpec((seq, hp * HD_FOX), head_blk),
                pl.BlockSpec((seq, hp * HD_FOX), head_blk),
                pl.BlockSpec((seq, hp * HD_FOX), head_blk),
                pl.BlockSpec((seq, LANES), lambda b, h: (b, 0)),
                pl.BlockSpec((None, LANES, seq), lambda b, h: (b, 0, 0))]
    args = [q, k, v, aug_cols, aug_rows]
    out_specs = [pl.BlockSpec((seq, hp * HD_FOX), head_blk)]
    out_shape = [jax.ShapeDtypeStruct((m, fw), F32)]
    side = _side_cast_specs(cast_w, grid)
    if side:
        in_specs.append(side[0])
        args.append(cast_w)
        out_specs.append(side[0])
        out_shape.append(side[1])
    outs = pl.pallas_call(
        functools.partial(_fox_prompt_kernel, tq=tq, heads=heads, side=bool(side)),
        grid=grid,
        in_specs=in_specs,
        out_specs=out_specs,
        out_shape=out_shape,
        scratch_shapes=[pltpu.VMEM((hp, seq, 2 * HD_FOX), BF16), pltpu.VMEM((hp, 2 * HD_FOX, seq), BF16),
                        pltpu.VMEM((hp, seq, HD_FOX), BF16)],
        compiler_params=_params("arbitrary", "arbitrary"),
        name="fox_prompt",
    )(*args)
    return outs[0], (outs[1] if side else cast_w.astype(BF16))


def _every_eighth_row(ref, j):
    n, s, d = ref.shape
    return ref.reshape(n * s, d)[pl.ds(j, n, stride=s), :]


def _fox_sample_kernel(*refs, heads):
    ng = heads // SUBLANES
    q_ref = refs[0]
    k_refs = refs[1:1 + ng]
    v_refs = refs[1 + ng:1 + 2 * ng]
    kn_ref, vn_ref, cq_ref, ckc_ref, ckn_ref, o_ref, m_scr, l_scr, acc_scr, s_scr, p_scr = refs[1 + 2 * ng:]
    ki = pl.program_id(1)
    last = pl.num_programs(1) - 1
    scale = HD_FOX ** -0.5
    t = q_ref.shape[0]
    head_cols = lambda h: slice(h * HD_FOX, (h + 1) * HD_FOX)
    head_rows = lambda h: slice(h * t, (h + 1) * t)

    @pl.when(ki == 0)
    def _():
        m_scr[...] = jnp.full(m_scr.shape, -jnp.inf, F32)
        l_scr[...] = jnp.zeros(l_scr.shape, F32)
        acc_scr[...] = jnp.zeros(acc_scr.shape, F32)

    def cached(group_refs, h):
        return _every_eighth_row(group_refs[h // SUBLANES], h % SUBLANES).astype(BF16)

    def update(load_k, load_v, ck_ref, n, causal):
        for h in range(heads):
            s = _qk(q_ref[:, head_cols(h)], load_k(h)) * scale + cq_ref[:, h:h + 1] - ck_ref[h:h + 1, :]
            if causal:
                row = lax.broadcasted_iota(jnp.int32, s.shape, 0)
                col = lax.broadcasted_iota(jnp.int32, s.shape, 1)
                s = jnp.where(col <= row, s, -jnp.inf)
            s_scr[head_rows(h), :n] = s
        s = s_scr[:, :n]
        m_prev = m_scr[:, :1]
        m_new = jnp.maximum(m_prev, jnp.max(s, axis=1, keepdims=True))
        alpha = jnp.exp(m_prev - m_new)
        p = jnp.exp(s - m_new)
        l_new = alpha * l_scr[:, :1] + jnp.sum(p, axis=1, keepdims=True)
        m_scr[...] = jnp.broadcast_to(m_new, m_scr.shape)
        l_scr[...] = jnp.broadcast_to(l_new, l_scr.shape)
        p_scr[:, :n] = p.astype(BF16)
        for h in range(heads):
            pv = jnp.dot(p_scr[head_rows(h), :n], load_v(h), preferred_element_type=F32)
            acc_scr[:, head_cols(h)] = alpha[head_rows(h)] * acc_scr[:, head_cols(h)] + pv

    @pl.when(ki < last)
    def _():
        update(functools.partial(cached, k_refs), functools.partial(cached, v_refs), ckc_ref,
               s_scr.shape[1], causal=False)

    @pl.when(ki == last)
    def _():
        new_rows = lambda ref, h: ref[:, head_cols(h)].astype(BF16)
        update(functools.partial(new_rows, kn_ref), functools.partial(new_rows, vn_ref), ckn_ref, t, causal=True)
        for h in range(heads):
            o_ref[:, head_cols(h)] = acc_scr[:, head_cols(h)] / l_scr[head_rows(h), :1]


def _fox_sample(q, kc, vc, kn, vn, cq, ckc, ckn, batch, t_new, heads):
    past = kc.shape[1]
    fw = heads * HD_FOX
    assert heads % SUBLANES == 0, heads
    ng = heads // SUBLANES
    tk = _tile(past, 1024)
    nkc = past // tk
    kc = kc.reshape(batch, past, ng, SUBLANES, HD_FOX)
    vc = vc.reshape(batch, past, ng, SUBLANES, HD_FOX)
    row_blk = pl.BlockSpec((t_new, fw), lambda b, ki: (b, 0))
    group_blk = [pl.BlockSpec((None, tk, None, SUBLANES, HD_FOX),
                              functools.partial(lambda b, ki, g: (b, jnp.minimum(ki, nkc - 1), g, 0, 0), g=g))
                 for g in range(ng)]
    return pl.pallas_call(
        functools.partial(_fox_sample_kernel, heads=heads),
        grid=(batch, nkc + 1),
        in_specs=[row_blk] + group_blk + group_blk + [
            row_blk, row_blk,
            pl.BlockSpec((t_new, heads), lambda b, ki: (b, 0)),
            pl.BlockSpec((None, heads, tk), lambda b, ki: (b, 0, jnp.minimum(ki, nkc - 1))),
            pl.BlockSpec((None, heads, t_new), lambda b, ki: (b, 0, 0))],
        out_specs=row_blk,
        out_shape=jax.ShapeDtypeStruct((batch * t_new, fw), F32),
        scratch_shapes=[pltpu.VMEM((heads * t_new, LANES), F32), pltpu.VMEM((heads * t_new, LANES), F32),
                        pltpu.VMEM((t_new, fw), F32),
                        pltpu.VMEM((heads * t_new, tk), F32), pltpu.VMEM((heads * t_new, tk), BF16)],
        compiler_params=_params("arbitrary", "arbitrary"),
        name="fox_sample",
    )(q, *([kc] * ng), *([vc] * ng), kn, vn, cq, ckc, ckn)


def _retention_kernel(q_ref, k_ref, v_ref, g_ref, cos_ref, sin_ref, gro_ref, lg_ref, s0_ref, *rest,
                      blk, heads, side):
    if side:
        side_in, r_ref, sout_ref, side_out = rest
        side_out[...] = side_in[...].astype(BF16)
    else:
        r_ref, sout_ref = rest
    seq = q_ref.shape[0]
    half = DK_RET // 2
    ri = lax.broadcasted_iota(jnp.int32, (blk, blk), 0)
    ci = lax.broadcasted_iota(jnp.int32, (blk, blk), 1)
    dist = jnp.abs(ri - ci).astype(F32)
    visible = ci // CHUNK <= ri // CHUNK
    idx = lax.broadcasted_iota(jnp.int32, (blk, 1), 0).astype(F32)
    lgs = [lg_ref[hh, :, :1] for hh in range(heads)]
    decay = [jnp.where(visible, jnp.exp(lg * dist), 0.0) for lg in lgs]
    cross = [jnp.exp(lg * (idx + 1.0)) for lg in lgs]
    k_decay = [jnp.exp(lg * (blk - 1.0 - idx)) for lg in lgs]
    s_decay = [jnp.exp(lg * blk) for lg in lgs]
    state = [s0_ref[hh] for hh in range(heads)]
    for t in range(seq // blk):
        rows = slice(t * blk, (t + 1) * blk)
        cos = cos_ref[rows, :]
        sin = sin_ref[rows, :]
        for hh in range(heads):
            qk_cols = slice(hh * DK_RET, (hh + 1) * DK_RET)
            v_cols = slice(hh * DV_RET, (hh + 1) * DV_RET)
            q = q_ref[rows, qk_cols]
            k = k_ref[rows, qk_cols]
            q = q * cos + pltpu.roll(q, half, axis=1) * sin
            k = (k * cos + pltpu.roll(k, half, axis=1) * sin) * (DK_RET ** -0.5)
            v = v_ref[rows, v_cols]
            qb = q.astype(BF16)
            a = _qk(qb, k.astype(BF16)) * decay[hh]
            o = jnp.dot(a.astype(BF16), v, preferred_element_type=F32)
            o = o + jnp.dot(qb, state[hh].astype(BF16), preferred_element_type=F32) * cross[hh]
            state[hh] = s_decay[hh] * state[hh] + lax.dot_general(
                (k * k_decay[hh]).astype(BF16), v, (((0,), (0,)), ((), ())), preferred_element_type=F32)
            ms = jnp.mean(o * o, axis=-1, keepdims=True)
            g = g_ref[rows, v_cols]
            silu = g / (1.0 + jnp.exp(-g))
            r_ref[rows, v_cols] = (o * lax.rsqrt(ms + EPS) * gro_ref[:, v_cols] * silu).astype(r_ref.dtype)
    for hh in range(heads):
        sout_ref[hh] = state[hh]


def _retention(qkg, qv, cos, sin, g_ret_out, lg, s0, batch, seq, heads, blk, heads_per_step, fox_width,
               cast_w=None):
    m = qkg.shape[0]
    hps = heads_per_step
    groups = heads // hps
    assert fox_width % (hps * DV_RET) == 0, (fox_width, hps)
    v_off = fox_width // (hps * DV_RET)
    grid = (batch, groups)
    row = lambda b, h: (b, h)
    state = pl.BlockSpec((None, hps, DK_RET, DV_RET), lambda b, h: (b, h, 0, 0))
    in_specs = [pl.BlockSpec((seq, hps * DK_RET), row),
                pl.BlockSpec((seq, hps * DK_RET), lambda b, h: (b, groups + h)),
                pl.BlockSpec((seq, hps * DV_RET), lambda b, h: (b, v_off + h)),
                pl.BlockSpec((seq, hps * DV_RET), lambda b, h: (b, groups + h)),
                pl.BlockSpec((seq, DK_RET), lambda b, h: (0, 0)),
                pl.BlockSpec((seq, DK_RET), lambda b, h: (0, 0)),
                pl.BlockSpec((1, hps * DV_RET), lambda b, h: (0, h)),
                pl.BlockSpec((hps, 1, LANES), lambda b, h: (h, 0, 0)),
                state]
    args = [qkg, qkg, qv, qkg, cos, sin, g_ret_out.reshape(1, -1), lg, s0]
    out_specs = [pl.BlockSpec((seq, hps * DV_RET), row), state]
    out_shape = [jax.ShapeDtypeStruct((m, heads * DV_RET), BF16),
                 jax.ShapeDtypeStruct((batch, heads, DK_RET, DV_RET), F32)]
    side = _side_cast_specs(cast_w, grid) if cast_w is not None else None
    if side:
        in_specs.append(side[0])
        args.append(cast_w)
        out_specs.append(side[0])
        out_shape.append(side[1])
    outs = pl.pallas_call(
        functools.partial(_retention_kernel, blk=blk, heads=hps, side=bool(side)),
        grid=grid,
        in_specs=in_specs,
        out_specs=out_specs,
        out_shape=out_shape,
        compiler_params=_params("arbitrary", "arbitrary"),
        name="retention",
    )(*args)
    if cast_w is None:
        return outs[0], outs[1]
    return outs[0], outs[1], (outs[2] if side else cast_w.astype(BF16))


def _rotary_tables(pos):
    half = DK_RET // 2
    inv = ROPE_BASE ** (-jnp.arange(half, dtype=F32) / half)
    ang = pos.astype(F32)[:, None] * inv[None, :]
    cos = jnp.cos(ang)
    sin = jnp.sin(ang)
    return jnp.concatenate([cos, cos], axis=-1), jnp.concatenate([-sin, sin], axis=-1)


def _feature_rows_kernel(w_ref, o_ref, *, valid):
    x = w_ref[...]
    if valid < x.shape[0]:
        row = lax.broadcasted_iota(jnp.int32, x.shape, 0)
        x = jnp.where(row < valid, x, 0.0)
    o_ref[...] = x.T.astype(BF16)


def _feature_rows(w_t, layer, segments, rows_per_step, valid=None):
    _, _, d = w_t.shape
    rb = rows_per_step
    starts = [s + k * rb for s, n in segments for k in range(n // rb)]
    assert all(n % rb == 0 for _, n in segments) and all(s % SUBLANES == 0 for s in starts), segments

    def index_map(i):
        tile_row = starts[-1] // SUBLANES
        for k in range(len(starts) - 2, -1, -1):
            tile_row = jnp.where(i <= k, starts[k] // SUBLANES, tile_row)
        return (layer, tile_row * SUBLANES, 0)

    return pl.pallas_call(
        functools.partial(_feature_rows_kernel, valid=rb if valid is None else valid),
        grid=(len(starts),),
        in_specs=[pl.BlockSpec((None, pl.Element(rb), pl.Element(d)), index_map)],
        out_specs=pl.BlockSpec((d, rb), lambda i: (0, i)),
        out_shape=jax.ShapeDtypeStruct((d, rb * len(starts)), BF16),
        compiler_params=_params("arbitrary"),
        name="w_in_rows",
    )(w_t)


def _in_proj_weights(w_in, layer, b_forget, h_fox, h_ret):
    fw = h_fox * HD_FOX
    qkw = h_ret * DK_RET
    vw = h_ret * DV_RET
    o1, o2, o3 = fw, 2 * fw, 3 * fw
    o4 = o3 + h_fox
    o6 = o4 + 2 * qkw
    o7 = o6 + vw
    w_t = jnp.swapaxes(w_in, 1, 2)
    groups = dict(qv=[(0, fw), (o6, vw)], k_f=[(o1, fw)], v_f=[(o2, fw)], qkg=[(o4, 2 * qkw), (o7, vw)])
    cols, col0 = {}, 0
    for name, segs in groups.items():
        width = sum(n for _, n in segs)
        cols[name] = (col0, width)
        col0 += width
    return dict(
        w=_feature_rows(w_t, layer, [seg for segs in groups.values() for seg in segs], 512), cols=cols,
        logf=_feature_rows(w_t, layer, [(o3, LANES)], LANES, valid=h_fox),
        b_logf=jnp.pad(b_forget, (0, LANES - h_fox)).reshape(1, LANES).astype(F32))


def _project(x2d, g_attn, w):
    h = _rmsnorm(x2d, g_attn, BF16)
    mm = lambda name, dtype, **kw: _matmul(h, w["w"], dtype, w_cols=w["cols"][name], **kw)
    qkg, logf = mm("qkg", F32, logf=(w["logf"], w["b_logf"]))
    return dict(qv=mm("qv", BF16), k_f=mm("k_f", F32), v_f=mm("v_f", F32), qkg=qkg, logf=logf)


def _finish_layer(x2d, o_fox, r, g_fox_out, g_mlp, w_out_b, w_up_b, w_down):
    x1, h2 = _out_proj(o_fox, g_fox_out, r, w_out_b, x2d, g_mlp)
    if w_down.dtype == BF16:
        u = _matmul(h2, w_up_b, BF16, epilogue="relu2")
    else:
        u, w_down = _matmul(h2, w_up_b, BF16, epilogue="relu2", cast_w=w_down)
    return _matmul_residual_ksplit(u, w_down, x1), w_down


def _retention_log_gamma(h_ret):
    lg = jnp.log1p(-jnp.exp2(-5.0 - jnp.arange(h_ret, dtype=F32)))
    return jnp.broadcast_to(lg[:, None, None], (h_ret, 1, LANES))


def kernel(x_prompt, x_sample, cache_fox_k, cache_fox_v, cache_fox_logf, state_ret,
           g_attn, w_in, b_forget, g_fox_out, g_ret_out, w_out, g_mlp, w_up, w_down, g_final):
    bp, seq, d = x_prompt.shape
    bs, t_new, _ = x_sample.shape
    depth, _, past, h_fox, _ = cache_fox_k.shape
    h_ret = state_ret.shape[2]
    fw = h_fox * HD_FOX
    dtype = x_prompt.dtype

    lg = _retention_log_gamma(h_ret)
    cos_p, sin_p = _rotary_tables(jnp.arange(seq))
    cos_s, sin_s = _rotary_tables(past + jnp.arange(t_new))
    blk_p = _tile(seq, 256, CHUNK)

    yp = x_prompt.reshape(bp * seq, d)
    ys = x_sample.reshape(bs * t_new, d)
    outs = [[] for _ in range(8)]
    for l in range(depth):
        w = _in_proj_weights(w_in, l, b_forget[l], h_fox, h_ret)

        p = _project(yp, g_attn[l], w)
        logf_p = p["logf"][:, :h_fox]
        _, aug_rows = _cumsum_lanes(jnp.swapaxes(logf_p.reshape(bp, seq, h_fox), 1, 2),
                                    bias_scale=HD_FOX ** 0.5)
        aug_cols = jnp.swapaxes(aug_rows, 1, 2).reshape(bp * seq, LANES)
        o_fox, w_up_b = _fox_prompt(p["qv"], p["k_f"], p["v_f"], aug_cols, aug_rows, bp, seq, h_fox, w_up[l])
        s0 = jnp.zeros((bp, h_ret, DK_RET, DV_RET), F32)
        r, s_fin, w_out_b = _retention(p["qkg"], p["qv"], cos_p, sin_p, g_ret_out[l], lg,
                                       s0, bp, seq, h_ret, blk_p, 2 if h_ret % 2 == 0 else 1, fw,
                                       cast_w=w_out[l])
        yp, w_down_b = _finish_layer(yp, o_fox, r, g_fox_out[l], g_mlp[l], w_out_b, w_up_b, w_down[l])
        outs[0].append(p["k_f"].reshape(bp, seq, h_fox, HD_FOX))
        outs[1].append(p["v_f"].reshape(bp, seq, h_fox, HD_FOX))
        outs[2].append(logf_p.reshape(bp, seq, h_fox).astype(dtype))
        outs[3].append(s_fin.astype(dtype))

        p = _project(ys, g_attn[l], w)
        logf_s = p["logf"][:, :h_fox]
        total = past + t_new
        padded = -(-total // LANES) * LANES
        logf_all = jnp.concatenate(
            [jnp.swapaxes(cache_fox_logf[l].astype(F32), 1, 2),
             jnp.swapaxes(logf_s.reshape(bs, t_new, h_fox), 1, 2),
             jnp.zeros((bs, h_fox, padded - total), F32)], axis=2)
        c_all = _cumsum_lanes(logf_all)
        ckc = c_all[:, :, :past]
        ckn = c_all[:, :, past:total]
        cq = jnp.swapaxes(ckn, 1, 2).reshape(bs * t_new, h_fox)
        o_fox = _fox_sample(p["qv"], cache_fox_k[l], cache_fox_v[l], p["k_f"], p["v_f"],
                            cq, ckc, ckn, bs, t_new, h_fox)
        r, s_new = _retention(p["qkg"], p["qv"], cos_s, sin_s, g_ret_out[l], lg,
                              state_ret[l].astype(F32), bs, t_new, h_ret, t_new, h_ret, fw)
        ys, _ = _finish_layer(ys, o_fox, r, g_fox_out[l], g_mlp[l], w_out_b, w_up_b, w_down_b)
        outs[4].append(p["k_f"].reshape(bs, t_new, h_fox, HD_FOX))
        outs[5].append(p["v_f"].reshape(bs, t_new, h_fox, HD_FOX))
        outs[6].append(logf_s.reshape(bs, t_new, h_fox).astype(dtype))
        outs[7].append(s_new.astype(dtype))

    y_prompt = _rmsnorm(yp, g_final, dtype).reshape(bp, seq, d)
    y_sample = _rmsnorm(ys, g_final, dtype).reshape(bs, t_new, d)
    return (y_prompt, y_sample) + tuple(jnp.stack(o) for o in outs)
```

```python
import functools

import jax
import jax.numpy as jnp
from jax import lax
from jax.experimental import pallas as pl
from jax.experimental.pallas import tpu as pltpu

CHUNK = 64
HD_FOX = 128
DK_RET = 128
DV_RET = 256
ROPE_BASE = 10000.0
EPS = 1e-6

LANES = 128
SUBLANES = 8
VMEM_LIMIT_BYTES = 62 * 1024 * 1024

F32 = jnp.float32
BF16 = jnp.bfloat16


def _params(*sem):
    return pltpu.CompilerParams(dimension_semantics=sem, vmem_limit_bytes=VMEM_LIMIT_BYTES)


def _tile(n, pref, mult=LANES):
    if n <= pref:
        return n
    t = (pref // mult) * mult
    while t > mult and n % t:
        t -= mult
    assert n % t == 0, (n, pref)
    return t


def _rms_kernel(x_ref, g_ref, o_ref):
    x = x_ref[...]
    ms = jnp.mean(x * x, axis=-1, keepdims=True)
    o_ref[...] = (x * lax.rsqrt(ms + EPS) * g_ref[...]).astype(o_ref.dtype)


def _rmsnorm(x, g, out_dtype):
    m, d = x.shape
    tr = _tile(m, 256, 8)
    return pl.pallas_call(
        _rms_kernel,
        grid=(m // tr,),
        in_specs=[pl.BlockSpec((tr, d), lambda i: (i, 0)),
                  pl.BlockSpec((1, d), lambda i: (0, 0))],
        out_specs=pl.BlockSpec((tr, d), lambda i: (i, 0)),
        out_shape=jax.ShapeDtypeStruct((m, d), out_dtype),
        compiler_params=_params("arbitrary"),
        name="rmsnorm",
    )(x, g.reshape(1, d))


def _log_sigmoid(z):
    return jnp.minimum(z, 0.0) - jnp.log(1.0 + jnp.exp(-jnp.abs(z)))


def _side_cast_specs(w, grid):
    steps = 1
    for g in grid:
        steps *= g
    rows, cols = w.shape
    if rows % steps or (rows // steps) % 16:
        return None

    def index_map(*idx):
        lin = idx[0]
        for g, i in zip(grid[1:], idx[1:]):
            lin = lin * g + i
        return (lin, 0)

    spec = pl.BlockSpec((rows // steps, cols), index_map)
    return spec, jax.ShapeDtypeStruct((rows, cols), BF16)


def _mm_kernel(a_ref, w_ref, *rest, epilogue, side, logf):
    rest = list(rest)
    side_out = rest.pop() if side else None
    lf_out = rest.pop() if logf else None
    o_ref = rest.pop()
    if side:
        side_out[...] = rest.pop()[...].astype(BF16)
    a = a_ref[...]
    if logf:
        wlf_ref, b_ref = rest

        @pl.when(pl.program_id(1) == 0)
        def _():
            lf_out[...] = _log_sigmoid(jnp.dot(a, wlf_ref[...], preferred_element_type=F32) + b_ref[...])

    acc = jnp.dot(a, w_ref[...], preferred_element_type=F32)
    if epilogue == "relu2":
        u = jnp.maximum(acc, 0.0)
        o_ref[...] = (u * u).astype(o_ref.dtype)
    else:
        o_ref[...] = acc.astype(o_ref.dtype)


def _matmul(a, w, out_dtype, epilogue="cast", tm_pref=1024, tn_pref=1024, cast_w=None, logf=None, w_cols=None):
    m, k = a.shape
    col0, n = (0, w.shape[1]) if w_cols is None else w_cols
    tm = _tile(m, tm_pref, 16)
    tn = _tile(n, tn_pref)
    assert col0 % tn == 0, (col0, tn)
    j0 = col0 // tn
    grid = (m // tm, n // tn)
    in_specs = [pl.BlockSpec((tm, k), lambda i, j: (i, 0)),
                pl.BlockSpec((k, tn), lambda i, j: (0, j0 + j))]
    args = [a, w]
    out_specs = [pl.BlockSpec((tm, tn), lambda i, j: (i, j))]
    out_shape = [jax.ShapeDtypeStruct((m, n), out_dtype)]
    if logf is not None:
        in_specs += [pl.BlockSpec((k, LANES), lambda i, j: (0, 0)), pl.BlockSpec((1, LANES), lambda i, j: (0, 0))]
        args += list(logf)
        out_specs.append(pl.BlockSpec((tm, LANES), lambda i, j: (i, 0)))
        out_shape.append(jax.ShapeDtypeStruct((m, LANES), F32))
    side = _side_cast_specs(cast_w, grid) if cast_w is not None else None
    if side:
        in_specs.append(side[0])
        args.append(cast_w)
        out_specs.append(side[0])
        out_shape.append(side[1])
    outs = list(pl.pallas_call(
        functools.partial(_mm_kernel, epilogue=epilogue, side=bool(side), logf=logf is not None),
        grid=grid,
        in_specs=in_specs,
        out_specs=out_specs,
        out_shape=out_shape,
        compiler_params=_params("arbitrary", "arbitrary"),
        name="matmul_" + epilogue,
    )(*args))
    if cast_w is not None and not side:
        outs.append(cast_w.astype(BF16))
    return outs[0] if len(outs) == 1 else tuple(outs)


def _mix_mm_kernel(o_ref, gf_ref, r_ref, w_ref, res_ref, gn_ref, out_ref, h_ref, a_scr, x_scr, *, fw):
    j = pl.program_id(1)
    nj, _, tn = x_scr.shape

    @pl.when(j == 0)
    def _():
        o = o_ref[...]
        ms = jnp.mean(o * o, axis=-1, keepdims=True)
        a_scr[:, :fw] = (o * lax.rsqrt(ms + EPS) * gf_ref[...]).astype(BF16)
        a_scr[:, fw:] = r_ref[...]

    x = res_ref[...] + jnp.dot(a_scr[...], w_ref[...], preferred_element_type=F32)
    out_ref[...] = x
    x_scr[j] = x

    @pl.when(j == nj - 1)
    def _():
        sq = x_scr[0] * x_scr[0]
        ssum = jnp.sum(sq, axis=-1, keepdims=True)
        for t in range(1, nj):
            xt = x_scr[t]
            ssum = ssum + jnp.sum(xt * xt, axis=-1, keepdims=True)
        rstd = lax.rsqrt(ssum / (nj * tn) + EPS)
        for t in range(nj):
            cols = slice(t * tn, (t + 1) * tn)
            h_ref[:, cols] = (x_scr[t] * rstd * gn_ref[:, cols]).astype(h_ref.dtype)


def _out_proj(o_fox, g_fox, r, w, res, g_next):
    m, fw = o_fox.shape
    rw = r.shape[1]
    n = w.shape[1]
    tm = _tile(m, 512, 16)
    tn = _tile(n, 1024)
    return pl.pallas_call(
        functools.partial(_mix_mm_kernel, fw=fw),
        grid=(m // tm, n // tn),
        in_specs=[pl.BlockSpec((tm, fw), lambda i, j: (i, 0)),
                  pl.BlockSpec((1, fw), lambda i, j: (0, 0)),
                  pl.BlockSpec((tm, rw), lambda i, j: (i, 0)),
                  pl.BlockSpec((fw + rw, tn), lambda i, j: (0, j)),
                  pl.BlockSpec((tm, tn), lambda i, j: (i, j)),
                  pl.BlockSpec((1, n), lambda i, j: (0, 0))],
        out_specs=[pl.BlockSpec((tm, tn), lambda i, j: (i, j)),
                   pl.BlockSpec((tm, n), lambda i, j: (i, 0))],
        out_shape=[jax.ShapeDtypeStruct((m, n), F32), jax.ShapeDtypeStruct((m, n), BF16)],
        scratch_shapes=[pltpu.VMEM((tm, fw + rw), BF16), pltpu.VMEM((n // tn, tm, tn), F32)],
        compiler_params=_params("arbitrary", "arbitrary"),
        name="out_proj",
    )(o_fox, g_fox.reshape(1, fw), r, w, res, g_next.reshape(1, n))


def _mm_acc_kernel(a_ref, w_ref, res_ref, out_ref):
    @pl.when(pl.program_id(2) == 0)
    def _():
        out_ref[...] = res_ref[...]

    out_ref[...] += jnp.dot(a_ref[...], w_ref[...], preferred_element_type=F32)


def _matmul_residual_ksplit(a, w, res):
    m, k = a.shape
    n = w.shape[1]
    tm = _tile(m, 1024, 16)
    tn = _tile(n, 1024)
    tk = _tile(k, 4096)
    return pl.pallas_call(
        _mm_acc_kernel,
        grid=(m // tm, n // tn, k // tk),
        in_specs=[pl.BlockSpec((tm, tk), lambda i, j, kk: (i, kk)),
                  pl.BlockSpec((tk, tn), lambda i, j, kk: (kk, j)),
                  pl.BlockSpec((tm, tn), lambda i, j, kk: (i, j))],
        out_specs=pl.BlockSpec((tm, tn), lambda i, j, kk: (i, j)),
        out_shape=jax.ShapeDtypeStruct((m, n), F32),
        compiler_params=_params("arbitrary", "arbitrary", "arbitrary"),
        name="down_proj",
    )(a, w, res)


def _cumsum_kernel(x_ref, o_ref, *aug_ref, bias_scale):
    x = x_ref[...]
    h, t = x.shape
    lane = lax.broadcasted_iota(jnp.int32, x.shape, 1)
    d = 1
    while d < t:
        x = x + jnp.where(lane >= d, pltpu.roll(x, d, axis=1), 0.0)
        d *= 2
    o_ref[...] = x
    if aug_ref:
        (a_ref,) = aug_ref
        y = x * bias_scale
        hi = y.astype(BF16).astype(F32)
        r1 = y - hi
        mid = r1.astype(BF16).astype(F32)
        lo = r1 - mid
        rows = a_ref.shape[0]
        base = jnp.concatenate([hi, mid, lo, jnp.zeros((rows - 3 * h, t), F32)], axis=0)
        row = lax.broadcasted_iota(jnp.int32, base.shape, 0)
        a_ref[...] = jnp.where(row == 3 * h, 1.0, base).astype(BF16)


def _cumsum_lanes(x, bias_scale=None):
    b, h, t = x.shape
    out_specs = [pl.BlockSpec((None, h, t), lambda i: (i, 0, 0))]
    out_shape = [jax.ShapeDtypeStruct((b, h, t), F32)]
    if bias_scale is not None:
        assert 3 * h < LANES, h
        out_specs.append(pl.BlockSpec((None, LANES, t), lambda i: (i, 0, 0)))
        out_shape.append(jax.ShapeDtypeStruct((b, LANES, t), BF16))
    outs = pl.pallas_call(
        functools.partial(_cumsum_kernel, bias_scale=bias_scale),
        grid=(b,),
        in_specs=[pl.BlockSpec((None, h, t), lambda i: (i, 0, 0))],
        out_specs=out_specs,
        out_shape=out_shape,
        compiler_params=_params("arbitrary"),
        name="cumsum_logf",
    )(x)
    return outs[0] if bias_scale is None else tuple(outs)


def _qk(q_bf16, k_bf16):
    return lax.dot_general(q_bf16, k_bf16, (((1,), (1,)), ((), ())), preferred_element_type=F32)


def _head_column(c_ref, h):
    c_all = c_ref[...]
    head = lax.broadcasted_iota(jnp.int32, c_all.shape, 1)
    return jnp.sum(jnp.where(head == h, c_all, 0.0), axis=1, keepdims=True)


def _fox_prompt_kernel(q_ref, k_ref, v_ref, ac_ref, ar_ref, *rest, tq, heads, side):
    if side:
        side_in, o_ref, side_out, qa_scr, kt_scr, vb_scr = rest
        side_out[...] = side_in[...].astype(BF16)
    else:
        o_ref, qa_scr, kt_scr, vb_scr = rest
    hp = vb_scr.shape[0]
    seq = q_ref.shape[0]
    r = lax.broadcasted_iota(jnp.int32, (LANES, LANES), 0)
    c = lax.broadcasted_iota(jnp.int32, (LANES, LANES), 1)
    for hh in range(hp):
        h = pl.program_id(1) * hp + hh
        cols = slice(hh * HD_FOX, (hh + 1) * HD_FOX)
        pick_q = (((r == h) & (c == 0)) | ((r == heads + h) & (c == 1)) | ((r == 2 * heads + h) & (c == 2))
                  | ((r == 3 * heads) & (c >= 3) & (c < 6)))
        pick_k = (((r == 3) & (c == h)) | ((r == 4) & (c == heads + h)) | ((r == 5) & (c == 2 * heads + h)))
        sel_q = jnp.where(pick_q, 1.0, 0.0).astype(BF16)
        sel_k = (jnp.where((r < 3) & (c == 3 * heads), 1.0, 0.0) - jnp.where(pick_k, 1.0, 0.0)).astype(BF16)
        qa_scr[hh, :, :HD_FOX] = q_ref[:, cols]
        qa_scr[hh, :, HD_FOX:] = jnp.dot(ac_ref[...], sel_q, preferred_element_type=F32).astype(BF16)
        kt_scr[hh, :HD_FOX, :] = k_ref[:, cols].T.astype(BF16)
        kt_scr[hh, HD_FOX:, :] = jnp.dot(sel_k, ar_ref[...], preferred_element_type=F32).astype(BF16)
        vb_scr[hh] = v_ref[:, cols].astype(BF16)
    c2 = (HD_FOX ** -0.5) * 1.4426950408889634
    row = lax.broadcasted_iota(jnp.int32, (tq, tq), 0)
    col = lax.broadcasted_iota(jnp.int32, (tq, tq), 1)
    causal = col <= row
    nq = seq // tq
    for pair in range((nq + 1) // 2):
        blocks = sorted({pair, nq - 1 - pair})
        chains = [(hh, qi) for qi in blocks for hh in range(hp)]
        qa = {c: qa_scr[c[0], c[1] * tq:(c[1] + 1) * tq, :] for c in chains}
        s_d = {(hh, qi): jnp.where(causal, jnp.dot(qa[hh, qi], kt_scr[hh, :, qi * tq:(qi + 1) * tq],
                                                   preferred_element_type=F32), -jnp.inf) for hh, qi in chains}
        m = {c: jnp.max(s_d[c], axis=1, keepdims=True) for c in chains}
        early = [c for c in chains if c[1] > 0]
        s_o = {(hh, qi): jnp.dot(qa[hh, qi], kt_scr[hh, :, :qi * tq], preferred_element_type=F32)
               for hh, qi in early}
        for c in early:
            m[c] = jnp.maximum(m[c], jnp.max(s_o[c], axis=1, keepdims=True))
        p_o = {c: jnp.exp2((s_o[c] - m[c]) * c2) for c in early}
        p_d = {c: jnp.exp2((s_d[c] - m[c]) * c2) for c in chains}
        for hh, qi in chains:
            lo, hi = qi * tq, (qi + 1) * tq
            l = jnp.sum(p_d[hh, qi], axis=1, keepdims=True)
            acc = jnp.dot(p_d[hh, qi].astype(BF16), vb_scr[hh, lo:hi, :], preferred_element_type=F32)
            if qi > 0:
                l = l + jnp.sum(p_o[hh, qi], axis=1, keepdims=True)
                acc = acc + jnp.dot(p_o[hh, qi].astype(BF16), vb_scr[hh, :lo, :], preferred_element_type=F32)
            o_ref[lo:hi, hh * HD_FOX:(hh + 1) * HD_FOX] = acc / l


def _fox_prompt(q, k, v, aug_cols, aug_rows, batch, seq, heads, cast_w):
    m, fw = k.shape
    tq = _tile(seq, 256)
    hp = 2 if heads % 2 == 0 else 1
    grid = (batch, heads // hp)
    head_blk = lambda b, h: (b, h)
    in_specs = [pl.BlockSpec((seq, hp * HD_FOX), head_blk),
                pl.BlockSpec((seq, hp * HD_FOX), head_blk),
                pl.BlockSpec((seq, hp * HD_FOX), head_blk),
                pl.BlockSpec((seq, LANES), lambda b, h: (b, 0)),
                pl.BlockSpec((None, LANES, seq), lambda b, h: (b, 0, 0))]
    args = [q, k, v, aug_cols, aug_rows]
    out_specs = [pl.BlockSpec((seq, hp * HD_FOX), head_blk)]
    out_shape = [jax.ShapeDtypeStruct((m, fw), F32)]
    side = _side_cast_specs(cast_w, grid)
    if side:
        in_specs.append(side[0])
        args.append(cast_w)
        out_specs.append(side[0])
        out_shape.append(side[1])
    outs = pl.pallas_call(
        functools.partial(_fox_prompt_kernel, tq=tq, heads=heads, side=bool(side)),
        grid=grid,
        in_specs=in_specs,
        out_specs=out_specs,
        out_shape=out_shape,
        scratch_shapes=[pltpu.VMEM((hp, seq, 2 * HD_FOX), BF16), pltpu.VMEM((hp, 2 * HD_FOX, seq), BF16),
                        pltpu.VMEM((hp, seq, HD_FOX), BF16)],
        compiler_params=_params("arbitrary", "arbitrary"),
        name="fox_prompt",
    )(*args)
    return outs[0], (outs[1] if side else cast_w.astype(BF16))


def _every_eighth_row(ref, j):
    n, s, d = ref.shape
    return ref.reshape(n * s, d)[pl.ds(j, n, stride=s), :]


def _fox_sample_kernel(*refs, heads):
    ng = heads // SUBLANES
    q_ref = refs[0]
    k_refs = refs[1:1 + ng]
    v_refs = refs[1 + ng:1 + 2 * ng]
    kn_ref, vn_ref, cq_ref, ckc_ref, ckn_ref, o_ref, m_scr, l_scr, acc_scr, s_scr, p_scr = refs[1 + 2 * ng:]
    ki = pl.program_id(1)
    last = pl.num_programs(1) - 1
    scale = HD_FOX ** -0.5
    t = q_ref.shape[0]
    head_cols = lambda h: slice(h * HD_FOX, (h + 1) * HD_FOX)
    head_rows = lambda h: slice(h * t, (h + 1) * t)

    @pl.when(ki == 0)
    def _():
        m_scr[...] = jnp.full(m_scr.shape, -jnp.inf, F32)
        l_scr[...] = jnp.zeros(l_scr.shape, F32)
        acc_scr[...] = jnp.zeros(acc_scr.shape, F32)

    def cached(group_refs, h):
        return _every_eighth_row(group_refs[h // SUBLANES], h % SUBLANES).astype(BF16)

    def update(load_k, load_v, ck_ref, n, causal):
        for h in range(heads):
            s = _qk(q_ref[:, head_cols(h)], load_k(h)) * scale + cq_ref[:, h:h + 1] - ck_ref[h:h + 1, :]
            if causal:
                row = lax.broadcasted_iota(jnp.int32, s.shape, 0)
                col = lax.broadcasted_iota(jnp.int32, s.shape, 1)
                s = jnp.where(col <= row, s, -jnp.inf)
            s_scr[head_rows(h), :n] = s
        s = s_scr[:, :n]
        m_prev = m_scr[:, :1]
        m_new = jnp.maximum(m_prev, jnp.max(s, axis=1, keepdims=True))
        alpha = jnp.exp(m_prev - m_new)
        p = jnp.exp(s - m_new)
        l_new = alpha * l_scr[:, :1] + jnp.sum(p, axis=1, keepdims=True)
        m_scr[...] = jnp.broadcast_to(m_new, m_scr.shape)
        l_scr[...] = jnp.broadcast_to(l_new, l_scr.shape)
        p_scr[:, :n] = p.astype(BF16)
        for h in range(heads):
            pv = jnp.dot(p_scr[head_rows(h), :n], load_v(h), preferred_element_type=F32)
            acc_scr[:, head_cols(h)] = alpha[head_rows(h)] * acc_scr[:, head_cols(h)] + pv

    @pl.when(ki < last)
    def _():
        update(functools.partial(cached, k_refs), functools.partial(cached, v_refs), ckc_ref,
               s_scr.shape[1], causal=False)

    @pl.when(ki == last)
    def _():
        new_rows = lambda ref, h: ref[:, head_cols(h)].astype(BF16)
        update(functools.partial(new_rows, kn_ref), functools.partial(new_rows, vn_ref), ckn_ref, t, causal=True)
        for h in range(heads):
            o_ref[:, head_cols(h)] = acc_scr[:, head_cols(h)] / l_scr[head_rows(h), :1]


def _fox_sample(q, kc, vc, kn, vn, cq, ckc, ckn, batch, t_new, heads):
    past = kc.shape[1]
    fw = heads * HD_FOX
    assert heads % SUBLANES == 0, heads
    ng = heads // SUBLANES
    tk = _tile(past, 1024)
    nkc = past // tk
    kc = kc.reshape(batch, past, ng, SUBLANES, HD_FOX)
    vc = vc.reshape(batch, past, ng, SUBLANES, HD_FOX)
    row_blk = pl.BlockSpec((t_new, fw), lambda b, ki: (b, 0))
    group_blk = [pl.BlockSpec((None, tk, None, SUBLANES, HD_FOX),
                              functools.partial(lambda b, ki, g: (b, jnp.minimum(ki, nkc - 1), g, 0, 0), g=g))
                 for g in range(ng)]
    return pl.pallas_call(
        functools.partial(_fox_sample_kernel, heads=heads),
        grid=(batch, nkc + 1),
        in_specs=[row_blk] + group_blk + group_blk + [
            row_blk, row_blk,
            pl.BlockSpec((t_new, heads), lambda b, ki: (b, 0)),
            pl.BlockSpec((None, heads, tk), lambda b, ki: (b, 0, jnp.minimum(ki, nkc - 1))),
            pl.BlockSpec((None, heads, t_new), lambda b, ki: (b, 0, 0))],
        out_specs=row_blk,
        out_shape=jax.ShapeDtypeStruct((batch * t_new, fw), F32),
        scratch_shapes=[pltpu.VMEM((heads * t_new, LANES), F32), pltpu.VMEM((heads * t_new, LANES), F32),
                        pltpu.VMEM((t_new, fw), F32),
                        pltpu.VMEM((heads * t_new, tk), F32), pltpu.VMEM((heads * t_new, tk), BF16)],
        compiler_params=_params("arbitrary", "arbitrary"),
        name="fox_sample",
    )(q, *([kc] * ng), *([vc] * ng), kn, vn, cq, ckc, ckn)


def _retention_kernel(q_ref, k_ref, v_ref, g_ref, cos_ref, sin_ref, gro_ref, lg_ref, s0_ref, *rest,
                      blk, heads, side):
    if side:
        side_in, r_ref, sout_ref, side_out = rest
        side_out[...] = side_in[...].astype(BF16)
    else:
        r_ref, sout_ref = rest
    seq = q_ref.shape[0]
    half = DK_RET // 2
    ri = lax.broadcasted_iota(jnp.int32, (blk, blk), 0)
    ci = lax.broadcasted_iota(jnp.int32, (blk, blk), 1)
    dist = jnp.abs(ri - ci).astype(F32)
    visible = ci // CHUNK <= ri // CHUNK
    idx = lax.broadcasted_iota(jnp.int32, (blk, 1), 0).astype(F32)
    lgs = [lg_ref[hh, :, :1] for hh in range(heads)]
    decay = [jnp.where(visible, jnp.exp(lg * dist), 0.0) for lg in lgs]
    cross = [jnp.exp(lg * (idx + 1.0)) for lg in lgs]
    k_decay = [jnp.exp(lg * (blk - 1.0 - idx)) for lg in lgs]
    s_decay = [jnp.exp(lg * blk) for lg in lgs]
    state = [s0_ref[hh] for hh in range(heads)]
    for t in range(seq // blk):
        rows = slice(t * blk, (t + 1) * blk)
        cos = cos_ref[rows, :]
        sin = sin_ref[rows, :]
        for hh in range(heads):
            qk_cols = slice(hh * DK_RET, (hh + 1) * DK_RET)
            v_cols = slice(hh * DV_RET, (hh + 1) * DV_RET)
            q = q_ref[rows, qk_cols]
            k = k_ref[rows, qk_cols]
            q = q * cos + pltpu.roll(q, half, axis=1) * sin
            k = (k * cos + pltpu.roll(k, half, axis=1) * sin) * (DK_RET ** -0.5)
            v = v_ref[rows, v_cols]
            qb = q.astype(BF16)
            a = _qk(qb, k.astype(BF16)) * decay[hh]
            o = jnp.dot(a.astype(BF16), v, preferred_element_type=F32)
            o = o + jnp.dot(qb, state[hh].astype(BF16), preferred_element_type=F32) * cross[hh]
            state[hh] = s_decay[hh] * state[hh] + lax.dot_general(
                (k * k_decay[hh]).astype(BF16), v, (((0,), (0,)), ((), ())), preferred_element_type=F32)
            ms = jnp.mean(o * o, axis=-1, keepdims=True)
            g = g_ref[rows, v_cols]
            silu = g / (1.0 + jnp.exp(-g))
            r_ref[rows, v_cols] = (o * lax.rsqrt(ms + EPS) * gro_ref[:, v_cols] * silu).astype(r_ref.dtype)
    for hh in range(heads):
        sout_ref[hh] = state[hh]


def _retention(qkg, qv, cos, sin, g_ret_out, lg, s0, batch, seq, heads, blk, heads_per_step, fox_width,
               cast_w=None):
    m = qkg.shape[0]
    hps = heads_per_step
    groups = heads // hps
    assert fox_width % (hps * DV_RET) == 0, (fox_width, hps)
    v_off = fox_width // (hps * DV_RET)
    grid = (batch, groups)
    row = lambda b, h: (b, h)
    state = pl.BlockSpec((None, hps, DK_RET, DV_RET), lambda b, h: (b, h, 0, 0))
    in_specs = [pl.BlockSpec((seq, hps * DK_RET), row),
                pl.BlockSpec((seq, hps * DK_RET), lambda b, h: (b, groups + h)),
                pl.BlockSpec((seq, hps * DV_RET), lambda b, h: (b, v_off + h)),
                pl.BlockSpec((seq, hps * DV_RET), lambda b, h: (b, groups + h)),
                pl.BlockSpec((seq, DK_RET), lambda b, h: (0, 0)),
                pl.BlockSpec((seq, DK_RET), lambda b, h: (0, 0)),
                pl.BlockSpec((1, hps * DV_RET), lambda b, h: (0, h)),
                pl.BlockSpec((hps, 1, LANES), lambda b, h: (h, 0, 0)),
                state]
    args = [qkg, qkg, qv, qkg, cos, sin, g_ret_out.reshape(1, -1), lg, s0]
    out_specs = [pl.BlockSpec((seq, hps * DV_RET), row), state]
    out_shape = [jax.ShapeDtypeStruct((m, heads * DV_RET), BF16),
                 jax.ShapeDtypeStruct((batch, heads, DK_RET, DV_RET), F32)]
    side = _side_cast_specs(cast_w, grid) if cast_w is not None else None
    if side:
        in_specs.append(side[0])
        args.append(cast_w)
        out_specs.append(side[0])
        out_shape.append(side[1])
    outs = pl.pallas_call(
        functools.partial(_retention_kernel, blk=blk, heads=hps, side=bool(side)),
        grid=grid,
        in_specs=in_specs,
        out_specs=out_specs,
        out_shape=out_shape,
        compiler_params=_params("arbitrary", "arbitrary"),
        name="retention",
    )(*args)
    if cast_w is None:
        return outs[0], outs[1]
    return outs[0], outs[1], (outs[2] if side else cast_w.astype(BF16))


def _rotary_tables(pos):
    half = DK_RET // 2
    inv = ROPE_BASE ** (-jnp.arange(half, dtype=F32) / half)
    ang = pos.astype(F32)[:, None] * inv[None, :]
    cos = jnp.cos(ang)
    sin = jnp.sin(ang)
    return jnp.concatenate([cos, cos], axis=-1), jnp.concatenate([-sin, sin], axis=-1)


def _feature_rows_kernel(w_ref, o_ref, *, valid):
    x = w_ref[...]
    if valid < x.shape[0]:
        row = lax.broadcasted_iota(jnp.int32, x.shape, 0)
        x = jnp.where(row < valid, x, 0.0)
    o_ref[...] = x.T.astype(BF16)


def _feature_rows(w_t, layer, segments, rows_per_step, valid=None):
    _, _, d = w_t.shape
    rb = rows_per_step
    starts = [s + k * rb for s, n in segments for k in range(n // rb)]
    assert all(n % rb == 0 for _, n in segments) and all(s % SUBLANES == 0 for s in starts), segments

    def index_map(i):
        tile_row = starts[-1] // SUBLANES
        for k in range(len(starts) - 2, -1, -1):
            tile_row = jnp.where(i <= k, starts[k] // SUBLANES, tile_row)
        return (layer, tile_row * SUBLANES, 0)

    return pl.pallas_call(
        functools.partial(_feature_rows_kernel, valid=rb if valid is None else valid),
        grid=(len(starts),),
        in_specs=[pl.BlockSpec((None, pl.Element(rb), pl.Element(d)), index_map)],
        out_specs=pl.BlockSpec((d, rb), lambda i: (0, i)),
        out_shape=jax.ShapeDtypeStruct((d, rb * len(starts)), BF16),
        compiler_params=_params("arbitrary"),
        name="w_in_rows",
    )(w_t)


def _in_proj_weights(w_in, layer, b_forget, h_fox, h_ret):
    fw = h_fox * HD_FOX
    qkw = h_ret * DK_RET
    vw = h_ret * DV_RET
    o1, o2, o3 = fw, 2 * fw, 3 * fw
    o4 = o3 + h_fox
    o6 = o4 + 2 * qkw
    o7 = o6 + vw
    w_t = jnp.swapaxes(w_in, 1, 2)
    groups = dict(qv=[(0, fw), (o6, vw)], k_f=[(o1, fw)], v_f=[(o2, fw)], qkg=[(o4, 2 * qkw), (o7, vw)])
    cols, col0 = {}, 0
    for name, segs in groups.items():
        width = sum(n for _, n in segs)
        cols[name] = (col0, width)
        col0 += width
    return dict(
        w=_feature_rows(w_t, layer, [seg for segs in groups.values() for seg in segs], 512), cols=cols,
        logf=_feature_rows(w_t, layer, [(o3, LANES)], LANES, valid=h_fox),
        b_logf=jnp.pad(b_forget, (0, LANES - h_fox)).reshape(1, LANES).astype(F32))


def _project(x2d, g_attn, w):
    h = _rmsnorm(x2d, g_attn, BF16)
    mm = lambda name, dtype, **kw: _matmul(h, w["w"], dtype, w_cols=w["cols"][name], **kw)
    qkg, logf = mm("qkg", F32, logf=(w["logf"], w["b_logf"]))
    return dict(qv=mm("qv", BF16), k_f=mm("k_f", F32), v_f=mm("v_f", F32), qkg=qkg, logf=logf)


def _finish_layer(x2d, o_fox, r, g_fox_out, g_mlp, w_out_b, w_up_b, w_down):
    x1, h2 = _out_proj(o_fox, g_fox_out, r, w_out_b, x2d, g_mlp)
    if w_down.dtype == BF16:
        u = _matmul(h2, w_up_b, BF16, epilogue="relu2")
    else:
        u, w_down = _matmul(h2, w_up_b, BF16, epilogue="relu2", cast_w=w_down)
    return _matmul_residual_ksplit(u, w_down, x1), w_down


def _retention_log_gamma(h_ret):
    lg = jnp.log1p(-jnp.exp2(-5.0 - jnp.arange(h_ret, dtype=F32)))
    return jnp.broadcast_to(lg[:, None, None], (h_ret, 1, LANES))


def kernel(x_prompt, x_sample, cache_fox_k, cache_fox_v, cache_fox_logf, state_ret,
           g_attn, w_in, b_forget, g_fox_out, g_ret_out, w_out, g_mlp, w_up, w_down, g_final):
    bp, seq, d = x_prompt.shape
    bs, t_new, _ = x_sample.shape
    depth, _, past, h_fox, _ = cache_fox_k.shape
    h_ret = state_ret.shape[2]
    fw = h_fox * HD_FOX
    dtype = x_prompt.dtype

    lg = _retention_log_gamma(h_ret)
    cos_p, sin_p = _rotary_tables(jnp.arange(seq))
    cos_s, sin_s = _rotary_tables(past + jnp.arange(t_new))
    blk_p = _tile(seq, 256, CHUNK)

    yp = x_prompt.reshape(bp * seq, d)
    ys = x_sample.reshape(bs * t_new, d)
    outs = [[] for _ in range(8)]
    for l in range(depth):
        w = _in_proj_weights(w_in, l, b_forget[l], h_fox, h_ret)

        p = _project(yp, g_attn[l], w)
        logf_p = p["logf"][:, :h_fox]
        _, aug_rows = _cumsum_lanes(jnp.swapaxes(logf_p.reshape(bp, seq, h_fox), 1, 2),
                                    bias_scale=HD_FOX ** 0.5)
        aug_cols = jnp.swapaxes(aug_rows, 1, 2).reshape(bp * seq, LANES)
        o_fox, w_up_b = _fox_prompt(p["qv"], p["k_f"], p["v_f"], aug_cols, aug_rows, bp, seq, h_fox, w_up[l])
        s0 = jnp.zeros((bp, h_ret, DK_RET, DV_RET), F32)
        r, s_fin, w_out_b = _retention(p["qkg"], p["qv"], cos_p, sin_p, g_ret_out[l], lg,
                                       s0, bp, seq, h_ret, blk_p, 2 if h_ret % 2 == 0 else 1, fw,
                                       cast_w=w_out[l])
        yp, w_down_b = _finish_layer(yp, o_fox, r, g_fox_out[l], g_mlp[l], w_out_b, w_up_b, w_down[l])
        outs[0].append(p["k_f"].reshape(bp, seq, h_fox, HD_FOX))
        outs[1].append(p["v_f"].reshape(bp, seq, h_fox, HD_FOX))
        outs[2].append(logf_p.reshape(bp, seq, h_fox).astype(dtype))
        outs[3].append(s_fin.astype(dtype))

        p = _project(ys, g_attn[l], w)
        logf_s = p["logf"][:, :h_fox]
        total = past + t_new
        padded = -(-total // LANES) * LANES
        logf_all = jnp.concatenate(
            [jnp.swapaxes(cache_fox_logf[l].astype(F32), 1, 2),
             jnp.swapaxes(logf_s.reshape(bs, t_new, h_fox), 1, 2),
             jnp.zeros((bs, h_fox, padded - total), F32)], axis=2)
        c_all = _cumsum_lanes(logf_all)
        ckc = c_all[:, :, :past]
        ckn = c_all[:, :, past:total]
        cq = jnp.swapaxes(ckn, 1, 2).reshape(bs * t_new, h_fox)
        o_fox = _fox_sample(p["qv"], cache_fox_k[l], cache_fox_v[l], p["k_f"], p["v_f"],
                            cq, ckc, ckn, bs, t_new, h_fox)
        r, s_new = _retention(p["qkg"], p["qv"], cos_s, sin_s, g_ret_out[l], lg,
                              state_ret[l].astype(F32), bs, t_new, h_ret, t_new, h_ret, fw)
        ys, _ = _finish_layer(ys, o_fox, r, g_fox_out[l], g_mlp[l], w_out_b, w_up_b, w_down_b)
        outs[4].append(p["k_f"].reshape(bs, t_new, h_fox, HD_FOX))
        outs[5].append(p["v_f"].reshape(bs, t_new, h_fox, HD_FOX))
        outs[6].append(logf_s.reshape(bs, t_new, h_fox).astype(dtype))
        outs[7].append(s_new.astype(dtype))

    y_prompt = _rmsnorm(yp, g_final, dtype).reshape(bp, seq, d)
    y_sample = _rmsnorm(ys, g_final, dtype).reshape(bs, t_new, d)
    return (y_prompt, y_sample) + tuple(jnp.stack(o) for o in outs)
```

```python
import functools

import jax
import jax.numpy as jnp
from jax import lax
from jax.experimental import pallas as pl
from jax.experimental.pallas import tpu as pltpu

CHUNK = 64
HD_FOX = 128
DK_RET = 128
DV_RET = 256
ROPE_BASE = 10000.0
EPS = 1e-6

LANES = 128
SUBLANES = 8
VMEM_LIMIT_BYTES = 62 * 1024 * 1024

F32 = jnp.float32
BF16 = jnp.bfloat16


def _params(*sem):
    return pltpu.CompilerParams(dimension_semantics=sem, vmem_limit_bytes=VMEM_LIMIT_BYTES)


def _tile(n, pref, mult=LANES):
    if n <= pref:
        return n
    t = (pref // mult) * mult
    while t > mult and n % t:
        t -= mult
    assert n % t == 0, (n, pref)
    return t


def _log_sigmoid(z):
    return jnp.minimum(z, 0.0) - jnp.log(1.0 + jnp.exp(-jnp.abs(z)))


def _rms_kernel(x_ref, g_ref, *rest):
    x = x_ref[...]
    ms = jnp.mean(x * x, axis=-1, keepdims=True)
    h = x * lax.rsqrt(ms + EPS) * g_ref[...]
    if len(rest) == 1:
        (o_ref,) = rest
        o_ref[...] = h.astype(o_ref.dtype)
    else:
        wlf_ref, b_ref, o_ref, lf_ref = rest
        hb = h.astype(o_ref.dtype)
        o_ref[...] = hb
        lf_ref[...] = _log_sigmoid(jnp.dot(hb, wlf_ref[...], preferred_element_type=F32) + b_ref[...])


def _rmsnorm(x, g, out_dtype, logf=None):
    m, d = x.shape
    tr = _tile(m, 256, 8)
    in_specs = [pl.BlockSpec((tr, d), lambda i: (i, 0)),
                pl.BlockSpec((1, d), lambda i: (0, 0))]
    args = [x, g.reshape(1, d)]
    out_specs = [pl.BlockSpec((tr, d), lambda i: (i, 0))]
    out_shape = [jax.ShapeDtypeStruct((m, d), out_dtype)]
    if logf is not None:
        in_specs += [pl.BlockSpec((d, LANES), lambda i: (0, 0)), pl.BlockSpec((1, LANES), lambda i: (0, 0))]
        args += list(logf)
        out_specs.append(pl.BlockSpec((tr, LANES), lambda i: (i, 0)))
        out_shape.append(jax.ShapeDtypeStruct((m, LANES), F32))
    outs = pl.pallas_call(
        _rms_kernel,
        grid=(m // tr,),
        in_specs=in_specs,
        out_specs=out_specs,
        out_shape=out_shape,
        compiler_params=_params("arbitrary"),
        name="rmsnorm",
    )(*args)
    return outs[0] if logf is None else tuple(outs)


def _side_cast_specs(w, grid):
    steps = 1
    for g in grid:
        steps *= g
    rows, cols = w.shape
    if rows % steps or (rows // steps) % 16:
        return None

    def index_map(*idx):
        lin = idx[0]
        for g, i in zip(grid[1:], idx[1:]):
            lin = lin * g + i
        return (lin, 0)

    spec = pl.BlockSpec((rows // steps, cols), index_map)
    return spec, jax.ShapeDtypeStruct((rows, cols), BF16)


def _mm_kernel(a_ref, w_ref, *rest, epilogue, side):
    if side:
        side_in, o_ref, side_out = rest
        side_out[...] = side_in[...].astype(BF16)
    else:
        (o_ref,) = rest
    acc = jnp.dot(a_ref[...], w_ref[...], preferred_element_type=F32)
    if epilogue == "relu2":
        u = jnp.maximum(acc, 0.0)
        o_ref[...] = (u * u).astype(o_ref.dtype)
    else:
        o_ref[...] = acc.astype(o_ref.dtype)


def _matmul(a, w, out_dtype, epilogue="cast", tm_pref=1024, tn_pref=1024, cast_w=None, w_cols=None):
    m, k = a.shape
    col0, n = (0, w.shape[1]) if w_cols is None else w_cols
    tm = _tile(m, tm_pref, 16)
    tn = _tile(n, tn_pref)
    assert col0 % tn == 0, (col0, tn)
    j0 = col0 // tn
    grid = (m // tm, n // tn)
    in_specs = [pl.BlockSpec((tm, k), lambda i, j: (i, 0)),
                pl.BlockSpec((k, tn), lambda i, j: (0, j0 + j))]
    args = [a, w]
    out_specs = [pl.BlockSpec((tm, tn), lambda i, j: (i, j))]
    out_shape = [jax.ShapeDtypeStruct((m, n), out_dtype)]
    side = _side_cast_specs(cast_w, grid) if cast_w is not None else None
    if side:
        in_specs.append(side[0])
        args.append(cast_w)
        out_specs.append(side[0])
        out_shape.append(side[1])
    outs = list(pl.pallas_call(
        functools.partial(_mm_kernel, epilogue=epilogue, side=bool(side)),
        grid=grid,
        in_specs=in_specs,
        out_specs=out_specs,
        out_shape=out_shape,
        compiler_params=_params("arbitrary", "arbitrary"),
        name="matmul_" + epilogue,
    )(*args))
    if cast_w is not None and not side:
        outs.append(cast_w.astype(BF16))
    return outs[0] if len(outs) == 1 else tuple(outs)


def _mix_mm_kernel(o_ref, gf_ref, r_ref, w_ref, res_ref, gn_ref, out_ref, h_ref, a_scr, x_scr, *, fw):
    j = pl.program_id(1)
    nj, _, tn = x_scr.shape

    @pl.when(j == 0)
    def _():
        o = o_ref[...]
        ms = jnp.mean(o * o, axis=-1, keepdims=True)
        a_scr[:, :fw] = (o * lax.rsqrt(ms + EPS) * gf_ref[...]).astype(BF16)
        a_scr[:, fw:] = r_ref[...]

    x = res_ref[...] + jnp.dot(a_scr[...], w_ref[...], preferred_element_type=F32)
    out_ref[...] = x
    x_scr[j] = x

    @pl.when(j == nj - 1)
    def _():
        sq = x_scr[0] * x_scr[0]
        ssum = jnp.sum(sq, axis=-1, keepdims=True)
        for t in range(1, nj):
            xt = x_scr[t]
            ssum = ssum + jnp.sum(xt * xt, axis=-1, keepdims=True)
        rstd = lax.rsqrt(ssum / (nj * tn) + EPS)
        for t in range(nj):
            cols = slice(t * tn, (t + 1) * tn)
            h_ref[:, cols] = (x_scr[t] * rstd * gn_ref[:, cols]).astype(h_ref.dtype)


def _out_proj(o_fox, g_fox, r, w, res, g_next):
    m, fw = o_fox.shape
    rw = r.shape[1]
    n = w.shape[1]
    tm = _tile(m, 512, 16)
    tn = _tile(n, 1024)
    return pl.pallas_call(
        functools.partial(_mix_mm_kernel, fw=fw),
        grid=(m // tm, n // tn),
        in_specs=[pl.BlockSpec((tm, fw), lambda i, j: (i, 0)),
                  pl.BlockSpec((1, fw), lambda i, j: (0, 0)),
                  pl.BlockSpec((tm, rw), lambda i, j: (i, 0)),
                  pl.BlockSpec((fw + rw, tn), lambda i, j: (0, j)),
                  pl.BlockSpec((tm, tn), lambda i, j: (i, j)),
                  pl.BlockSpec((1, n), lambda i, j: (0, 0))],
        out_specs=[pl.BlockSpec((tm, tn), lambda i, j: (i, j)),
                   pl.BlockSpec((tm, n), lambda i, j: (i, 0))],
        out_shape=[jax.ShapeDtypeStruct((m, n), F32), jax.ShapeDtypeStruct((m, n), BF16)],
        scratch_shapes=[pltpu.VMEM((tm, fw + rw), BF16), pltpu.VMEM((n // tn, tm, tn), F32)],
        compiler_params=_params("arbitrary", "arbitrary"),
        name="out_proj",
    )(o_fox, g_fox.reshape(1, fw), r, w, res, g_next.reshape(1, n))


def _mm_acc_kernel(a_ref, w_ref, res_ref, out_ref):
    @pl.when(pl.program_id(2) == 0)
    def _():
        out_ref[...] = res_ref[...]

    out_ref[...] += jnp.dot(a_ref[...], w_ref[...], preferred_element_type=F32)


def _matmul_residual_ksplit(a, w, res):
    m, k = a.shape
    n = w.shape[1]
    tm = _tile(m, 1024, 16)
    tn = _tile(n, 1024)
    tk = _tile(k, 4096)
    return pl.pallas_call(
        _mm_acc_kernel,
        grid=(m // tm, n // tn, k // tk),
        in_specs=[pl.BlockSpec((tm, tk), lambda i, j, kk: (i, kk)),
                  pl.BlockSpec((tk, tn), lambda i, j, kk: (kk, j)),
                  pl.BlockSpec((tm, tn), lambda i, j, kk: (i, j))],
        out_specs=pl.BlockSpec((tm, tn), lambda i, j, kk: (i, j)),
        out_shape=jax.ShapeDtypeStruct((m, n), F32),
        compiler_params=_params("arbitrary", "arbitrary", "arbitrary"),
        name="down_proj",
    )(a, w, res)


def _cumsum_kernel(x_ref, o_ref, *aug_ref, bias_scale):
    x = x_ref[...]
    h, t = x.shape
    lane = lax.broadcasted_iota(jnp.int32, x.shape, 1)
    d = 1
    while d < t:
        x = x + jnp.where(lane >= d, pltpu.roll(x, d, axis=1), 0.0)
        d *= 2
    o_ref[...] = x
    if aug_ref:
        (a_ref,) = aug_ref
        y = x * bias_scale
        hi = y.astype(BF16).astype(F32)
        r1 = y - hi
        mid = r1.astype(BF16).astype(F32)
        lo = r1 - mid
        rows = a_ref.shape[0]
        base = jnp.concatenate([hi, mid, lo, jnp.zeros((rows - 3 * h, t), F32)], axis=0)
        row = lax.broadcasted_iota(jnp.int32, base.shape, 0)
        a_ref[...] = jnp.where(row == 3 * h, 1.0, base).astype(BF16)


def _cumsum_lanes(x, bias_scale=None):
    b, h, t = x.shape
    out_specs = [pl.BlockSpec((None, h, t), lambda i: (i, 0, 0))]
    out_shape = [jax.ShapeDtypeStruct((b, h, t), F32)]
    if bias_scale is not None:
        assert 3 * h < LANES, h
        out_specs.append(pl.BlockSpec((None, LANES, t), lambda i: (i, 0, 0)))
        out_shape.append(jax.ShapeDtypeStruct((b, LANES, t), BF16))
    outs = pl.pallas_call(
        functools.partial(_cumsum_kernel, bias_scale=bias_scale),
        grid=(b,),
        in_specs=[pl.BlockSpec((None, h, t), lambda i: (i, 0, 0))],
        out_specs=out_specs,
        out_shape=out_shape,
        compiler_params=_params("arbitrary"),
        name="cumsum_logf",
    )(x)
    return outs[0] if bias_scale is None else tuple(outs)


def _qk(q_bf16, k_bf16):
    return lax.dot_general(q_bf16, k_bf16, (((1,), (1,)), ((), ())), preferred_element_type=F32)


def _fox_prompt_kernel(q_ref, k_ref, v_ref, ac_ref, ar_ref, *rest, tq, heads, side):
    if side:
        side_in, o_ref, side_out, qa_scr, kt_scr, vb_scr = rest
        side_out[...] = side_in[...].astype(BF16)
    else:
        o_ref, qa_scr, kt_scr, vb_scr = rest
    hp = vb_scr.shape[0]
    seq = q_ref.shape[0]
    r = lax.broadcasted_iota(jnp.int32, (LANES, LANES), 0)
    c = lax.broadcasted_iota(jnp.int32, (LANES, LANES), 1)
    for hh in range(hp):
        h = pl.program_id(1) * hp + hh
        cols = slice(hh * HD_FOX, (hh + 1) * HD_FOX)
        pick_q = (((r == h) & (c == 0)) | ((r == heads + h) & (c == 1)) | ((r == 2 * heads + h) & (c == 2))
                  | ((r == 3 * heads) & (c >= 3) & (c < 6)))
        pick_k = (((r == 3) & (c == h)) | ((r == 4) & (c == heads + h)) | ((r == 5) & (c == 2 * heads + h)))
        sel_q = jnp.where(pick_q, 1.0, 0.0).astype(BF16)
        sel_k = (jnp.where((r < 3) & (c == 3 * heads), 1.0, 0.0) - jnp.where(pick_k, 1.0, 0.0)).astype(BF16)
        qa_scr[hh, :, :HD_FOX] = q_ref[:, cols]
        qa_scr[hh, :, HD_FOX:] = jnp.dot(ac_ref[...], sel_q, preferred_element_type=F32).astype(BF16)
        kt_scr[hh, :HD_FOX, :] = k_ref[:, cols].T.astype(BF16)
        kt_scr[hh, HD_FOX:, :] = jnp.dot(sel_k, ar_ref[...], preferred_element_type=F32).astype(BF16)
        vb_scr[hh] = v_ref[:, cols].astype(BF16)
    c2 = (HD_FOX ** -0.5) * 1.4426950408889634
    row = lax.broadcasted_iota(jnp.int32, (tq, tq), 0)
    col = lax.broadcasted_iota(jnp.int32, (tq, tq), 1)
    causal = col <= row
    nq = seq // tq
    for pair in range((nq + 1) // 2):
        blocks = sorted({pair, nq - 1 - pair})
        chains = [(hh, qi) for qi in blocks for hh in range(hp)]
        qa = {c: qa_scr[c[0], c[1] * tq:(c[1] + 1) * tq, :] for c in chains}
        s_d = {(hh, qi): jnp.where(causal, jnp.dot(qa[hh, qi], kt_scr[hh, :, qi * tq:(qi + 1) * tq],
                                                   preferred_element_type=F32), -jnp.inf) for hh, qi in chains}
        m = {c: jnp.max(s_d[c], axis=1, keepdims=True) for c in chains}
        early = [c for c in chains if c[1] > 0]
        s_o = {(hh, qi): jnp.dot(qa[hh, qi], kt_scr[hh, :, :qi * tq], preferred_element_type=F32)
               for hh, qi in early}
        for c in early:
            m[c] = jnp.maximum(m[c], jnp.max(s_o[c], axis=1, keepdims=True))
        p_o = {c: jnp.exp2((s_o[c] - m[c]) * c2) for c in early}
        p_d = {c: jnp.exp2((s_d[c] - m[c]) * c2) for c in chains}
        for hh, qi in chains:
            lo, hi = qi * tq, (qi + 1) * tq
            l = jnp.sum(p_d[hh, qi], axis=1, keepdims=True)
            acc = jnp.dot(p_d[hh, qi].astype(BF16), vb_scr[hh, lo:hi, :], preferred_element_type=F32)
            if qi > 0:
                l = l + jnp.sum(p_o[hh, qi], axis=1, keepdims=True)
                acc = acc + jnp.dot(p_o[hh, qi].astype(BF16), vb_scr[hh, :lo, :], preferred_element_type=F32)
            o_ref[lo:hi, hh * HD_FOX:(hh + 1) * HD_FOX] = acc / l


def _fox_prompt(q, k, v, aug_cols, aug_rows, batch, seq, heads, cast_w):
    m, fw = k.shape
    tq = _tile(seq, 256)
    hp = 2 if heads % 2 == 0 else 1
    grid = (batch, heads // hp)
    head_blk = lambda b, h: (b, h)
    in_specs = [pl.BlockSpec((seq, hp * HD_FOX), head_blk),
                pl.BlockSpec((seq, hp * HD_FOX), head_blk),
                pl.BlockSpec((seq, hp * HD_FOX), head_blk),
                pl.BlockSpec((seq, LANES), lambda b, h: (b, 0)),
                pl.BlockSpec((None, LANES, seq), lambda b, h: (b, 0, 0))]
    args = [q, k, v, aug_cols, aug_rows]
    out_specs = [pl.BlockSpec((seq, hp * HD_FOX), head_blk)]
    out_shape = [jax.ShapeDtypeStruct((m, fw), F32)]
    side = _side_cast_specs(cast_w, grid)
    if side:
        in_specs.append(side[0])
        args.append(cast_w)
        out_specs.append(side[0])
        out_shape.append(side[1])
    outs = pl.pallas_call(
        functools.partial(_fox_prompt_kernel, tq=tq, heads=heads, side=bool(side)),
        grid=grid,
        in_specs=in_specs,
        out_specs=out_specs,
        out_shape=out_shape,
        scratch_shapes=[pltpu.VMEM((hp, seq, 2 * HD_FOX), BF16), pltpu.VMEM((hp, 2 * HD_FOX, seq), BF16),
                        pltpu.VMEM((hp, seq, HD_FOX), BF16)],
        compiler_params=_params("arbitrary", "arbitrary"),
        name="fox_prompt",
    )(*args)
    return outs[0], (outs[1] if side else cast_w.astype(BF16))


def _every_eighth_row(ref, j):
    n, s, d = ref.shape
    return ref.reshape(n * s, d)[pl.ds(j, n, stride=s), :]


def _fox_sample_kernel(*refs, heads):
    ng = heads // SUBLANES
    q_ref = refs[0]
    k_refs = refs[1:1 + ng]
    v_refs = refs[1 + ng:1 + 2 * ng]
    kn_ref, vn_ref, cq_ref, ckc_ref, ckn_ref, o_ref, m_scr, l_scr, acc_scr, s_scr, p_scr = refs[1 + 2 * ng:]
    ki = pl.program_id(1)
    last = pl.num_programs(1) - 1
    scale = HD_FOX ** -0.5
    t = q_ref.shape[0]
    head_cols = lambda h: slice(h * HD_FOX, (h + 1) * HD_FOX)
    head_rows = lambda h: slice(h * t, (h + 1) * t)

    @pl.when(ki == 0)
    def _():
        m_scr[...] = jnp.full(m_scr.shape, -jnp.inf, F32)
        l_scr[...] = jnp.zeros(l_scr.shape, F32)
        acc_scr[...] = jnp.zeros(acc_scr.shape, F32)

    def cached(group_refs, h):
        return _every_eighth_row(group_refs[h // SUBLANES], h % SUBLANES).astype(BF16)

    def update(load_k, load_v, ck_ref, n, causal):
        for h in range(heads):
            s = _qk(q_ref[:, head_cols(h)], load_k(h)) * scale + cq_ref[:, h:h + 1] - ck_ref[h:h + 1, :]
            if causal:
                row = lax.broadcasted_iota(jnp.int32, s.shape, 0)
                col = lax.broadcasted_iota(jnp.int32, s.shape, 1)
                s = jnp.where(col <= row, s, -jnp.inf)
            s_scr[head_rows(h), :n] = s
        s = s_scr[:, :n]
        m_prev = m_scr[:, :1]
        m_new = jnp.maximum(m_prev, jnp.max(s, axis=1, keepdims=True))
        alpha = jnp.exp(m_prev - m_new)
        p = jnp.exp(s - m_new)
        l_new = alpha * l_scr[:, :1] + jnp.sum(p, axis=1, keepdims=True)
        m_scr[...] = jnp.broadcast_to(m_new, m_scr.shape)
        l_scr[...] = jnp.broadcast_to(l_new, l_scr.shape)
        p_scr[:, :n] = p.astype(BF16)
        for h in range(heads):
            pv = jnp.dot(p_scr[head_rows(h), :n], load_v(h), preferred_element_type=F32)
            acc_scr[:, head_cols(h)] = alpha[head_rows(h)] * acc_scr[:, head_cols(h)] + pv

    @pl.when(ki < last)
    def _():
        update(functools.partial(cached, k_refs), functools.partial(cached, v_refs), ckc_ref,
               s_scr.shape[1], causal=False)

    @pl.when(ki == last)
    def _():
        new_rows = lambda ref, h: ref[:, head_cols(h)].astype(BF16)
        update(functools.partial(new_rows, kn_ref), functools.partial(new_rows, vn_ref), ckn_ref, t, causal=True)
        for h in range(heads):
            o_ref[:, head_cols(h)] = acc_scr[:, head_cols(h)] / l_scr[head_rows(h), :1]


def _fox_sample(q, kc, vc, kn, vn, cq, ckc, ckn, batch, t_new, heads):
    past = kc.shape[1]
    fw = heads * HD_FOX
    assert heads % SUBLANES == 0, heads
    ng = heads // SUBLANES
    tk = _tile(past, 1024)
    nkc = past // tk
    kc = kc.reshape(batch, past, ng, SUBLANES, HD_FOX)
    vc = vc.reshape(batch, past, ng, SUBLANES, HD_FOX)
    row_blk = pl.BlockSpec((t_new, fw), lambda b, ki: (b, 0))
    group_blk = [pl.BlockSpec((None, tk, None, SUBLANES, HD_FOX),
                              functools.partial(lambda b, ki, g: (b, jnp.minimum(ki, nkc - 1), g, 0, 0), g=g))
                 for g in range(ng)]
    return pl.pallas_call(
        functools.partial(_fox_sample_kernel, heads=heads),
        grid=(batch, nkc + 1),
        in_specs=[row_blk] + group_blk + group_blk + [
            row_blk, row_blk,
            pl.BlockSpec((t_new, heads), lambda b, ki: (b, 0)),
            pl.BlockSpec((None, heads, tk), lambda b, ki: (b, 0, jnp.minimum(ki, nkc - 1))),
            pl.BlockSpec((None, heads, t_new), lambda b, ki: (b, 0, 0))],
        out_specs=row_blk,
        out_shape=jax.ShapeDtypeStruct((batch * t_new, fw), F32),
        scratch_shapes=[pltpu.VMEM((heads * t_new, LANES), F32), pltpu.VMEM((heads * t_new, LANES), F32),
                        pltpu.VMEM((t_new, fw), F32),
                        pltpu.VMEM((heads * t_new, tk), F32), pltpu.VMEM((heads * t_new, tk), BF16)],
        compiler_params=_params("arbitrary", "arbitrary"),
        name="fox_sample",
    )(q, *([kc] * ng), *([vc] * ng), kn, vn, cq, ckc, ckn)


def _retention_kernel(q_ref, k_ref, v_ref, g_ref, cos_ref, sin_ref, gro_ref, lg_ref, s0_ref, *rest,
                      blk, heads, side):
    if side:
        side_in, r_ref, sout_ref, side_out = rest
        side_out[...] = side_in[...].astype(BF16)
    else:
        r_ref, sout_ref = rest
    seq = q_ref.shape[0]
    half = DK_RET // 2
    ri = lax.broadcasted_iota(jnp.int32, (blk, blk), 0)
    ci = lax.broadcasted_iota(jnp.int32, (blk, blk), 1)
    dist = jnp.abs(ri - ci).astype(F32)
    visible = ci // CHUNK <= ri // CHUNK
    idx = lax.broadcasted_iota(jnp.int32, (blk, 1), 0).astype(F32)
    lgs = [lg_ref[hh, :, :1] for hh in range(heads)]
    decay = [jnp.where(visible, jnp.exp(lg * dist), 0.0) for lg in lgs]
    cross = [jnp.exp(lg * (idx + 1.0)) for lg in lgs]
    k_decay = [jnp.exp(lg * (blk - 1.0 - idx)) for lg in lgs]
    s_decay = [jnp.exp(lg * blk) for lg in lgs]
    state = [s0_ref[hh] for hh in range(heads)]
    for t in range(seq // blk):
        rows = slice(t * blk, (t + 1) * blk)
        cos = cos_ref[rows, :]
        sin = sin_ref[rows, :]
        for hh in range(heads):
            qk_cols = slice(hh * DK_RET, (hh + 1) * DK_RET)
            v_cols = slice(hh * DV_RET, (hh + 1) * DV_RET)
            q = q_ref[rows, qk_cols]
            k = k_ref[rows, qk_cols]
            q = q * cos + pltpu.roll(q, half, axis=1) * sin
            k = (k * cos + pltpu.roll(k, half, axis=1) * sin) * (DK_RET ** -0.5)
            v = v_ref[rows, v_cols]
            qb = q.astype(BF16)
            a = _qk(qb, k.astype(BF16)) * decay[hh]
            o = jnp.dot(a.astype(BF16), v, preferred_element_type=F32)
            o = o + jnp.dot(qb, state[hh].astype(BF16), preferred_element_type=F32) * cross[hh]
            state[hh] = s_decay[hh] * state[hh] + lax.dot_general(
                (k * k_decay[hh]).astype(BF16), v, (((0,), (0,)), ((), ())), preferred_element_type=F32)
            ms = jnp.mean(o * o, axis=-1, keepdims=True)
            g = g_ref[rows, v_cols]
            silu = g / (1.0 + jnp.exp(-g))
            r_ref[rows, v_cols] = (o * lax.rsqrt(ms + EPS) * gro_ref[:, v_cols] * silu).astype(r_ref.dtype)
    for hh in range(heads):
        sout_ref[hh] = state[hh]


def _retention(qkg, qv, cos, sin, g_ret_out, lg, s0, batch, seq, heads, blk, heads_per_step, fox_width,
               cast_w=None):
    m = qkg.shape[0]
    hps = heads_per_step
    groups = heads // hps
    assert fox_width % (hps * DV_RET) == 0, (fox_width, hps)
    v_off = fox_width // (hps * DV_RET)
    grid = (batch, groups)
    row = lambda b, h: (b, h)
    state = pl.BlockSpec((None, hps, DK_RET, DV_RET), lambda b, h: (b, h, 0, 0))
    in_specs = [pl.BlockSpec((seq, hps * DK_RET), row),
                pl.BlockSpec((seq, hps * DK_RET), lambda b, h: (b, groups + h)),
                pl.BlockSpec((seq, hps * DV_RET), lambda b, h: (b, v_off + h)),
                pl.BlockSpec((seq, hps * DV_RET), lambda b, h: (b, groups + h)),
                pl.BlockSpec((seq, DK_RET), lambda b, h: (0, 0)),
                pl.BlockSpec((seq, DK_RET), lambda b, h: (0, 0)),
                pl.BlockSpec((1, hps * DV_RET), lambda b, h: (0, h)),
                pl.BlockSpec((hps, 1, LANES), lambda b, h: (h, 0, 0)),
                state]
    args = [qkg, qkg, qv, qkg, cos, sin, g_ret_out.reshape(1, -1), lg, s0]
    out_specs = [pl.BlockSpec((seq, hps * DV_RET), row), state]
    out_shape = [jax.ShapeDtypeStruct((m, heads * DV_RET), BF16),
                 jax.ShapeDtypeStruct((batch, heads, DK_RET, DV_RET), F32)]
    side = _side_cast_specs(cast_w, grid) if cast_w is not None else None
    if side:
        in_specs.append(side[0])
        args.append(cast_w)
        out_specs.append(side[0])
        out_shape.append(side[1])
    outs = pl.pallas_call(
        functools.partial(_retention_kernel, blk=blk, heads=hps, side=bool(side)),
        grid=grid,
        in_specs=in_specs,
        out_specs=out_specs,
        out_shape=out_shape,
        compiler_params=_params("arbitrary", "arbitrary"),
        name="retention",
    )(*args)
    if cast_w is None:
        return outs[0], outs[1]
    return outs[0], outs[1], (outs[2] if side else cast_w.astype(BF16))


def _rotary_tables(pos):
    half = DK_RET // 2
    inv = ROPE_BASE ** (-jnp.arange(half, dtype=F32) / half)
    ang = pos.astype(F32)[:, None] * inv[None, :]
    cos = jnp.cos(ang)
    sin = jnp.sin(ang)
    return jnp.concatenate([cos, cos], axis=-1), jnp.concatenate([-sin, sin], axis=-1)


def _feature_rows_kernel(w_ref, o_ref, *, valid):
    x = w_ref[...]
    if valid < x.shape[0]:
        row = lax.broadcasted_iota(jnp.int32, x.shape, 0)
        x = jnp.where(row < valid, x, 0.0)
    o_ref[...] = x.T.astype(BF16)


def _feature_rows(w_t, layer, segments, rows_per_step, valid=None):
    _, _, d = w_t.shape
    rb = rows_per_step
    starts = [s + k * rb for s, n in segments for k in range(n // rb)]
    assert all(n % rb == 0 for _, n in segments) and all(s % SUBLANES == 0 for s in starts), segments

    def index_map(i):
        tile_row = starts[-1] // SUBLANES
        for k in range(len(starts) - 2, -1, -1):
            tile_row = jnp.where(i <= k, starts[k] // SUBLANES, tile_row)
        return (layer, tile_row * SUBLANES, 0)

    return pl.pallas_call(
        functools.partial(_feature_rows_kernel, valid=rb if valid is None else valid),
        grid=(len(starts),),
        in_specs=[pl.BlockSpec((None, pl.Element(rb), pl.Element(d)), index_map)],
        out_specs=pl.BlockSpec((d, rb), lambda i: (0, i)),
        out_shape=jax.ShapeDtypeStruct((d, rb * len(starts)), BF16),
        compiler_params=_params("arbitrary"),
        name="w_in_rows",
    )(w_t)


def _in_proj_weights(w_in, layer, b_forget, h_fox, h_ret):
    fw = h_fox * HD_FOX
    qkw = h_ret * DK_RET
    vw = h_ret * DV_RET
    o1, o2, o3 = fw, 2 * fw, 3 * fw
    o4 = o3 + h_fox
    o6 = o4 + 2 * qkw
    o7 = o6 + vw
    w_t = jnp.swapaxes(w_in, 1, 2)
    groups = dict(qv=[(0, fw), (o6, vw)], k_f=[(o1, fw)], v_f=[(o2, fw)], qkg=[(o4, 2 * qkw), (o7, vw)])
    cols, col0 = {}, 0
    for name, segs in groups.items():
        width = sum(n for _, n in segs)
        cols[name] = (col0, width)
        col0 += width
    return dict(
        w=_feature_rows(w_t, layer, [seg for segs in groups.values() for seg in segs], 512), cols=cols,
        logf=_feature_rows(w_t, layer, [(o3, LANES)], LANES, valid=h_fox),
        b_logf=jnp.pad(b_forget, (0, LANES - h_fox)).reshape(1, LANES).astype(F32))


def _project(x2d, g_attn, w):
    h, logf = _rmsnorm(x2d, g_attn, BF16, logf=(w["logf"], w["b_logf"]))
    mm = lambda name, dtype: _matmul(h, w["w"], dtype, w_cols=w["cols"][name])
    return dict(qv=mm("qv", BF16), k_f=mm("k_f", F32), v_f=mm("v_f", F32), qkg=mm("qkg", F32), logf=logf)


def _finish_layer(x2d, o_fox, r, g_fox_out, g_mlp, w_out_b, w_up_b, w_down):
    x1, h2 = _out_proj(o_fox, g_fox_out, r, w_out_b, x2d, g_mlp)
    if w_down.dtype == BF16:
        u = _matmul(h2, w_up_b, BF16, epilogue="relu2")
    else:
        u, w_down = _matmul(h2, w_up_b, BF16, epilogue="relu2", cast_w=w_down)
    return _matmul_residual_ksplit(u, w_down, x1), w_down


def _retention_log_gamma(h_ret):
    lg = jnp.log1p(-jnp.exp2(-5.0 - jnp.arange(h_ret, dtype=F32)))
    return jnp.broadcast_to(lg[:, None, None], (h_ret, 1, LANES))


def kernel(x_prompt, x_sample, cache_fox_k, cache_fox_v, cache_fox_logf, state_ret,
           g_attn, w_in, b_forget, g_fox_out, g_ret_out, w_out, g_mlp, w_up, w_down, g_final):
    bp, seq, d = x_prompt.shape
    bs, t_new, _ = x_sample.shape
    depth, _, past, h_fox, _ = cache_fox_k.shape
    h_ret = state_ret.shape[2]
    fw = h_fox * HD_FOX
    dtype = x_prompt.dtype

    lg = _retention_log_gamma(h_ret)
    cos_p, sin_p = _rotary_tables(jnp.arange(seq))
    cos_s, sin_s = _rotary_tables(past + jnp.arange(t_new))
    blk_p = _tile(seq, 256, CHUNK)

    yp = x_prompt.reshape(bp * seq, d)
    ys = x_sample.reshape(bs * t_new, d)
    outs = [[] for _ in range(8)]
    for l in range(depth):
        w = _in_proj_weights(w_in, l, b_forget[l], h_fox, h_ret)

        p = _project(yp, g_attn[l], w)
        logf_p = p["logf"][:, :h_fox]
        _, aug_rows = _cumsum_lanes(jnp.swapaxes(logf_p.reshape(bp, seq, h_fox), 1, 2),
                                    bias_scale=HD_FOX ** 0.5)
        aug_cols = jnp.swapaxes(aug_rows, 1, 2).reshape(bp * seq, LANES)
        o_fox, w_up_b = _fox_prompt(p["qv"], p["k_f"], p["v_f"], aug_cols, aug_rows, bp, seq, h_fox, w_up[l])
        s0 = jnp.zeros((bp, h_ret, DK_RET, DV_RET), F32)
        r, s_fin, w_out_b = _retention(p["qkg"], p["qv"], cos_p, sin_p, g_ret_out[l], lg,
                                       s0, bp, seq, h_ret, blk_p, 2 if h_ret % 2 == 0 else 1, fw,
                                       cast_w=w_out[l])
        yp, w_down_b = _finish_layer(yp, o_fox, r, g_fox_out[l], g_mlp[l], w_out_b, w_up_b, w_down[l])
        outs[0].append(p["k_f"].reshape(bp, seq, h_fox, HD_FOX))
        outs[1].append(p["v_f"].reshape(bp, seq, h_fox, HD_FOX))
        outs[2].append(logf_p.reshape(bp, seq, h_fox).astype(dtype))
        outs[3].append(s_fin.astype(dtype))

        p = _project(ys, g_attn[l], w)
        logf_s = p["logf"][:, :h_fox]
        total = past + t_new
        padded = -(-total // LANES) * LANES
        logf_all = jnp.concatenate(
            [jnp.swapaxes(cache_fox_logf[l].astype(F32), 1, 2),
             jnp.swapaxes(logf_s.reshape(bs, t_new, h_fox), 1, 2),
             jnp.zeros((bs, h_fox, padded - total), F32)], axis=2)
        c_all = _cumsum_lanes(logf_all)
        ckc = c_all[:, :, :past]
        ckn = c_all[:, :, past:total]
        cq = jnp.swapaxes(ckn, 1, 2).reshape(bs * t_new, h_fox)
        o_fox = _fox_sample(p["qv"], cache_fox_k[l], cache_fox_v[l], p["k_f"], p["v_f"],
                            cq, ckc, ckn, bs, t_new, h_fox)
        r, s_new = _retention(p["qkg"], p["qv"], cos_s, sin_s, g_ret_out[l], lg,
                              state_ret[l].astype(F32), bs, t_new, h_ret, t_new, h_ret, fw)
        ys, _ = _finish_layer(ys, o_fox, r, g_fox_out[l], g_mlp[l], w_out_b, w_up_b, w_down_b)
        outs[4].append(p["k_f"].reshape(bs, t_new, h_fox, HD_FOX))
        outs[5].append(p["v_f"].reshape(bs, t_new, h_fox, HD_FOX))
        outs[6].append(logf_s.reshape(bs, t_new, h_fox).astype(dtype))
        outs[7].append(s_new.astype(dtype))

    y_prompt = _rmsnorm(yp, g_final, dtype).reshape(bp, seq, d)
    y_sample = _rmsnorm(ys, g_final, dtype).reshape(bs, t_new, d)
    return (y_prompt, y_sample) + tuple(jnp.stack(o) for o in outs)
```

```python
import functools

import jax
import jax.numpy as jnp
from jax import lax
from jax.experimental import pallas as pl
from jax.experimental.pallas import tpu as pltpu

CHUNK = 64
HD_FOX = 128
DK_RET = 128
DV_RET = 256
ROPE_BASE = 10000.0
EPS = 1e-6

LANES = 128
SUBLANES = 8
VMEM_LIMIT_BYTES = 62 * 1024 * 1024

F32 = jnp.float32
BF16 = jnp.bfloat16


def _params(*sem):
    return pltpu.CompilerParams(dimension_semantics=sem, vmem_limit_bytes=VMEM_LIMIT_BYTES)


def _tile(n, pref, mult=LANES):
    if n <= pref:
        return n
    t = (pref // mult) * mult
    while t > mult and n % t:
        t -= mult
    assert n % t == 0, (n, pref)
    return t


def _log_sigmoid(z):
    return jnp.minimum(z, 0.0) - jnp.log(1.0 + jnp.exp(-jnp.abs(z)))


def _rms_kernel(x_ref, g_ref, *rest):
    x = x_ref[...]
    ms = jnp.mean(x * x, axis=-1, keepdims=True)
    h = x * lax.rsqrt(ms + EPS) * g_ref[...]
    if len(rest) == 1:
        (o_ref,) = rest
        o_ref[...] = h.astype(o_ref.dtype)
    else:
        wlf_ref, b_ref, w_ref, o_ref, lf_ref, p_ref = rest
        hb = h.astype(o_ref.dtype)
        o_ref[...] = hb
        lf_ref[...] = _log_sigmoid(jnp.dot(hb, wlf_ref[...], preferred_element_type=F32) + b_ref[...])
        p_ref[...] = jnp.dot(hb, w_ref[...], preferred_element_type=F32).astype(p_ref.dtype)


def _rmsnorm(x, g, out_dtype, heads=None):
    m, d = x.shape
    tr = _tile(m, 256 if heads is None else 512, 16)
    in_specs = [pl.BlockSpec((tr, d), lambda i: (i, 0)),
                pl.BlockSpec((1, d), lambda i: (0, 0))]
    args = [x, g.reshape(1, d)]
    out_specs = [pl.BlockSpec((tr, d), lambda i: (i, 0))]
    out_shape = [jax.ShapeDtypeStruct((m, d), out_dtype)]
    if heads is not None:
        w_lf, b_lf, w, (col0, n) = heads
        assert col0 % n == 0, (col0, n)
        in_specs += [pl.BlockSpec((d, LANES), lambda i: (0, 0)), pl.BlockSpec((1, LANES), lambda i: (0, 0)),
                     pl.BlockSpec((d, n), lambda i: (0, col0 // n), pipeline_mode=pl.Buffered(1))]
        args += [w_lf, b_lf, w]
        out_specs += [pl.BlockSpec((tr, LANES), lambda i: (i, 0)), pl.BlockSpec((tr, n), lambda i: (i, 0))]
        out_shape += [jax.ShapeDtypeStruct((m, LANES), F32), jax.ShapeDtypeStruct((m, n), F32)]
    outs = pl.pallas_call(
        _rms_kernel,
        grid=(m // tr,),
        in_specs=in_specs,
        out_specs=out_specs,
        out_shape=out_shape,
        compiler_params=_params("arbitrary"),
        name="rmsnorm",
    )(*args)
    return outs[0] if heads is None else tuple(outs)


def _side_cast_specs(w, grid):
    steps = 1
    for g in grid:
        steps *= g
    rows, cols = w.shape
    if rows % steps or (rows // steps) % 16:
        return None

    def index_map(*idx):
        lin = idx[0]
        for g, i in zip(grid[1:], idx[1:]):
            lin = lin * g + i
        return (lin, 0)

    spec = pl.BlockSpec((rows // steps, cols), index_map)
    return spec, jax.ShapeDtypeStruct((rows, cols), BF16)


def _mm_kernel(a_ref, w_ref, *rest, epilogue, side):
    if side:
        side_in, o_ref, side_out = rest
        side_out[...] = side_in[...].astype(BF16)
    else:
        (o_ref,) = rest
    acc = jnp.dot(a_ref[...], w_ref[...], preferred_element_type=F32)
    if epilogue == "relu2":
        u = jnp.maximum(acc, 0.0)
        o_ref[...] = (u * u).astype(o_ref.dtype)
    else:
        o_ref[...] = acc.astype(o_ref.dtype)


def _matmul(a, w, out_dtype, epilogue="cast", tm_pref=1024, tn_pref=1024, cast_w=None, w_cols=None):
    m, k = a.shape
    col0, n = (0, w.shape[1]) if w_cols is None else w_cols
    tm = _tile(m, tm_pref, 16)
    tn = _tile(n, tn_pref)
    assert col0 % tn == 0, (col0, tn)
    j0 = col0 // tn
    grid = (m // tm, n // tn)
    in_specs = [pl.BlockSpec((tm, k), lambda i, j: (i, 0)),
                pl.BlockSpec((k, tn), lambda i, j: (0, j0 + j))]
    args = [a, w]
    out_specs = [pl.BlockSpec((tm, tn), lambda i, j: (i, j))]
    out_shape = [jax.ShapeDtypeStruct((m, n), out_dtype)]
    side = _side_cast_specs(cast_w, grid) if cast_w is not None else None
    if side:
        in_specs.append(side[0])
        args.append(cast_w)
        out_specs.append(side[0])
        out_shape.append(side[1])
    outs = list(pl.pallas_call(
        functools.partial(_mm_kernel, epilogue=epilogue, side=bool(side)),
        grid=grid,
        in_specs=in_specs,
        out_specs=out_specs,
        out_shape=out_shape,
        compiler_params=_params("arbitrary", "arbitrary"),
        name="matmul_" + epilogue,
    )(*args))
    if cast_w is not None and not side:
        outs.append(cast_w.astype(BF16))
    return outs[0] if len(outs) == 1 else tuple(outs)


def _mix_mm_kernel(o_ref, gf_ref, r_ref, w_ref, res_ref, gn_ref, out_ref, h_ref, a_scr, x_scr, *, fw):
    j = pl.program_id(1)
    nj, _, tn = x_scr.shape

    @pl.when(j == 0)
    def _():
        o = o_ref[...]
        ms = jnp.mean(o * o, axis=-1, keepdims=True)
        a_scr[:, :fw] = (o * lax.rsqrt(ms + EPS) * gf_ref[...]).astype(BF16)
        a_scr[:, fw:] = r_ref[...]

    x = res_ref[...] + jnp.dot(a_scr[...], w_ref[...], preferred_element_type=F32)
    out_ref[...] = x
    x_scr[j] = x

    @pl.when(j == nj - 1)
    def _():
        sq = x_scr[0] * x_scr[0]
        ssum = jnp.sum(sq, axis=-1, keepdims=True)
        for t in range(1, nj):
            xt = x_scr[t]
            ssum = ssum + jnp.sum(xt * xt, axis=-1, keepdims=True)
        rstd = lax.rsqrt(ssum / (nj * tn) + EPS)
        for t in range(nj):
            cols = slice(t * tn, (t + 1) * tn)
            h_ref[:, cols] = (x_scr[t] * rstd * gn_ref[:, cols]).astype(h_ref.dtype)


def _out_proj(o_fox, g_fox, r, w, res, g_next):
    m, fw = o_fox.shape
    rw = r.shape[1]
    n = w.shape[1]
    tm = _tile(m, 512, 16)
    tn = _tile(n, 1024)
    return pl.pallas_call(
        functools.partial(_mix_mm_kernel, fw=fw),
        grid=(m // tm, n // tn),
        in_specs=[pl.BlockSpec((tm, fw), lambda i, j: (i, 0)),
                  pl.BlockSpec((1, fw), lambda i, j: (0, 0)),
                  pl.BlockSpec((tm, rw), lambda i, j: (i, 0)),
                  pl.BlockSpec((fw + rw, tn), lambda i, j: (0, j)),
                  pl.BlockSpec((tm, tn), lambda i, j: (i, j)),
                  pl.BlockSpec((1, n), lambda i, j: (0, 0))],
        out_specs=[pl.BlockSpec((tm, tn), lambda i, j: (i, j)),
                   pl.BlockSpec((tm, n), lambda i, j: (i, 0))],
        out_shape=[jax.ShapeDtypeStruct((m, n), F32), jax.ShapeDtypeStruct((m, n), BF16)],
        scratch_shapes=[pltpu.VMEM((tm, fw + rw), BF16), pltpu.VMEM((n // tn, tm, tn), F32)],
        compiler_params=_params("arbitrary", "arbitrary"),
        name="out_proj",
    )(o_fox, g_fox.reshape(1, fw), r, w, res, g_next.reshape(1, n))


def _mm_acc_kernel(a_ref, w_ref, res_ref, out_ref):
    @pl.when(pl.program_id(2) == 0)
    def _():
        out_ref[...] = res_ref[...]

    out_ref[...] += jnp.dot(a_ref[...], w_ref[...], preferred_element_type=F32)


def _matmul_residual_ksplit(a, w, res):
    m, k = a.shape
    n = w.shape[1]
    tm = _tile(m, 1024, 16)
    tn = _tile(n, 1024)
    tk = _tile(k, 4096)
    return pl.pallas_call(
        _mm_acc_kernel,
        grid=(m // tm, n // tn, k // tk),
        in_specs=[pl.BlockSpec((tm, tk), lambda i, j, kk: (i, kk)),
                  pl.BlockSpec((tk, tn), lambda i, j, kk: (kk, j)),
                  pl.BlockSpec((tm, tn), lambda i, j, kk: (i, j))],
        out_specs=pl.BlockSpec((tm, tn), lambda i, j, kk: (i, j)),
        out_shape=jax.ShapeDtypeStruct((m, n), F32),
        compiler_params=_params("arbitrary", "arbitrary", "arbitrary"),
        name="down_proj",
    )(a, w, res)


def _cumsum_kernel(x_ref, o_ref, *aug_ref, bias_scale):
    x = x_ref[...]
    h, t = x.shape
    lane = lax.broadcasted_iota(jnp.int32, x.shape, 1)
    d = 1
    while d < t:
        x = x + jnp.where(lane >= d, pltpu.roll(x, d, axis=1), 0.0)
        d *= 2
    o_ref[...] = x
    if aug_ref:
        (a_ref,) = aug_ref
        y = x * bias_scale
        hi = y.astype(BF16).astype(F32)
        r1 = y - hi
        mid = r1.astype(BF16).astype(F32)
        lo = r1 - mid
        rows = a_ref.shape[0]
        base = jnp.concatenate([hi, mid, lo, jnp.zeros((rows - 3 * h, t), F32)], axis=0)
        row = lax.broadcasted_iota(jnp.int32, base.shape, 0)
        a_ref[...] = jnp.where(row == 3 * h, 1.0, base).astype(BF16)


def _cumsum_lanes(x, bias_scale=None):
    b, h, t = x.shape
    out_specs = [pl.BlockSpec((None, h, t), lambda i: (i, 0, 0))]
    out_shape = [jax.ShapeDtypeStruct((b, h, t), F32)]
    if bias_scale is not None:
        assert 3 * h < LANES, h
        out_specs.append(pl.BlockSpec((None, LANES, t), lambda i: (i, 0, 0)))
        out_shape.append(jax.ShapeDtypeStruct((b, LANES, t), BF16))
    outs = pl.pallas_call(
        functools.partial(_cumsum_kernel, bias_scale=bias_scale),
        grid=(b,),
        in_specs=[pl.BlockSpec((None, h, t), lambda i: (i, 0, 0))],
        out_specs=out_specs,
        out_shape=out_shape,
        compiler_params=_params("arbitrary"),
        name="cumsum_logf",
    )(x)
    return outs[0] if bias_scale is None else tuple(outs)


def _qk(q_bf16, k_bf16):
    return lax.dot_general(q_bf16, k_bf16, (((1,), (1,)), ((), ())), preferred_element_type=F32)


def _fox_prompt_kernel(q_ref, k_ref, v_ref, ac_ref, ar_ref, *rest, tq, heads, side):
    if side:
        side_in, o_ref, side_out, qa_scr, kt_scr, vb_scr = rest
        side_out[...] = side_in[...].astype(BF16)
    else:
        o_ref, qa_scr, kt_scr, vb_scr = rest
    hp = vb_scr.shape[0]
    seq = q_ref.shape[0]
    r = lax.broadcasted_iota(jnp.int32, (LANES, LANES), 0)
    c = lax.broadcasted_iota(jnp.int32, (LANES, LANES), 1)
    for hh in range(hp):
        h = pl.program_id(1) * hp + hh
        cols = slice(hh * HD_FOX, (hh + 1) * HD_FOX)
        pick_q = (((r == h) & (c == 0)) | ((r == heads + h) & (c == 1)) | ((r == 2 * heads + h) & (c == 2))
                  | ((r == 3 * heads) & (c >= 3) & (c < 6)))
        pick_k = (((r == 3) & (c == h)) | ((r == 4) & (c == heads + h)) | ((r == 5) & (c == 2 * heads + h)))
        sel_q = jnp.where(pick_q, 1.0, 0.0).astype(BF16)
        sel_k = (jnp.where((r < 3) & (c == 3 * heads), 1.0, 0.0) - jnp.where(pick_k, 1.0, 0.0)).astype(BF16)
        qa_scr[hh, :, :HD_FOX] = q_ref[:, cols]
        qa_scr[hh, :, HD_FOX:] = jnp.dot(ac_ref[...], sel_q, preferred_element_type=F32).astype(BF16)
        kt_scr[hh, :HD_FOX, :] = k_ref[:, cols].T.astype(BF16)
        kt_scr[hh, HD_FOX:, :] = jnp.dot(sel_k, ar_ref[...], preferred_element_type=F32).astype(BF16)
        vb_scr[hh] = v_ref[:, cols].astype(BF16)
    c2 = (HD_FOX ** -0.5) * 1.4426950408889634
    row = lax.broadcasted_iota(jnp.int32, (tq, tq), 0)
    col = lax.broadcasted_iota(jnp.int32, (tq, tq), 1)
    causal = col <= row
    nq = seq // tq
    for pair in range((nq + 1) // 2):
        blocks = sorted({pair, nq - 1 - pair})
        chains = [(hh, qi) for qi in blocks for hh in range(hp)]
        qa = {c: qa_scr[c[0], c[1] * tq:(c[1] + 1) * tq, :] for c in chains}
        s_d = {(hh, qi): jnp.where(causal, jnp.dot(qa[hh, qi], kt_scr[hh, :, qi * tq:(qi + 1) * tq],
                                                   preferred_element_type=F32), -jnp.inf) for hh, qi in chains}
        m = {c: jnp.max(s_d[c], axis=1, keepdims=True) for c in chains}
        early = [c for c in chains if c[1] > 0]
        s_o = {(hh, qi): jnp.dot(qa[hh, qi], kt_scr[hh, :, :qi * tq], preferred_element_type=F32)
               for hh, qi in early}
        for c in early:
            m[c] = jnp.maximum(m[c], jnp.max(s_o[c], axis=1, keepdims=True))
        p_o = {c: jnp.exp2((s_o[c] - m[c]) * c2) for c in early}
        p_d = {c: jnp.exp2((s_d[c] - m[c]) * c2) for c in chains}
        for hh, qi in chains:
            lo, hi = qi * tq, (qi + 1) * tq
            l = jnp.sum(p_d[hh, qi], axis=1, keepdims=True)
            acc = jnp.dot(p_d[hh, qi].astype(BF16), vb_scr[hh, lo:hi, :], preferred_element_type=F32)
            if qi > 0:
                l = l + jnp.sum(p_o[hh, qi], axis=1, keepdims=True)
                acc = acc + jnp.dot(p_o[hh, qi].astype(BF16), vb_scr[hh, :lo, :], preferred_element_type=F32)
            o_ref[lo:hi, hh * HD_FOX:(hh + 1) * HD_FOX] = acc / l


def _fox_prompt(q, k, v, aug_cols, aug_rows, batch, seq, heads, cast_w):
    m, fw = k.shape
    tq = _tile(seq, 256)
    hp = 2 if heads % 2 == 0 else 1
    grid = (batch, heads // hp)
    head_blk = lambda b, h: (b, h)
    in_specs = [pl.BlockSpec((seq, hp * HD_FOX), head_blk),
                pl.BlockSpec((seq, hp * HD_FOX), head_blk),
                pl.BlockSpec((seq, hp * HD_FOX), head_blk),
                pl.BlockSpec((seq, LANES), lambda b, h: (b, 0)),
                pl.BlockSpec((None, LANES, seq), lambda b, h: (b, 0, 0))]
    args = [q, k, v, aug_cols, aug_rows]
    out_specs = [pl.BlockSpec((seq, hp * HD_FOX), head_blk)]
    out_shape = [jax.ShapeDtypeStruct((m, fw), F32)]
    side = _side_cast_specs(cast_w, grid)
    if side:
        in_specs.append(side[0])
        args.append(cast_w)
        out_specs.append(side[0])
        out_shape.append(side[1])
    outs = pl.pallas_call(
        functools.partial(_fox_prompt_kernel, tq=tq, heads=heads, side=bool(side)),
        grid=grid,
        in_specs=in_specs,
        out_specs=out_specs,
        out_shape=out_shape,
        scratch_shapes=[pltpu.VMEM((hp, seq, 2 * HD_FOX), BF16), pltpu.VMEM((hp, 2 * HD_FOX, seq), BF16),
                        pltpu.VMEM((hp, seq, HD_FOX), BF16)],
        compiler_params=_params("arbitrary", "arbitrary"),
        name="fox_prompt",
    )(*args)
    return outs[0], (outs[1] if side else cast_w.astype(BF16))


def _every_eighth_row(ref, j):
    n, s, d = ref.shape
    return ref.reshape(n * s, d)[pl.ds(j, n, stride=s), :]


def _fox_sample_kernel(*refs, heads):
    ng = heads // SUBLANES
    q_ref = refs[0]
    k_refs = refs[1:1 + ng]
    v_refs = refs[1 + ng:1 + 2 * ng]
    kn_ref, vn_ref, cq_ref, ckc_ref, ckn_ref, o_ref, m_scr, l_scr, acc_scr, s_scr, p_scr = refs[1 + 2 * ng:]
    ki = pl.program_id(1)
    last = pl.num_programs(1) - 1
    scale = HD_FOX ** -0.5
    t = q_ref.shape[0]
    head_cols = lambda h: slice(h * HD_FOX, (h + 1) * HD_FOX)
    head_rows = lambda h: slice(h * t, (h + 1) * t)

    @pl.when(ki == 0)
    def _():
        m_scr[...] = jnp.full(m_scr.shape, -jnp.inf, F32)
        l_scr[...] = jnp.zeros(l_scr.shape, F32)
        acc_scr[...] = jnp.zeros(acc_scr.shape, F32)

    def cached(group_refs, h):
        return _every_eighth_row(group_refs[h // SUBLANES], h % SUBLANES).astype(BF16)

    def update(load_k, load_v, ck_ref, n, causal):
        for h in range(heads):
            s = _qk(q_ref[:, head_cols(h)], load_k(h)) * scale + cq_ref[:, h:h + 1] - ck_ref[h:h + 1, :]
            if causal:
                row = lax.broadcasted_iota(jnp.int32, s.shape, 0)
                col = lax.broadcasted_iota(jnp.int32, s.shape, 1)
                s = jnp.where(col <= row, s, -jnp.inf)
            s_scr[head_rows(h), :n] = s
        s = s_scr[:, :n]
        m_prev = m_scr[:, :1]
        m_new = jnp.maximum(m_prev, jnp.max(s, axis=1, keepdims=True))
        alpha = jnp.exp(m_prev - m_new)
        p = jnp.exp(s - m_new)
        l_new = alpha * l_scr[:, :1] + jnp.sum(p, axis=1, keepdims=True)
        m_scr[...] = jnp.broadcast_to(m_new, m_scr.shape)
        l_scr[...] = jnp.broadcast_to(l_new, l_scr.shape)
        p_scr[:, :n] = p.astype(BF16)
        for h in range(heads):
            pv = jnp.dot(p_scr[head_rows(h), :n], load_v(h), preferred_element_type=F32)
            acc_scr[:, head_cols(h)] = alpha[head_rows(h)] * acc_scr[:, head_cols(h)] + pv

    @pl.when(ki < last)
    def _():
        update(functools.partial(cached, k_refs), functools.partial(cached, v_refs), ckc_ref,
               s_scr.shape[1], causal=False)

    @pl.when(ki == last)
    def _():
        new_rows = lambda ref, h: ref[:, head_cols(h)].astype(BF16)
        update(functools.partial(new_rows, kn_ref), functools.partial(new_rows, vn_ref), ckn_ref, t, causal=True)
        for h in range(heads):
            o_ref[:, head_cols(h)] = acc_scr[:, head_cols(h)] / l_scr[head_rows(h), :1]


def _fox_sample(q, kc, vc, kn, vn, cq, ckc, ckn, batch, t_new, heads):
    past = kc.shape[1]
    fw = heads * HD_FOX
    assert heads % SUBLANES == 0, heads
    ng = heads // SUBLANES
    tk = _tile(past, 1024)
    nkc = past // tk
    kc = kc.reshape(batch, past, ng, SUBLANES, HD_FOX)
    vc = vc.reshape(batch, past, ng, SUBLANES, HD_FOX)
    row_blk = pl.BlockSpec((t_new, fw), lambda b, ki: (b, 0))
    group_blk = [pl.BlockSpec((None, tk, None, SUBLANES, HD_FOX),
                              functools.partial(lambda b, ki, g: (b, jnp.minimum(ki, nkc - 1), g, 0, 0), g=g))
                 for g in range(ng)]
    return pl.pallas_call(
        functools.partial(_fox_sample_kernel, heads=heads),
        grid=(batch, nkc + 1),
        in_specs=[row_blk] + group_blk + group_blk + [
            row_blk, row_blk,
            pl.BlockSpec((t_new, heads), lambda b, ki: (b, 0)),
            pl.BlockSpec((None, heads, tk), lambda b, ki: (b, 0, jnp.minimum(ki, nkc - 1))),
            pl.BlockSpec((None, heads, t_new), lambda b, ki: (b, 0, 0))],
        out_specs=row_blk,
        out_shape=jax.ShapeDtypeStruct((batch * t_new, fw), F32),
        scratch_shapes=[pltpu.VMEM((heads * t_new, LANES), F32), pltpu.VMEM((heads * t_new, LANES), F32),
                        pltpu.VMEM((t_new, fw), F32),
                        pltpu.VMEM((heads * t_new, tk), F32), pltpu.VMEM((heads * t_new, tk), BF16)],
        compiler_params=_params("arbitrary", "arbitrary"),
        name="fox_sample",
    )(q, *([kc] * ng), *([vc] * ng), kn, vn, cq, ckc, ckn)


def _retention_kernel(q_ref, k_ref, v_ref, g_ref, cos_ref, sin_ref, gro_ref, lg_ref, s0_ref, *rest,
                      blk, heads, side):
    if side:
        side_in, r_ref, sout_ref, side_out = rest
        side_out[...] = side_in[...].astype(BF16)
    else:
        r_ref, sout_ref = rest
    seq = q_ref.shape[0]
    half = DK_RET // 2
    ri = lax.broadcasted_iota(jnp.int32, (blk, blk), 0)
    ci = lax.broadcasted_iota(jnp.int32, (blk, blk), 1)
    dist = jnp.abs(ri - ci).astype(F32)
    visible = ci // CHUNK <= ri // CHUNK
    idx = lax.broadcasted_iota(jnp.int32, (blk, 1), 0).astype(F32)
    lgs = [lg_ref[hh, :, :1] for hh in range(heads)]
    decay = [jnp.where(visible, jnp.exp(lg * dist), 0.0) for lg in lgs]
    cross = [jnp.exp(lg * (idx + 1.0)) for lg in lgs]
    k_decay = [jnp.exp(lg * (blk - 1.0 - idx)) for lg in lgs]
    s_decay = [jnp.exp(lg * blk) for lg in lgs]
    state = [s0_ref[hh] for hh in range(heads)]
    for t in range(seq // blk):
        rows = slice(t * blk, (t + 1) * blk)
        cos = cos_ref[rows, :]
        sin = sin_ref[rows, :]
        for hh in range(heads):
            qk_cols = slice(hh * DK_RET, (hh + 1) * DK_RET)
            v_cols = slice(hh * DV_RET, (hh + 1) * DV_RET)
            q = q_ref[rows, qk_cols]
            k = k_ref[rows, qk_cols]
            q = q * cos + pltpu.roll(q, half, axis=1) * sin
            k = (k * cos + pltpu.roll(k, half, axis=1) * sin) * (DK_RET ** -0.5)
            v = v_ref[rows, v_cols]
            qb = q.astype(BF16)
            a = _qk(qb, k.astype(BF16)) * decay[hh]
            o = jnp.dot(a.astype(BF16), v, preferred_element_type=F32)
            o = o + jnp.dot(qb, state[hh].astype(BF16), preferred_element_type=F32) * cross[hh]
            state[hh] = s_decay[hh] * state[hh] + lax.dot_general(
                (k * k_decay[hh]).astype(BF16), v, (((0,), (0,)), ((), ())), preferred_element_type=F32)
            ms = jnp.mean(o * o, axis=-1, keepdims=True)
            g = g_ref[rows, v_cols]
            silu = g / (1.0 + jnp.exp(-g))
            r_ref[rows, v_cols] = (o * lax.rsqrt(ms + EPS) * gro_ref[:, v_cols] * silu).astype(r_ref.dtype)
    for hh in range(heads):
        sout_ref[hh] = state[hh]


def _retention(qkg, qv, cos, sin, g_ret_out, lg, s0, batch, seq, heads, blk, heads_per_step, fox_width,
               cast_w=None):
    m = qkg.shape[0]
    hps = heads_per_step
    groups = heads // hps
    assert fox_width % (hps * DV_RET) == 0, (fox_width, hps)
    v_off = fox_width // (hps * DV_RET)
    grid = (batch, groups)
    row = lambda b, h: (b, h)
    state = pl.BlockSpec((None, hps, DK_RET, DV_RET), lambda b, h: (b, h, 0, 0))
    in_specs = [pl.BlockSpec((seq, hps * DK_RET), row),
                pl.BlockSpec((seq, hps * DK_RET), lambda b, h: (b, groups + h)),
                pl.BlockSpec((seq, hps * DV_RET), lambda b, h: (b, v_off + h)),
                pl.BlockSpec((seq, hps * DV_RET), lambda b, h: (b, groups + h)),
                pl.BlockSpec((seq, DK_RET), lambda b, h: (0, 0)),
                pl.BlockSpec((seq, DK_RET), lambda b, h: (0, 0)),
                pl.BlockSpec((1, hps * DV_RET), lambda b, h: (0, h)),
                pl.BlockSpec((hps, 1, LANES), lambda b, h: (h, 0, 0)),
                state]
    args = [qkg, qkg, qv, qkg, cos, sin, g_ret_out.reshape(1, -1), lg, s0]
    out_specs = [pl.BlockSpec((seq, hps * DV_RET), row), state]
    out_shape = [jax.ShapeDtypeStruct((m, heads * DV_RET), BF16),
                 jax.ShapeDtypeStruct((batch, heads, DK_RET, DV_RET), F32)]
    side = _side_cast_specs(cast_w, grid) if cast_w is not None else None
    if side:
        in_specs.append(side[0])
        args.append(cast_w)
        out_specs.append(side[0])
        out_shape.append(side[1])
    outs = pl.pallas_call(
        functools.partial(_retention_kernel, blk=blk, heads=hps, side=bool(side)),
        grid=grid,
        in_specs=in_specs,
        out_specs=out_specs,
        out_shape=out_shape,
        compiler_params=_params("arbitrary", "arbitrary"),
        name="retention",
    )(*args)
    if cast_w is None:
        return outs[0], outs[1]
    return outs[0], outs[1], (outs[2] if side else cast_w.astype(BF16))


def _rotary_tables(pos):
    half = DK_RET // 2
    inv = ROPE_BASE ** (-jnp.arange(half, dtype=F32) / half)
    ang = pos.astype(F32)[:, None] * inv[None, :]
    cos = jnp.cos(ang)
    sin = jnp.sin(ang)
    return jnp.concatenate([cos, cos], axis=-1), jnp.concatenate([-sin, sin], axis=-1)


def _feature_rows_kernel(w_ref, o_ref, *, valid):
    x = w_ref[...]
    if valid < x.shape[0]:
        row = lax.broadcasted_iota(jnp.int32, x.shape, 0)
        x = jnp.where(row < valid, x, 0.0)
    o_ref[...] = x.T.astype(BF16)


def _feature_rows(w_t, layer, segments, rows_per_step, valid=None):
    _, _, d = w_t.shape
    rb = rows_per_step
    starts = [s + k * rb for s, n in segments for k in range(n // rb)]
    assert all(n % rb == 0 for _, n in segments) and all(s % SUBLANES == 0 for s in starts), segments

    def index_map(i):
        tile_row = starts[-1] // SUBLANES
        for k in range(len(starts) - 2, -1, -1):
            tile_row = jnp.where(i <= k, starts[k] // SUBLANES, tile_row)
        return (layer, tile_row * SUBLANES, 0)

    return pl.pallas_call(
        functools.partial(_feature_rows_kernel, valid=rb if valid is None else valid),
        grid=(len(starts),),
        in_specs=[pl.BlockSpec((None, pl.Element(rb), pl.Element(d)), index_map)],
        out_specs=pl.BlockSpec((d, rb), lambda i: (0, i)),
        out_shape=jax.ShapeDtypeStruct((d, rb * len(starts)), BF16),
        compiler_params=_params("arbitrary"),
        name="w_in_rows",
    )(w_t)


def _in_proj_weights(w_in, layer, b_forget, h_fox, h_ret):
    fw = h_fox * HD_FOX
    qkw = h_ret * DK_RET
    vw = h_ret * DV_RET
    o1, o2, o3 = fw, 2 * fw, 3 * fw
    o4 = o3 + h_fox
    o6 = o4 + 2 * qkw
    o7 = o6 + vw
    w_t = jnp.swapaxes(w_in, 1, 2)
    groups = dict(qv=[(0, fw), (o6, vw)], k_f=[(o1, fw)], v_f=[(o2, fw)], qkg=[(o4, 2 * qkw), (o7, vw)])
    cols, col0 = {}, 0
    for name, segs in groups.items():
        width = sum(n for _, n in segs)
        cols[name] = (col0, width)
        col0 += width
    return dict(
        w=_feature_rows(w_t, layer, [seg for segs in groups.values() for seg in segs], 512), cols=cols,
        logf=_feature_rows(w_t, layer, [(o3, LANES)], LANES, valid=h_fox),
        b_logf=jnp.pad(b_forget, (0, LANES - h_fox)).reshape(1, LANES).astype(F32))


def _project(x2d, g_attn, w):
    h, logf, k_f = _rmsnorm(x2d, g_attn, BF16, heads=(w["logf"], w["b_logf"], w["w"], w["cols"]["k_f"]))
    mm = lambda name, dtype: _matmul(h, w["w"], dtype, w_cols=w["cols"][name])
    return dict(qv=mm("qv", BF16), k_f=k_f, v_f=mm("v_f", F32), qkg=mm("qkg", F32), logf=logf)


def _finish_layer(x2d, o_fox, r, g_fox_out, g_mlp, w_out_b, w_up_b, w_down):
    x1, h2 = _out_proj(o_fox, g_fox_out, r, w_out_b, x2d, g_mlp)
    if w_down.dtype == BF16:
        u = _matmul(h2, w_up_b, BF16, epilogue="relu2")
    else:
        u, w_down = _matmul(h2, w_up_b, BF16, epilogue="relu2", cast_w=w_down)
    return _matmul_residual_ksplit(u, w_down, x1), w_down


def _retention_log_gamma(h_ret):
    lg = jnp.log1p(-jnp.exp2(-5.0 - jnp.arange(h_ret, dtype=F32)))
    return jnp.broadcast_to(lg[:, None, None], (h_ret, 1, LANES))


def kernel(x_prompt, x_sample, cache_fox_k, cache_fox_v, cache_fox_logf, state_ret,
           g_attn, w_in, b_forget, g_fox_out, g_ret_out, w_out, g_mlp, w_up, w_down, g_final):
    bp, seq, d = x_prompt.shape
    bs, t_new, _ = x_sample.shape
    depth, _, past, h_fox, _ = cache_fox_k.shape
    h_ret = state_ret.shape[2]
    fw = h_fox * HD_FOX
    dtype = x_prompt.dtype

    lg = _retention_log_gamma(h_ret)
    cos_p, sin_p = _rotary_tables(jnp.arange(seq))
    cos_s, sin_s = _rotary_tables(past + jnp.arange(t_new))
    blk_p = _tile(seq, 256, CHUNK)

    yp = x_prompt.reshape(bp * seq, d)
    ys = x_sample.reshape(bs * t_new, d)
    outs = [[] for _ in range(8)]
    for l in range(depth):
        w = _in_proj_weights(w_in, l, b_forget[l], h_fox, h_ret)

        p = _project(yp, g_attn[l], w)
        logf_p = p["logf"][:, :h_fox]
        _, aug_rows = _cumsum_lanes(jnp.swapaxes(logf_p.reshape(bp, seq, h_fox), 1, 2),
                                    bias_scale=HD_FOX ** 0.5)
        aug_cols = jnp.swapaxes(aug_rows, 1, 2).reshape(bp * seq, LANES)
        o_fox, w_up_b = _fox_prompt(p["qv"], p["k_f"], p["v_f"], aug_cols, aug_rows, bp, seq, h_fox, w_up[l])
        s0 = jnp.zeros((bp, h_ret, DK_RET, DV_RET), F32)
        r, s_fin, w_out_b = _retention(p["qkg"], p["qv"], cos_p, sin_p, g_ret_out[l], lg,
                                       s0, bp, seq, h_ret, blk_p, 2 if h_ret % 2 == 0 else 1, fw,
                                       cast_w=w_out[l])
        yp, w_down_b = _finish_layer(yp, o_fox, r, g_fox_out[l], g_mlp[l], w_out_b, w_up_b, w_down[l])
        outs[0].append(p["k_f"].reshape(bp, seq, h_fox, HD_FOX))
        outs[1].append(p["v_f"].reshape(bp, seq, h_fox, HD_FOX))
        outs[2].append(logf_p.reshape(bp, seq, h_fox).astype(dtype))
        outs[3].append(s_fin.astype(dtype))

        p = _project(ys, g_attn[l], w)
        logf_s = p["logf"][:, :h_fox]
        total = past + t_new
        padded = -(-total // LANES) * LANES
        logf_all = jnp.concatenate(
            [jnp.swapaxes(cache_fox_logf[l].astype(F32), 1, 2),
             jnp.swapaxes(logf_s.reshape(bs, t_new, h_fox), 1, 2),
             jnp.zeros((bs, h_fox, padded - total), F32)], axis=2)
        c_all = _cumsum_lanes(logf_all)
        ckc = c_all[:, :, :past]
        ckn = c_all[:, :, past:total]
        cq = jnp.swapaxes(ckn, 1, 2).reshape(bs * t_new, h_fox)
        o_fox = _fox_sample(p["qv"], cache_fox_k[l], cache_fox_v[l], p["k_f"], p["v_f"],
                            cq, ckc, ckn, bs, t_new, h_fox)
        r, s_new = _retention(p["qkg"], p["qv"], cos_s, sin_s, g_ret_out[l], lg,
                              state_ret[l].astype(F32), bs, t_new, h_ret, t_new, h_ret, fw)
        ys, _ = _finish_layer(ys, o_fox, r, g_fox_out[l], g_mlp[l], w_out_b, w_up_b, w_down_b)
        outs[4].append(p["k_f"].reshape(bs, t_new, h_fox, HD_FOX))
        outs[5].append(p["v_f"].reshape(bs, t_new, h_fox, HD_FOX))
        outs[6].append(logf_s.reshape(bs, t_new, h_fox).astype(dtype))
        outs[7].append(s_new.astype(dtype))

    y_prompt = _rmsnorm(yp, g_final, dtype).reshape(bp, seq, d)
    y_sample = _rmsnorm(ys, g_final, dtype).reshape(bs, t_new, d)
    return (y_prompt, y_sample) + tuple(jnp.stack(o) for o in outs)
```

```python
import functools

import jax
import jax.numpy as jnp
from jax import lax
from jax.experimental import pallas as pl
from jax.experimental.pallas import tpu as pltpu

CHUNK = 64
HD_FOX = 128
DK_RET = 128
DV_RET = 256
ROPE_BASE = 10000.0
EPS = 1e-6

LANES = 128
SUBLANES = 8
VMEM_LIMIT_BYTES = 62 * 1024 * 1024

F32 = jnp.float32
BF16 = jnp.bfloat16


def _params(*sem):
    return pltpu.CompilerParams(dimension_semantics=sem, vmem_limit_bytes=VMEM_LIMIT_BYTES)


def _tile(n, pref, mult=LANES):
    if n <= pref:
        return n
    t = (pref // mult) * mult
    while t > mult and n % t:
        t -= mult
    assert n % t == 0, (n, pref)
    return t


def _log_sigmoid(z):
    return jnp.minimum(z, 0.0) - jnp.log(1.0 + jnp.exp(-jnp.abs(z)))


def _rms_kernel(x_ref, g_ref, *rest):
    x = x_ref[...]
    ms = jnp.mean(x * x, axis=-1, keepdims=True)
    h = x * lax.rsqrt(ms + EPS) * g_ref[...]
    if len(rest) == 1:
        (o_ref,) = rest
        o_ref[...] = h.astype(o_ref.dtype)
    else:
        wlf_ref, b_ref, w_ref, o_ref, lf_ref, p_ref = rest
        hb = h.astype(o_ref.dtype)
        o_ref[...] = hb
        lf_ref[...] = _log_sigmoid(jnp.dot(hb, wlf_ref[...], preferred_element_type=F32) + b_ref[...])
        p_ref[...] = jnp.dot(hb, w_ref[...], preferred_element_type=F32).astype(p_ref.dtype)


def _rmsnorm(x, g, out_dtype, heads=None):
    m, d = x.shape
    tr = _tile(m, 256 if heads is None else 512, 16)
    in_specs = [pl.BlockSpec((tr, d), lambda i: (i, 0)),
                pl.BlockSpec((1, d), lambda i: (0, 0))]
    args = [x, g.reshape(1, d)]
    out_specs = [pl.BlockSpec((tr, d), lambda i: (i, 0))]
    out_shape = [jax.ShapeDtypeStruct((m, d), out_dtype)]
    if heads is not None:
        w_lf, b_lf, w, (col0, n) = heads
        assert col0 % n == 0, (col0, n)
        in_specs += [pl.BlockSpec((d, LANES), lambda i: (0, 0)), pl.BlockSpec((1, LANES), lambda i: (0, 0)),
                     pl.BlockSpec((d, n), lambda i: (0, col0 // n), pipeline_mode=pl.Buffered(1))]
        args += [w_lf, b_lf, w]
        out_specs += [pl.BlockSpec((tr, LANES), lambda i: (i, 0)), pl.BlockSpec((tr, n), lambda i: (i, 0))]
        out_shape += [jax.ShapeDtypeStruct((m, LANES), F32), jax.ShapeDtypeStruct((m, n), F32)]
    outs = pl.pallas_call(
        _rms_kernel,
        grid=(m // tr,),
        in_specs=in_specs,
        out_specs=out_specs,
        out_shape=out_shape,
        compiler_params=_params("arbitrary"),
        name="rmsnorm",
    )(*args)
    return outs[0] if heads is None else tuple(outs)


def _side_cast_specs(w, grid):
    steps = 1
    for g in grid:
        steps *= g
    rows, cols = w.shape
    if rows % steps or (rows // steps) % 16:
        return None

    def index_map(*idx):
        lin = idx[0]
        for g, i in zip(grid[1:], idx[1:]):
            lin = lin * g + i
        return (lin, 0)

    spec = pl.BlockSpec((rows // steps, cols), index_map)
    return spec, jax.ShapeDtypeStruct((rows, cols), BF16)


def _mm_kernel(a_ref, w_ref, *rest, epilogue, side):
    if side:
        side_in, o_ref, side_out = rest
        side_out[...] = side_in[...].astype(BF16)
    else:
        (o_ref,) = rest
    acc = jnp.dot(a_ref[...], w_ref[...], preferred_element_type=F32)
    if epilogue == "relu2":
        u = jnp.maximum(acc, 0.0)
        o_ref[...] = (u * u).astype(o_ref.dtype)
    else:
        o_ref[...] = acc.astype(o_ref.dtype)


def _matmul(a, w, out_dtype, epilogue="cast", tm_pref=1024, tn_pref=1024, cast_w=None, w_cols=None):
    m, k = a.shape
    col0, n = (0, w.shape[1]) if w_cols is None else w_cols
    tm = _tile(m, tm_pref, 16)
    tn = _tile(n, tn_pref)
    assert col0 % tn == 0, (col0, tn)
    j0 = col0 // tn
    grid = (m // tm, n // tn)
    in_specs = [pl.BlockSpec((tm, k), lambda i, j: (i, 0)),
                pl.BlockSpec((k, tn), lambda i, j: (0, j0 + j))]
    args = [a, w]
    out_specs = [pl.BlockSpec((tm, tn), lambda i, j: (i, j))]
    out_shape = [jax.ShapeDtypeStruct((m, n), out_dtype)]
    side = _side_cast_specs(cast_w, grid) if cast_w is not None else None
    if side:
        in_specs.append(side[0])
        args.append(cast_w)
        out_specs.append(side[0])
        out_shape.append(side[1])
    outs = list(pl.pallas_call(
        functools.partial(_mm_kernel, epilogue=epilogue, side=bool(side)),
        grid=grid,
        in_specs=in_specs,
        out_specs=out_specs,
        out_shape=out_shape,
        compiler_params=_params("arbitrary", "arbitrary"),
        name="matmul_" + epilogue,
    )(*args))
    if cast_w is not None and not side:
        outs.append(cast_w.astype(BF16))
    return outs[0] if len(outs) == 1 else tuple(outs)


def _mix_mm_kernel(o_ref, gf_ref, r_ref, w_ref, res_ref, gn_ref, out_ref, h_ref, a_scr, x_scr, *, fw):
    j = pl.program_id(1)
    nj, _, tn = x_scr.shape

    @pl.when(j == 0)
    def _():
        o = o_ref[...]
        ms = jnp.mean(o * o, axis=-1, keepdims=True)
        a_scr[:, :fw] = (o * lax.rsqrt(ms + EPS) * gf_ref[...]).astype(BF16)
        a_scr[:, fw:] = r_ref[...]

    x = res_ref[...] + jnp.dot(a_scr[...], w_ref[...], preferred_element_type=F32)
    out_ref[...] = x
    x_scr[j] = x

    @pl.when(j == nj - 1)
    def _():
        sq = x_scr[0] * x_scr[0]
        ssum = jnp.sum(sq, axis=-1, keepdims=True)
        for t in range(1, nj):
            xt = x_scr[t]
            ssum = ssum + jnp.sum(xt * xt, axis=-1, keepdims=True)
        rstd = lax.rsqrt(ssum / (nj * tn) + EPS)
        for t in range(nj):
            cols = slice(t * tn, (t + 1) * tn)
            h_ref[:, cols] = (x_scr[t] * rstd * gn_ref[:, cols]).astype(h_ref.dtype)


def _out_proj(o_fox, g_fox, r, w, res, g_next):
    m, fw = o_fox.shape
    rw = r.shape[1]
    n = w.shape[1]
    tm = _tile(m, 512, 16)
    tn = _tile(n, 1024)
    return pl.pallas_call(
        functools.partial(_mix_mm_kernel, fw=fw),
        grid=(m // tm, n // tn),
        in_specs=[pl.BlockSpec((tm, fw), lambda i, j: (i, 0)),
                  pl.BlockSpec((1, fw), lambda i, j: (0, 0)),
                  pl.BlockSpec((tm, rw), lambda i, j: (i, 0)),
                  pl.BlockSpec((fw + rw, tn), lambda i, j: (0, j)),
                  pl.BlockSpec((tm, tn), lambda i, j: (i, j)),
                  pl.BlockSpec((1, n), lambda i, j: (0, 0))],
        out_specs=[pl.BlockSpec((tm, tn), lambda i, j: (i, j)),
                   pl.BlockSpec((tm, n), lambda i, j: (i, 0))],
        out_shape=[jax.ShapeDtypeStruct((m, n), F32), jax.ShapeDtypeStruct((m, n), BF16)],
        scratch_shapes=[pltpu.VMEM((tm, fw + rw), BF16), pltpu.VMEM((n // tn, tm, tn), F32)],
        compiler_params=_params("arbitrary", "arbitrary"),
        name="out_proj",
    )(o_fox, g_fox.reshape(1, fw), r, w, res, g_next.reshape(1, n))


def _mm_acc_kernel(a_ref, w_ref, res_ref, out_ref):
    @pl.when(pl.program_id(2) == 0)
    def _():
        out_ref[...] = res_ref[...]

    out_ref[...] += jnp.dot(a_ref[...], w_ref[...], preferred_element_type=F32)


def _matmul_residual_ksplit(a, w, res):
    m, k = a.shape
    n = w.shape[1]
    tm = _tile(m, 1024, 16)
    tn = _tile(n, 1024)
    tk = _tile(k, 4096)
    return pl.pallas_call(
        _mm_acc_kernel,
        grid=(m // tm, n // tn, k // tk),
        in_specs=[pl.BlockSpec((tm, tk), lambda i, j, kk: (i, kk)),
                  pl.BlockSpec((tk, tn), lambda i, j, kk: (kk, j)),
                  pl.BlockSpec((tm, tn), lambda i, j, kk: (i, j))],
        out_specs=pl.BlockSpec((tm, tn), lambda i, j, kk: (i, j)),
        out_shape=jax.ShapeDtypeStruct((m, n), F32),
        compiler_params=_params("arbitrary", "arbitrary", "arbitrary"),
        name="down_proj",
    )(a, w, res)


def _cumsum_kernel(x_ref, o_ref, *aug_ref, bias_scale):
    x = x_ref[...]
    h, t = x.shape
    lane = lax.broadcasted_iota(jnp.int32, x.shape, 1)
    d = 1
    while d < t:
        x = x + jnp.where(lane >= d, pltpu.roll(x, d, axis=1), 0.0)
        d *= 2
    o_ref[...] = x
    if aug_ref:
        (a_ref,) = aug_ref
        y = x * bias_scale
        hi = y.astype(BF16).astype(F32)
        r1 = y - hi
        mid = r1.astype(BF16).astype(F32)
        lo = r1 - mid
        rows = a_ref.shape[0]
        base = jnp.concatenate([hi, mid, lo, jnp.zeros((rows - 3 * h, t), F32)], axis=0)
        row = lax.broadcasted_iota(jnp.int32, base.shape, 0)
        a_ref[...] = jnp.where(row == 3 * h, 1.0, base).astype(BF16)


def _cumsum_lanes(x, bias_scale=None):
    b, h, t = x.shape
    out_specs = [pl.BlockSpec((None, h, t), lambda i: (i, 0, 0))]
    out_shape = [jax.ShapeDtypeStruct((b, h, t), F32)]
    if bias_scale is not None:
        assert 3 * h < LANES, h
        out_specs.append(pl.BlockSpec((None, LANES, t), lambda i: (i, 0, 0)))
        out_shape.append(jax.ShapeDtypeStruct((b, LANES, t), BF16))
    outs = pl.pallas_call(
        functools.partial(_cumsum_kernel, bias_scale=bias_scale),
        grid=(b,),
        in_specs=[pl.BlockSpec((None, h, t), lambda i: (i, 0, 0))],
        out_specs=out_specs,
        out_shape=out_shape,
        compiler_params=_params("arbitrary"),
        name="cumsum_logf",
    )(x)
    return outs[0] if bias_scale is None else tuple(outs)


def _qk(q_bf16, k_bf16):
    return lax.dot_general(q_bf16, k_bf16, (((1,), (1,)), ((), ())), preferred_element_type=F32)


def _fox_prompt_kernel(q_ref, k_ref, v_ref, ac_ref, ar_ref, *rest, tq, heads, side):
    if side:
        side_in, o_ref, side_out, qa_scr, kt_scr, vb_scr = rest
        side_out[...] = side_in[...].astype(BF16)
    else:
        o_ref, qa_scr, kt_scr, vb_scr = rest
    hp = vb_scr.shape[0]
    seq = q_ref.shape[0]
    r = lax.broadcasted_iota(jnp.int32, (LANES, LANES), 0)
    c = lax.broadcasted_iota(jnp.int32, (LANES, LANES), 1)
    for hh in range(hp):
        h = pl.program_id(1) * hp + hh
        cols = slice(hh * HD_FOX, (hh + 1) * HD_FOX)
        pick_q = (((r == h) & (c == 0)) | ((r == heads + h) & (c == 1)) | ((r == 2 * heads + h) & (c == 2))
                  | ((r == 3 * heads) & (c >= 3) & (c < 6)))
        pick_k = (((r == 3) & (c == h)) | ((r == 4) & (c == heads + h)) | ((r == 5) & (c == 2 * heads + h)))
        sel_q = jnp.where(pick_q, 1.0, 0.0).astype(BF16)
        sel_k = (jnp.where((r < 3) & (c == 3 * heads), 1.0, 0.0) - jnp.where(pick_k, 1.0, 0.0)).astype(BF16)
        qa_scr[hh, :, :HD_FOX] = q_ref[:, cols]
        qa_scr[hh, :, HD_FOX:] = jnp.dot(ac_ref[...], sel_q, preferred_element_type=F32).astype(BF16)
        kt_scr[hh, :HD_FOX, :] = k_ref[:, cols].T.astype(BF16)
        kt_scr[hh, HD_FOX:, :] = jnp.dot(sel_k, ar_ref[...], preferred_element_type=F32).astype(BF16)
        vb_scr[hh] = v_ref[:, cols].T.astype(BF16)
    c2 = (HD_FOX ** -0.5) * 1.4426950408889634
    row = lax.broadcasted_iota(jnp.int32, (tq, tq), 0)
    col = lax.broadcasted_iota(jnp.int32, (tq, tq), 1)
    causal = col <= row
    nq = seq // tq
    for pair in range((nq + 1) // 2):
        blocks = sorted({pair, nq - 1 - pair})
        chains = [(hh, qi) for qi in blocks for hh in range(hp)]
        qa = {c: qa_scr[c[0], c[1] * tq:(c[1] + 1) * tq, :] for c in chains}
        s_d = {(hh, qi): jnp.where(causal, jnp.dot(qa[hh, qi], kt_scr[hh, :, qi * tq:(qi + 1) * tq],
                                                   preferred_element_type=F32), -jnp.inf) for hh, qi in chains}
        m = {c: jnp.max(s_d[c], axis=1, keepdims=True) for c in chains}
        early = [c for c in chains if c[1] > 0]
        s_o = {(hh, qi): jnp.dot(qa[hh, qi], kt_scr[hh, :, :qi * tq], preferred_element_type=F32)
               for hh, qi in early}
        for c in early:
            m[c] = jnp.maximum(m[c], jnp.max(s_o[c], axis=1, keepdims=True))
        p_o = {c: jnp.exp2((s_o[c] - m[c]) * c2) for c in early}
        p_d = {c: jnp.exp2((s_d[c] - m[c]) * c2) for c in chains}
        for hh, qi in chains:
            lo, hi = qi * tq, (qi + 1) * tq
            l = jnp.sum(p_d[hh, qi], axis=1, keepdims=True)
            acc = _qk(vb_scr[hh, :, lo:hi], p_d[hh, qi].astype(BF16))
            if qi > 0:
                l = l + jnp.sum(p_o[hh, qi], axis=1, keepdims=True)
                acc = acc + _qk(vb_scr[hh, :, :lo], p_o[hh, qi].astype(BF16))
            o_ref[lo:hi, hh * HD_FOX:(hh + 1) * HD_FOX] = acc.T / l


def _fox_prompt(q, k, v, aug_cols, aug_rows, batch, seq, heads, cast_w):
    m, fw = k.shape
    tq = _tile(seq, 256)
    hp = 2 if heads % 2 == 0 else 1
    grid = (batch, heads // hp)
    head_blk = lambda b, h: (b, h)
    in_specs = [pl.BlockSpec((seq, hp * HD_FOX), head_blk),
                pl.BlockSpec((seq, hp * HD_FOX), head_blk),
                pl.BlockSpec((seq, hp * HD_FOX), head_blk),
                pl.BlockSpec((seq, LANES), lambda b, h: (b, 0)),
                pl.BlockSpec((None, LANES, seq), lambda b, h: (b, 0, 0))]
    args = [q, k, v, aug_cols, aug_rows]
    out_specs = [pl.BlockSpec((seq, hp * HD_FOX), head_blk)]
    out_shape = [jax.ShapeDtypeStruct((m, fw), F32)]
    side = _side_cast_specs(cast_w, grid)
    if side:
        in_specs.append(side[0])
        args.append(cast_w)
        out_specs.append(side[0])
        out_shape.append(side[1])
    outs = pl.pallas_call(
        functools.partial(_fox_prompt_kernel, tq=tq, heads=heads, side=bool(side)),
        grid=grid,
        in_specs=in_specs,
        out_specs=out_specs,
        out_shape=out_shape,
        scratch_shapes=[pltpu.VMEM((hp, seq, 2 * HD_FOX), BF16), pltpu.VMEM((hp, 2 * HD_FOX, seq), BF16),
                        pltpu.VMEM((hp, HD_FOX, seq), BF16)],
        compiler_params=_params("arbitrary", "arbitrary"),
        name="fox_prompt",
    )(*args)
    return outs[0], (outs[1] if side else cast_w.astype(BF16))


def _every_eighth_row(ref, j):
    n, s, d = ref.shape
    return ref.reshape(n * s, d)[pl.ds(j, n, stride=s), :]


def _fox_sample_kernel(*refs, heads):
    ng = heads // SUBLANES
    q_ref = refs[0]
    k_refs = refs[1:1 + ng]
    v_refs = refs[1 + ng:1 + 2 * ng]
    kn_ref, vn_ref, cq_ref, ckc_ref, ckn_ref, o_ref, m_scr, l_scr, acc_scr, s_scr, p_scr = refs[1 + 2 * ng:]
    ki = pl.program_id(1)
    last = pl.num_programs(1) - 1
    scale = HD_FOX ** -0.5
    t = q_ref.shape[0]
    head_cols = lambda h: slice(h * HD_FOX, (h + 1) * HD_FOX)
    head_rows = lambda h: slice(h * t, (h + 1) * t)

    @pl.when(ki == 0)
    def _():
        m_scr[...] = jnp.full(m_scr.shape, -jnp.inf, F32)
        l_scr[...] = jnp.zeros(l_scr.shape, F32)
        acc_scr[...] = jnp.zeros(acc_scr.shape, F32)

    def cached(group_refs, h):
        return _every_eighth_row(group_refs[h // SUBLANES], h % SUBLANES).astype(BF16)

    def update(load_k, load_v, ck_ref, n, causal):
        for h in range(heads):
            s = _qk(q_ref[:, head_cols(h)], load_k(h)) * scale + cq_ref[:, h:h + 1] - ck_ref[h:h + 1, :]
            if causal:
                row = lax.broadcasted_iota(jnp.int32, s.shape, 0)
                col = lax.broadcasted_iota(jnp.int32, s.shape, 1)
                s = jnp.where(col <= row, s, -jnp.inf)
            s_scr[head_rows(h), :n] = s
        s = s_scr[:, :n]
        m_prev = m_scr[:, :1]
        m_new = jnp.maximum(m_prev, jnp.max(s, axis=1, keepdims=True))
        alpha = jnp.exp(m_prev - m_new)
        p = jnp.exp(s - m_new)
        l_new = alpha * l_scr[:, :1] + jnp.sum(p, axis=1, keepdims=True)
        m_scr[...] = jnp.broadcast_to(m_new, m_scr.shape)
        l_scr[...] = jnp.broadcast_to(l_new, l_scr.shape)
        p_scr[:, :n] = p.astype(BF16)
        for h in range(heads):
            pv = jnp.dot(p_scr[head_rows(h), :n], load_v(h), preferred_element_type=F32)
            acc_scr[:, head_cols(h)] = alpha[head_rows(h)] * acc_scr[:, head_cols(h)] + pv

    @pl.when(ki < last)
    def _():
        update(functools.partial(cached, k_refs), functools.partial(cached, v_refs), ckc_ref,
               s_scr.shape[1], causal=False)

    @pl.when(ki == last)
    def _():
        new_rows = lambda ref, h: ref[:, head_cols(h)].astype(BF16)
        update(functools.partial(new_rows, kn_ref), functools.partial(new_rows, vn_ref), ckn_ref, t, causal=True)
        for h in range(heads):
            o_ref[:, head_cols(h)] = acc_scr[:, head_cols(h)] / l_scr[head_rows(h), :1]


def _fox_sample(q, kc, vc, kn, vn, cq, ckc, ckn, batch, t_new, heads):
    past = kc.shape[1]
    fw = heads * HD_FOX
    assert heads % SUBLANES == 0, heads
    ng = heads // SUBLANES
    tk = _tile(past, 1024)
    nkc = past // tk
    kc = kc.reshape(batch, past, ng, SUBLANES, HD_FOX)
    vc = vc.reshape(batch, past, ng, SUBLANES, HD_FOX)
    row_blk = pl.BlockSpec((t_new, fw), lambda b, ki: (b, 0))
    group_blk = [pl.BlockSpec((None, tk, None, SUBLANES, HD_FOX),
                              functools.partial(lambda b, ki, g: (b, jnp.minimum(ki, nkc - 1), g, 0, 0), g=g))
                 for g in range(ng)]
    return pl.pallas_call(
        functools.partial(_fox_sample_kernel, heads=heads),
        grid=(batch, nkc + 1),
        in_specs=[row_blk] + group_blk + group_blk + [
            row_blk, row_blk,
            pl.BlockSpec((t_new, heads), lambda b, ki: (b, 0)),
            pl.BlockSpec((None, heads, tk), lambda b, ki: (b, 0, jnp.minimum(ki, nkc - 1))),
            pl.BlockSpec((None, heads, t_new), lambda b, ki: (b, 0, 0))],
        out_specs=row_blk,
        out_shape=jax.ShapeDtypeStruct((batch * t_new, fw), F32),
        scratch_shapes=[pltpu.VMEM((heads * t_new, LANES), F32), pltpu.VMEM((heads * t_new, LANES), F32),
                        pltpu.VMEM((t_new, fw), F32),
                        pltpu.VMEM((heads * t_new, tk), F32), pltpu.VMEM((heads * t_new, tk), BF16)],
        compiler_params=_params("arbitrary", "arbitrary"),
        name="fox_sample",
    )(q, *([kc] * ng), *([vc] * ng), kn, vn, cq, ckc, ckn)


def _retention_kernel(q_ref, k_ref, v_ref, g_ref, cos_ref, sin_ref, gro_ref, lg_ref, s0_ref, *rest,
                      blk, heads, side):
    if side:
        side_in, r_ref, sout_ref, side_out = rest
        side_out[...] = side_in[...].astype(BF16)
    else:
        r_ref, sout_ref = rest
    seq = q_ref.shape[0]
    half = DK_RET // 2
    ri = lax.broadcasted_iota(jnp.int32, (blk, blk), 0)
    ci = lax.broadcasted_iota(jnp.int32, (blk, blk), 1)
    dist = jnp.abs(ri - ci).astype(F32)
    visible = ci // CHUNK <= ri // CHUNK
    idx = lax.broadcasted_iota(jnp.int32, (blk, 1), 0).astype(F32)
    lgs = [lg_ref[hh, :, :1] for hh in range(heads)]
    decay = [jnp.where(visible, jnp.exp(lg * dist), 0.0) for lg in lgs]
    cross = [jnp.exp(lg * (idx + 1.0)) for lg in lgs]
    k_decay = [jnp.exp(lg * (blk - 1.0 - idx)) for lg in lgs]
    s_decay = [jnp.exp(lg * blk) for lg in lgs]
    state = [s0_ref[hh] for hh in range(heads)]
    for t in range(seq // blk):
        rows = slice(t * blk, (t + 1) * blk)
        cos = cos_ref[rows, :]
        sin = sin_ref[rows, :]
        for hh in range(heads):
            qk_cols = slice(hh * DK_RET, (hh + 1) * DK_RET)
            v_cols = slice(hh * DV_RET, (hh + 1) * DV_RET)
            q = q_ref[rows, qk_cols]
            k = k_ref[rows, qk_cols]
            q = q * cos + pltpu.roll(q, half, axis=1) * sin
            k = (k * cos + pltpu.roll(k, half, axis=1) * sin) * (DK_RET ** -0.5)
            v = v_ref[rows, v_cols]
            qb = q.astype(BF16)
            a = _qk(qb, k.astype(BF16)) * decay[hh]
            o = jnp.dot(a.astype(BF16), v, preferred_element_type=F32)
            o = o + jnp.dot(qb, state[hh].astype(BF16), preferred_element_type=F32) * cross[hh]
            state[hh] = s_decay[hh] * state[hh] + lax.dot_general(
                (k * k_decay[hh]).astype(BF16), v, (((0,), (0,)), ((), ())), preferred_element_type=F32)
            ms = jnp.mean(o * o, axis=-1, keepdims=True)
            g = g_ref[rows, v_cols]
            silu = g / (1.0 + jnp.exp(-g))
            r_ref[rows, v_cols] = (o * lax.rsqrt(ms + EPS) * gro_ref[:, v_cols] * silu).astype(r_ref.dtype)
    for hh in range(heads):
        sout_ref[hh] = state[hh]


def _retention(qkg, qv, cos, sin, g_ret_out, lg, s0, batch, seq, heads, blk, heads_per_step, fox_width,
               cast_w=None):
    m = qkg.shape[0]
    hps = heads_per_step
    groups = heads // hps
    assert fox_width % (hps * DV_RET) == 0, (fox_width, hps)
    v_off = fox_width // (hps * DV_RET)
    grid = (batch, groups)
    row = lambda b, h: (b, h)
    state = pl.BlockSpec((None, hps, DK_RET, DV_RET), lambda b, h: (b, h, 0, 0))
    in_specs = [pl.BlockSpec((seq, hps * DK_RET), row),
                pl.BlockSpec((seq, hps * DK_RET), lambda b, h: (b, groups + h)),
                pl.BlockSpec((seq, hps * DV_RET), lambda b, h: (b, v_off + h)),
                pl.BlockSpec((seq, hps * DV_RET), lambda b, h: (b, groups + h)),
                pl.BlockSpec((seq, DK_RET), lambda b, h: (0, 0)),
                pl.BlockSpec((seq, DK_RET), lambda b, h: (0, 0)),
                pl.BlockSpec((1, hps * DV_RET), lambda b, h: (0, h)),
                pl.BlockSpec((hps, 1, LANES), lambda b, h: (h, 0, 0)),
                state]
    args = [qkg, qkg, qv, qkg, cos, sin, g_ret_out.reshape(1, -1), lg, s0]
    out_specs = [pl.BlockSpec((seq, hps * DV_RET), row), state]
    out_shape = [jax.ShapeDtypeStruct((m, heads * DV_RET), BF16),
                 jax.ShapeDtypeStruct((batch, heads, DK_RET, DV_RET), F32)]
    side = _side_cast_specs(cast_w, grid) if cast_w is not None else None
    if side:
        in_specs.append(side[0])
        args.append(cast_w)
        out_specs.append(side[0])
        out_shape.append(side[1])
    outs = pl.pallas_call(
        functools.partial(_retention_kernel, blk=blk, heads=hps, side=bool(side)),
        grid=grid,
        in_specs=in_specs,
        out_specs=out_specs,
        out_shape=out_shape,
        compiler_params=_params("arbitrary", "arbitrary"),
        name="retention",
    )(*args)
    if cast_w is None:
        return outs[0], outs[1]
    return outs[0], outs[1], (outs[2] if side else cast_w.astype(BF16))


def _rotary_tables(pos):
    half = DK_RET // 2
    inv = ROPE_BASE ** (-jnp.arange(half, dtype=F32) / half)
    ang = pos.astype(F32)[:, None] * inv[None, :]
    cos = jnp.cos(ang)
    sin = jnp.sin(ang)
    return jnp.concatenate([cos, cos], axis=-1), jnp.concatenate([-sin, sin], axis=-1)


def _feature_rows_kernel(w_ref, o_ref, *, valid):
    x = w_ref[...]
    if valid < x.shape[0]:
        row = lax.broadcasted_iota(jnp.int32, x.shape, 0)
        x = jnp.where(row < valid, x, 0.0)
    o_ref[...] = x.T.astype(BF16)


def _feature_rows(w_t, layer, segments, rows_per_step, valid=None):
    _, _, d = w_t.shape
    rb = rows_per_step
    starts = [s + k * rb for s, n in segments for k in range(n // rb)]
    assert all(n % rb == 0 for _, n in segments) and all(s % SUBLANES == 0 for s in starts), segments

    def index_map(i):
        tile_row = starts[-1] // SUBLANES
        for k in range(len(starts) - 2, -1, -1):
            tile_row = jnp.where(i <= k, starts[k] // SUBLANES, tile_row)
        return (layer, tile_row * SUBLANES, 0)

    return pl.pallas_call(
        functools.partial(_feature_rows_kernel, valid=rb if valid is None else valid),
        grid=(len(starts),),
        in_specs=[pl.BlockSpec((None, pl.Element(rb), pl.Element(d)), index_map)],
        out_specs=pl.BlockSpec((d, rb), lambda i: (0, i)),
        out_shape=jax.ShapeDtypeStruct((d, rb * len(starts)), BF16),
        compiler_params=_params("arbitrary"),
        name="w_in_rows",
    )(w_t)


def _in_proj_weights(w_in, layer, b_forget, h_fox, h_ret):
    fw = h_fox * HD_FOX
    qkw = h_ret * DK_RET
    vw = h_ret * DV_RET
    o1, o2, o3 = fw, 2 * fw, 3 * fw
    o4 = o3 + h_fox
    o6 = o4 + 2 * qkw
    o7 = o6 + vw
    w_t = jnp.swapaxes(w_in, 1, 2)
    groups = dict(qv=[(0, fw), (o6, vw)], k_f=[(o1, fw)], v_f=[(o2, fw)], qkg=[(o4, 2 * qkw), (o7, vw)])
    cols, col0 = {}, 0
    for name, segs in groups.items():
        width = sum(n for _, n in segs)
        cols[name] = (col0, width)
        col0 += width
    return dict(
        w=_feature_rows(w_t, layer, [seg for segs in groups.values() for seg in segs], 512), cols=cols,
        logf=_feature_rows(w_t, layer, [(o3, LANES)], LANES, valid=h_fox),
        b_logf=jnp.pad(b_forget, (0, LANES - h_fox)).reshape(1, LANES).astype(F32))


def _project(x2d, g_attn, w):
    h, logf, k_f = _rmsnorm(x2d, g_attn, BF16, heads=(w["logf"], w["b_logf"], w["w"], w["cols"]["k_f"]))
    mm = lambda name, dtype: _matmul(h, w["w"], dtype, w_cols=w["cols"][name])
    return dict(qv=mm("qv", BF16), k_f=k_f, v_f=mm("v_f", F32), qkg=mm("qkg", F32), logf=logf)


def _finish_layer(x2d, o_fox, r, g_fox_out, g_mlp, w_out_b, w_up_b, w_down):
    x1, h2 = _out_proj(o_fox, g_fox_out, r, w_out_b, x2d, g_mlp)
    if w_down.dtype == BF16:
        u = _matmul(h2, w_up_b, BF16, epilogue="relu2")
    else:
        u, w_down = _matmul(h2, w_up_b, BF16, epilogue="relu2", cast_w=w_down)
    return _matmul_residual_ksplit(u, w_down, x1), w_down


def _retention_log_gamma(h_ret):
    lg = jnp.log1p(-jnp.exp2(-5.0 - jnp.arange(h_ret, dtype=F32)))
    return jnp.broadcast_to(lg[:, None, None], (h_ret, 1, LANES))


def kernel(x_prompt, x_sample, cache_fox_k, cache_fox_v, cache_fox_logf, state_ret,
           g_attn, w_in, b_forget, g_fox_out, g_ret_out, w_out, g_mlp, w_up, w_down, g_final):
    bp, seq, d = x_prompt.shape
    bs, t_new, _ = x_sample.shape
    depth, _, past, h_fox, _ = cache_fox_k.shape
    h_ret = state_ret.shape[2]
    fw = h_fox * HD_FOX
    dtype = x_prompt.dtype

    lg = _retention_log_gamma(h_ret)
    cos_p, sin_p = _rotary_tables(jnp.arange(seq))
    cos_s, sin_s = _rotary_tables(past + jnp.arange(t_new))
    blk_p = _tile(seq, 256, CHUNK)

    yp = x_prompt.reshape(bp * seq, d)
    ys = x_sample.reshape(bs * t_new, d)
    outs = [[] for _ in range(8)]
    for l in range(depth):
        w = _in_proj_weights(w_in, l, b_forget[l], h_fox, h_ret)

        p = _project(yp, g_attn[l], w)
        logf_p = p["logf"][:, :h_fox]
        _, aug_rows = _cumsum_lanes(jnp.swapaxes(logf_p.reshape(bp, seq, h_fox), 1, 2),
                                    bias_scale=HD_FOX ** 0.5)
        aug_cols = jnp.swapaxes(aug_rows, 1, 2).reshape(bp * seq, LANES)
        o_fox, w_up_b = _fox_prompt(p["qv"], p["k_f"], p["v_f"], aug_cols, aug_rows, bp, seq, h_fox, w_up[l])
        s0 = jnp.zeros((bp, h_ret, DK_RET, DV_RET), F32)
        r, s_fin, w_out_b = _retention(p["qkg"], p["qv"], cos_p, sin_p, g_ret_out[l], lg,
                                       s0, bp, seq, h_ret, blk_p, 2 if h_ret % 2 == 0 else 1, fw,
                                       cast_w=w_out[l])
        yp, w_down_b = _finish_layer(yp, o_fox, r, g_fox_out[l], g_mlp[l], w_out_b, w_up_b, w_down[l])
        outs[0].append(p["k_f"].reshape(bp, seq, h_fox, HD_FOX))
        outs[1].append(p["v_f"].reshape(bp, seq, h_fox, HD_FOX))
        outs[2].append(logf_p.reshape(bp, seq, h_fox).astype(dtype))
        outs[3].append(s_fin.astype(dtype))

        p = _project(ys, g_attn[l], w)
        logf_s = p["logf"][:, :h_fox]
        total = past + t_new
        padded = -(-total // LANES) * LANES
        logf_all = jnp.concatenate(
            [jnp.swapaxes(cache_fox_logf[l].astype(F32), 1, 2),
             jnp.swapaxes(logf_s.reshape(bs, t_new, h_fox), 1, 2),
             jnp.zeros((bs, h_fox, padded - total), F32)], axis=2)
        c_all = _cumsum_lanes(logf_all)
        ckc = c_all[:, :, :past]
        ckn = c_all[:, :, past:total]
        cq = jnp.swapaxes(ckn, 1, 2).reshape(bs * t_new, h_fox)
        o_fox = _fox_sample(p["qv"], cache_fox_k[l], cache_fox_v[l], p["k_f"], p["v_f"],
                            cq, ckc, ckn, bs, t_new, h_fox)
        r, s_new = _retention(p["qkg"], p["qv"], cos_s, sin_s, g_ret_out[l], lg,
                              state_ret[l].astype(F32), bs, t_new, h_ret, t_new, h_ret, fw)
        ys, _ = _finish_layer(ys, o_fox, r, g_fox_out[l], g_mlp[l], w_out_b, w_up_b, w_down_b)
        outs[4].append(p["k_f"].reshape(bs, t_new, h_fox, HD_FOX))
        outs[5].append(p["v_f"].reshape(bs, t_new, h_fox, HD_FOX))
        outs[6].append(logf_s.reshape(bs, t_new, h_fox).astype(dtype))
        outs[7].append(s_new.astype(dtype))

    y_prompt = _rmsnorm(yp, g_final, dtype).reshape(bp, seq, d)
    y_sample = _rmsnorm(ys, g_final, dtype).reshape(bs, t_new, d)
    return (y_prompt, y_sample) + tuple(jnp.stack(o) for o in outs)
```
